```python
import math
import jax
import jax.numpy as jnp
from jax import lax
import numpy as np

D_MODEL = 1024
BATCH = 8
SEQ = 4096
DEPTH = 2
DEC_BATCH = 32
DEC_SEQ = 4
PAST_LEN = 16384
PAGE_SIZE = 128

RMS_EPS = 1e-6
CONV_W = 4
SSD_HEADS = 16
SSD_HEADDIM = 64
SSD_INNER = SSD_HEADS * SSD_HEADDIM
SSD_STATE = 128
SSD_GROUPS = 2
SSD_CHUNK = 64
SSD_CONV_DIM = SSD_INNER + 2 * SSD_GROUPS * SSD_STATE
GDN_QK_HEADS = 4
GDN_V_HEADS = 8
GDN_HEAD_DIM = 128
GDN_CHUNK = 64
GDN_QK_DIM = GDN_QK_HEADS * GDN_HEAD_DIM
GDN_V_DIM = GDN_V_HEADS * GDN_HEAD_DIM
GDN_CONV_DIM = 2 * GDN_QK_DIM + GDN_V_DIM
AB_IN_SIZES = (SSD_INNER, SSD_CONV_DIM, SSD_HEADS, GDN_CONV_DIM, GDN_V_DIM, GDN_V_HEADS, GDN_V_HEADS)
AB_IN_DIM = SSD_INNER + SSD_CONV_DIM + SSD_HEADS + GDN_CONV_DIM + GDN_V_DIM + 2 * GDN_V_HEADS
AB_MIX_DIM = SSD_INNER + GDN_V_DIM
NSA_HEADS = 16
NSA_KV_HEADS = 4
NSA_GROUP = NSA_HEADS // NSA_KV_HEADS
NSA_HEAD_DIM = 64
CMP_STRIDE = 16
CMP_LEN = 2 * CMP_STRIDE
CMP_HIDDEN = 64
SLC_BLOCK = 64
SLC_TOPN = 16
WINDOW = 512
NSA_QBLOCK = 16
FORCE_BONUS = 1e4
NSA_KV_DIM = 2 * NSA_KV_HEADS * NSA_HEAD_DIM
NSA_IN_SIZES = (NSA_HEADS * NSA_HEAD_DIM, NSA_KV_DIM, NSA_KV_DIM, NSA_KV_DIM, 3 * NSA_HEADS)
NSA_IN_DIM = NSA_HEADS * NSA_HEAD_DIM + 3 * NSA_KV_DIM + 3 * NSA_HEADS
MLP_HIDDEN = 4 * D_MODEL
NEG_INF = -1e30

kernel_name = 'hybrid_ssd_gdn_nsa_decode_step'


def rms_unit(x):
    xf = x.astype(jnp.float32)
    return xf * lax.rsqrt(jnp.mean(xf * xf, axis=-1, keepdims=True) + RMS_EPS)


def rmsnorm(x, w):
    return rms_unit(x) * w.astype(jnp.float32)


def l2norm(x):
    return x * lax.rsqrt(jnp.sum(x * x, axis=-1, keepdims=True) + 1e-6)


def masked_softmax(s, mask):
    p = jax.nn.softmax(jnp.where(mask, s.astype(jnp.float32), NEG_INF), axis=-1)
    return jnp.where(mask, p, 0.0)


def split_last(x, sizes):
    idx, acc = [], 0
    for s in sizes[:-1]:
        acc += s
        idx.append(acc)
    return jnp.split(x, idx, axis=-1)


def pad_time(a, length):
    extra = length - a.shape[1]
    if extra == 0:
        return a
    return jnp.pad(a, [(0, 0), (0, extra)] + [(0, 0)] * (a.ndim - 2))


def causal_dwconv(x, prev, w, b):
    L = x.shape[1]
    xp = jnp.concatenate([prev.astype(x.dtype), x], axis=1)
    y = sum(xp[:, k:k + L] * w[k] for k in range(CONV_W))
    if b is not None:
        y = y + b
    return y, xp[:, L:]


def ssd_chunked(x, dt, a, bm, cm, state0):
    Bsz, L, H, P = x.shape
    G, N = bm.shape[2], bm.shape[3]
    R = H // G
    C = min(SSD_CHUNK, L)
    nc = -(-L // C)
    Lp = nc * C
    xc = pad_time(x, Lp).reshape(Bsz, nc, C, G, R, P)
    dtc = pad_time(dt, Lp).reshape(Bsz, nc, C, G, R)
    bc = pad_time(bm, Lp).reshape(Bsz, nc, C, G, N)
    cc = pad_time(cm, Lp).reshape(Bsz, nc, C, G, N)
    cs = jnp.cumsum(dtc * a.reshape(G, R), axis=2)
    causal = jnp.tril(jnp.ones((C, C), bool))[None, None, :, :, None, None]
    seg = cs[:, :, :, None] - cs[:, :, None, :]
    decay = jnp.where(causal, jnp.exp(jnp.where(causal, seg, 0.0)), 0.0)
    cb = jnp.einsum('bcign,bcjgn->bcijg', cc, bc)
    scores = cb[..., None] * decay * dtc[:, :, None]
    y_diag = jnp.einsum('bcijgr,bcjgrp->bcigrp', scores, xc)
    to_end = jnp.exp(cs[:, :, -1:] - cs) * dtc
    chunk_states = jnp.einsum('bcjgn,bcjgr,bcjgrp->bcgrpn', bc, to_end, xc)
    chunk_decay = jnp.exp(cs[:, :, -1])

    def step(s, inp):
        st, dec = inp
        return s * dec[..., None, None] + st, s

    s0 = state0.astype(jnp.float32).reshape(Bsz, G, R, P, N)
    s_fin, s_prev = lax.scan(step, s0, (jnp.moveaxis(chunk_states, 1, 0), jnp.moveaxis(chunk_decay, 1, 0)))
    s_prev = jnp.moveaxis(s_prev, 0, 1)
    y_off = jnp.einsum('bcign,bcgrpn,bcigr->bcigrp', cc, s_prev, jnp.exp(cs))
    y = (y_diag + y_off).reshape(Bsz, Lp, H, P)[:, :L]
    return y, s_fin.reshape(Bsz, H, P, N)


def gdn_chunked(q, k, v, g, beta, state0):
    Bsz, L, H, K = q.shape
    C = min(GDN_CHUNK, L)
    nc = -(-L // C)
    Lp = nc * C

    def chunks(t):
        t = pad_time(t, Lp)
        return jnp.moveaxis(t.reshape((Bsz, nc, C) + t.shape[2:]), 3, 2)

    qc, kc, vc, gc, bc = chunks(q), chunks(k), chunks(v), chunks(g), chunks(beta)
    cs = jnp.cumsum(gc, axis=-1)
    causal = jnp.tril(jnp.ones((C, C), bool))
    strict = jnp.tril(jnp.ones((C, C), bool), -1)
    seg = cs[..., :, None] - cs[..., None, :]
    decay = jnp.where(causal, jnp.exp(jnp.where(causal, seg, 0.0)), 0.0)
    kb = kc * bc[..., None]
    a_mat = jnp.where(strict, jnp.einsum('bnhik,bnhjk->bnhij', kb, kc) * decay, 0.0)
    t_mat = a_mat + jnp.eye(C, dtype=a_mat.dtype)
    u = lax.linalg.triangular_solve(t_mat, vc * bc[..., None], left_side=True, lower=True, unit_diagonal=True)
    wk = lax.linalg.triangular_solve(t_mat, kb * jnp.exp(cs)[..., None], left_side=True, lower=True, unit_diagonal=True)
    qk = jnp.where(causal, jnp.einsum('bnhik,bnhjk->bnhij', qc, kc) * decay, 0.0)
    q_dec = qc * jnp.exp(cs)[..., None]
    k_dec = kc * jnp.exp(cs[..., -1:] - cs)[..., None]
    tot = jnp.exp(cs[..., -1])

    def step(s, inp):
        u_c, w_c, qd_c, kd_c, qk_c, tot_c = inp
        v_new = u_c - jnp.einsum('bhck,bhkv->bhcv', w_c, s)
        o = jnp.einsum('bhck,bhkv->bhcv', qd_c, s) + jnp.einsum('bhij,bhjv->bhiv', qk_c, v_new)
        s = s * tot_c[..., None, None] + jnp.einsum('bhck,bhcv->bhkv', kd_c, v_new)
        return s, o

    xs = tuple(jnp.moveaxis(t, 1, 0) for t in (u, wk, q_dec, k_dec, qk, tot))
    s_fin, o = lax.scan(step, state0.astype(jnp.float32), xs)
    o = jnp.transpose(o, (1, 0, 3, 2, 4)).reshape(Bsz, Lp, H, v.shape[-1])[:, :L]
    return o, s_fin


def mixer_ab(h, ssd_state0, ssd_conv0, gdn_state0, gdn_conv0, w):
    Bsz, L, _ = h.shape
    proj = h @ w['w_in_ab']
    z_ssd, xbc, dt_raw, qkv, z_gdn, b_raw, a_raw = split_last(proj, AB_IN_SIZES)
    xbc, ssd_conv_new = causal_dwconv(xbc, ssd_conv0, w['ssd_conv_w'], w['ssd_conv_b'])
    xbc = jax.nn.silu(xbc)
    xs, bm, cm = split_last(xbc, (SSD_INNER, SSD_GROUPS * SSD_STATE, SSD_GROUPS * SSD_STATE))
    xs = xs.reshape(Bsz, L, SSD_HEADS, SSD_HEADDIM)
    dt = jax.nn.softplus(dt_raw + w['ssd_dt_bias'])
    a = -jnp.exp(w['ssd_a_log'].astype(jnp.float32))
    y_ssd, ssd_state_new = ssd_chunked(xs, dt, a, bm.reshape(Bsz, L, SSD_GROUPS, SSD_STATE),
                                       cm.reshape(Bsz, L, SSD_GROUPS, SSD_STATE), ssd_state0)
    y_ssd = y_ssd + w['ssd_d'][:, None] * xs
    yz = (y_ssd.reshape(Bsz, L, SSD_INNER) * jax.nn.silu(z_ssd)).reshape(Bsz, L, SSD_GROUPS, SSD_INNER // SSD_GROUPS)
    y_ssd = rms_unit(yz).reshape(Bsz, L, SSD_INNER) * w['ssd_norm_w']
    qkv, gdn_conv_new = causal_dwconv(qkv, gdn_conv0, w['gdn_conv_w'], None)
    qkv = jax.nn.silu(qkv)
    q, k, v = split_last(qkv, (GDN_QK_DIM, GDN_QK_DIM, GDN_V_DIM))
    rep = GDN_V_HEADS // GDN_QK_HEADS
    q = jnp.repeat(l2norm(q.reshape(Bsz, L, GDN_QK_HEADS, GDN_HEAD_DIM)), rep, axis=2) * GDN_HEAD_DIM ** -0.5
    k = jnp.repeat(l2norm(k.reshape(Bsz, L, GDN_QK_HEADS, GDN_HEAD_DIM)), rep, axis=2)
    v = v.reshape(Bsz, L, GDN_V_HEADS, GDN_HEAD_DIM)
    beta = jax.nn.sigmoid(b_raw)
    g = -jnp.exp(w['gdn_a_log'].astype(jnp.float32)) * jax.nn.softplus(a_raw + w['gdn_dt_bias'])
    y_gdn, gdn_state_new = gdn_chunked(q, k, v, g, beta, gdn_state0)
    y_gdn = rmsnorm(y_gdn, w['gdn_norm_w']) * jax.nn.silu(z_gdn.reshape(Bsz, L, GDN_V_HEADS, GDN_HEAD_DIM))
    y = jnp.concatenate([y_ssd, y_gdn.reshape(Bsz, L, GDN_V_DIM)], axis=-1) @ w['w_out_ab']
    return y, ssd_state_new, ssd_conv_new, gdn_state_new, gdn_conv_new


def compress(kv, pe, w1, b1, w2):
    Bsz, L = kv.shape[:2]
    n_chunks = L // CMP_STRIDE
    chunks = kv[:, :n_chunks * CMP_STRIDE].reshape(Bsz, n_chunks, CMP_STRIDE, NSA_KV_HEADS, NSA_HEAD_DIM)
    w1r = w1.reshape(2, CMP_STRIDE, NSA_HEAD_DIM, CMP_HIDDEN)
    lead = jnp.einsum('bcjkd,jdh->bckh', chunks, w1r[0])
    tail = jnp.einsum('bcjkd,jdh->bckh', chunks, w1r[1])
    pe_term = jnp.einsum('ld,ldh->h', pe, w1.reshape(CMP_LEN, NSA_HEAD_DIM, CMP_HIDDEN)) + b1
    hid = jax.nn.silu(lead[:, :-1] + tail[:, 1:] + pe_term)
    return jnp.einsum('bnkh,hd->bnkd', hid, w2)


def mixer_nsa(h, past_cmp, past_slc, past_win, w):
    Bsz, Lq, _ = h.shape
    P0 = past_cmp.shape[1]
    L_tot = P0 + Lq
    proj = h @ w['w_in_nsa']
    q, kv_cmp, kv_slc, kv_win, gate_raw = split_last(proj, NSA_IN_SIZES)
    q = q.reshape(Bsz, Lq, NSA_KV_HEADS, NSA_GROUP, NSA_HEAD_DIM) * NSA_HEAD_DIM ** -0.5
    kv_shape = (Bsz, Lq, 2, NSA_KV_HEADS, NSA_HEAD_DIM)
    kv_cmp, kv_slc, kv_win = kv_cmp.reshape(kv_shape), kv_slc.reshape(kv_shape), kv_win.reshape(kv_shape)
    gates = jax.nn.sigmoid(gate_raw).reshape(Bsz, Lq, 3, NSA_KV_HEADS, NSA_GROUP)
    cmp_all = jnp.concatenate([past_cmp.astype(jnp.float32), kv_cmp], axis=1)
    slc_all = jnp.concatenate([past_slc.astype(jnp.float32), kv_slc], axis=1)
    win_all = jnp.concatenate([past_win.astype(jnp.float32), kv_win], axis=1)
    k_cmp = compress(cmp_all[:, :, 0], w['cmp_pe_k'], w['cmp_w1_k'], w['cmp_b1_k'], w['cmp_w2_k'])
    v_cmp = compress(cmp_all[:, :, 1], w['cmp_pe_v'], w['cmp_w1_v'], w['cmp_b1_v'], w['cmp_w2_v'])
    n_cmp = k_cmp.shape[1]
    c_start = jnp.arange(n_cmp) * CMP_STRIDE
    cmp_end = c_start + CMP_LEN - 1
    n_slc = -(-L_tot // SLC_BLOCK)
    slc_blocks = pad_time(slc_all, n_slc * SLC_BLOCK).reshape(Bsz, n_slc, SLC_BLOCK, 2, NSA_KV_HEADS, NSA_HEAD_DIM)
    slc_blocks = jnp.transpose(slc_blocks, (0, 4, 1, 2, 3, 5))
    s_start = jnp.arange(n_slc) * SLC_BLOCK
    overlap = ((c_start[:, None] < s_start[None] + SLC_BLOCK) & (c_start[:, None] + CMP_LEN > s_start[None])).astype(jnp.float32)
    k_sel = min(SLC_TOPN, n_slc)
    win_keys = jnp.concatenate([jnp.zeros((Bsz, WINDOW - past_win.shape[1], 2, NSA_KV_HEADS, NSA_HEAD_DIM), jnp.float32), win_all], axis=1)
    qb_len = NSA_QBLOCK if Lq % NSA_QBLOCK == 0 else Lq
    n_qb = Lq // qb_len
    q_blocks = jnp.moveaxis(q.reshape(Bsz, n_qb, qb_len, NSA_KV_HEADS, NSA_GROUP, NSA_HEAD_DIM), 1, 0)
    g_blocks = jnp.moveaxis(gates.reshape(Bsz, n_qb, qb_len, 3, NSA_KV_HEADS, NSA_GROUP), 1, 0)
    b_ix = jnp.arange(Bsz)[:, None, None, None]
    kv_ix = jnp.arange(NSA_KV_HEADS)[None, :, None, None]
    blk_id = jnp.arange(n_slc)

    def attend_block(args):
        qb, gb, blk = args
        t = P0 + blk * qb_len + jnp.arange(qb_len)
        p_c = masked_softmax(jnp.einsum('bqkgd,bnkd->bkgqn', qb, k_cmp), cmp_end[None, :] <= t[:, None])
        o_c = jnp.einsum('bkgqn,bnkd->bqkgd', p_c, v_cmp)
        imp = jnp.einsum('bkgqn,ns->bkqs', p_c, overlap)
        cur = (t // SLC_BLOCK)[:, None]
        forced = (blk_id == 0) | (blk_id == cur) | (blk_id == cur - 1)
        causal_blk = blk_id * SLC_BLOCK <= t[:, None]
        score = jnp.where(causal_blk, imp + jnp.where(forced, FORCE_BONUS, 0.0), NEG_INF)
        _, idx = lax.top_k(score, k_sel)
        sel = slc_blocks[b_ix, kv_ix, idx]
        kpos = idx[..., None] * SLC_BLOCK + jnp.arange(SLC_BLOCK)
        valid = (kpos <= t[None, None, :, None, None]).reshape(Bsz, NSA_KV_HEADS, 1, qb_len, k_sel * SLC_BLOCK)
        s_s = jnp.einsum('bqkgd,bkqsjd->bkgqsj', qb, sel[..., 0, :]).reshape(Bsz, NSA_KV_HEADS, NSA_GROUP, qb_len, k_sel * SLC_BLOCK)
        p_s = masked_softmax(s_s, valid).reshape(Bsz, NSA_KV_HEADS, NSA_GROUP, qb_len, k_sel, SLC_BLOCK)
        o_s = jnp.einsum('bkgqsj,bkqsjd->bqkgd', p_s, sel[..., 1, :])
        kw = lax.dynamic_slice_in_dim(win_keys, blk * qb_len, qb_len + WINDOW, axis=1)
        wpos = P0 - WINDOW + blk * qb_len + jnp.arange(qb_len + WINDOW)
        dist = t[:, None] - wpos[None, :]
        valid_w = (wpos[None, :] >= 0) & (dist >= 0) & (dist < WINDOW)
        p_w = masked_softmax(jnp.einsum('bqkgd,bjkd->bkgqj', qb, kw[:, :, 0]), valid_w)
        o_w = jnp.einsum('bkgqj,bjkd->bqkgd', p_w, kw[:, :, 1])
        return gb[:, :, 0][..., None] * o_c + gb[:, :, 1][..., None] * o_s + gb[:, :, 2][..., None] * o_w

    out = lax.map(attend_block, (q_blocks, g_blocks, jnp.arange(n_qb)))
    out = jnp.moveaxis(out, 0, 1).reshape(Bsz, Lq, NSA_HEADS * NSA_HEAD_DIM) @ w['w_out_nsa']
    keep = min(WINDOW, win_all.shape[1])
    return out, kv_cmp, kv_slc, win_all[:, win_all.shape[1] - keep:]


def trunk(x, ssd_state0, ssd_conv0, gdn_state0, gdn_conv0, past_cmp, past_slc, past_win, w):
    for layer in range(DEPTH):
        h = rmsnorm(x, w['norm_mix'][layer])
        if layer % 2 == 0:
            mix, ssd_state, ssd_conv, gdn_state, gdn_conv = mixer_ab(h, ssd_state0, ssd_conv0, gdn_state0, gdn_conv0, w)
        else:
            mix, cmp_rows, slc_rows, win_rows = mixer_nsa(h, past_cmp, past_slc, past_win, w)
        x = x + mix
        h = rmsnorm(x, w['norm_mlp'][layer])
        x = x + jnp.square(jax.nn.relu(h @ w['w_up'][layer])) @ w['w_down'][layer]
    y = rmsnorm(x, w['norm_final'])
    return y, ssd_state, ssd_conv, gdn_state, gdn_conv, cmp_rows, slc_rows, win_rows


def setup_inputs(seed: int = 0) -> dict:
    key = jax.random.key(seed)
    ks = jax.random.split(key, 37)
    n_pages = PAST_LEN // PAGE_SIZE
    n_pool = (DEC_BATCH * n_pages * 5) // 4
    win_buf = min(WINDOW, PAST_LEN)
    kvh = (2, NSA_KV_HEADS, NSA_HEAD_DIM)

    def nrm(k, shape, scale=1.0):
        return jax.random.normal(k, shape, jnp.float32) * scale

    def gain(k, shape):
        return 1.0 + 0.01 * jax.random.normal(k, shape, jnp.float32)

    def dt_bias(k, n):
        dt = jnp.exp(jax.random.uniform(k, (n,), jnp.float32, math.log(1e-3), math.log(1e-1)))
        return dt + jnp.log(-jnp.expm1(-dt))

    def a_log(k, n):
        return jnp.log(jax.random.uniform(k, (n,), jnp.float32, 1.0, 16.0))

    page_table = jax.random.permutation(ks[9], n_pool)[:DEC_BATCH * n_pages].reshape(DEC_BATCH, n_pages).astype(jnp.int32)
    cmp_in = CMP_LEN * NSA_HEAD_DIM
    return {
        'x_prompt': nrm(ks[0], (BATCH, SEQ, D_MODEL)),
        'x_sample': nrm(ks[1], (DEC_BATCH, DEC_SEQ, D_MODEL)),
        'state_ssd': nrm(ks[2], (DEC_BATCH, SSD_HEADS, SSD_HEADDIM, SSD_STATE), 0.1),
        'state_ssd_conv': nrm(ks[3], (DEC_BATCH, CONV_W - 1, SSD_CONV_DIM)),
        'state_gdn': nrm(ks[4], (DEC_BATCH, GDN_V_HEADS, GDN_HEAD_DIM, GDN_HEAD_DIM), 0.1),
        'state_gdn_conv': nrm(ks[5], (DEC_BATCH, CONV_W - 1, GDN_CONV_DIM)),
        'cache_cmp_kv': nrm(ks[6], (n_pool, PAGE_SIZE) + kvh),
        'cache_slc_kv': nrm(ks[7], (n_pool, PAGE_SIZE) + kvh),
        'state_win_kv': nrm(ks[8], (DEC_BATCH, win_buf) + kvh),
        'page_table': page_table,
        'norm_mix': gain(ks[10], (DEPTH, D_MODEL)),
        'norm_mlp': gain(ks[11], (DEPTH, D_MODEL)),
        'norm_final': gain(ks[12], (D_MODEL,)),
        'w_in_ab': nrm(ks[13], (D_MODEL, AB_IN_DIM), D_MODEL ** -0.5),
        'ssd_conv_w': nrm(ks[14], (CONV_W, SSD_CONV_DIM), CONV_W ** -0.5),
        'ssd_conv_b': nrm(ks[15], (SSD_CONV_DIM,), 0.01),
        'ssd_dt_bias': dt_bias(ks[16], SSD_HEADS),
        'ssd_a_log': a_log(ks[17], SSD_HEADS),
        'ssd_d': gain(ks[18], (SSD_HEADS,)),
        'ssd_norm_w': gain(ks[19], (SSD_INNER,)),
        'gdn_conv_w': nrm(ks[20], (CONV_W, GDN_CONV_DIM), CONV_W ** -0.5),
        'gdn_dt_bias': dt_bias(ks[21], GDN_V_HEADS),
        'gdn_a_log': a_log(ks[22], GDN_V_HEADS),
        'gdn_norm_w': gain(ks[23], (GDN_HEAD_DIM,)),
        'w_out_ab': nrm(ks[24], (AB_MIX_DIM, D_MODEL), AB_MIX_DIM ** -0.5),
        'w_in_nsa': nrm(ks[25], (D_MODEL, NSA_IN_DIM), D_MODEL ** -0.5),
        'cmp_pe_k': nrm(ks[26], (CMP_LEN, NSA_HEAD_DIM), 0.1),
        'cmp_w1_k': nrm(ks[27], (cmp_in, CMP_HIDDEN), cmp_in ** -0.5),
        'cmp_b1_k': nrm(ks[28], (CMP_HIDDEN,), 0.01),
        'cmp_w2_k': nrm(ks[29], (CMP_HIDDEN, NSA_HEAD_DIM), CMP_HIDDEN ** -0.5),
        'cmp_pe_v': nrm(ks[30], (CMP_LEN, NSA_HEAD_DIM), 0.1),
        'cmp_w1_v': nrm(ks[31], (cmp_in, CMP_HIDDEN), cmp_in ** -0.5),
        'cmp_b1_v': nrm(ks[32], (CMP_HIDDEN,), 0.01),
        'cmp_w2_v': nrm(ks[33], (CMP_HIDDEN, NSA_HEAD_DIM), CMP_HIDDEN ** -0.5),
        'w_out_nsa': nrm(ks[34], (NSA_HEADS * NSA_HEAD_DIM, D_MODEL), (NSA_HEADS * NSA_HEAD_DIM) ** -0.5),
        'w_up': nrm(ks[35], (DEPTH, D_MODEL, MLP_HIDDEN), D_MODEL ** -0.5),
        'w_down': nrm(ks[36], (DEPTH, MLP_HIDDEN, D_MODEL), MLP_HIDDEN ** -0.5),
    }


def reference(x_prompt, x_sample, state_ssd, state_ssd_conv, state_gdn, state_gdn_conv,
              cache_cmp_kv, cache_slc_kv, state_win_kv, page_table,
              norm_mix, norm_mlp, norm_final, w_in_ab, ssd_conv_w, ssd_conv_b, ssd_dt_bias,
              ssd_a_log, ssd_d, ssd_norm_w, gdn_conv_w, gdn_dt_bias, gdn_a_log, gdn_norm_w,
              w_out_ab, w_in_nsa, cmp_pe_k, cmp_w1_k, cmp_b1_k, cmp_w2_k,
              cmp_pe_v, cmp_w1_v, cmp_b1_v, cmp_w2_v, w_out_nsa, w_up, w_down):
    w = {'norm_mix': norm_mix, 'norm_mlp': norm_mlp, 'norm_final': norm_final,
         'w_in_ab': w_in_ab, 'ssd_conv_w': ssd_conv_w, 'ssd_conv_b': ssd_conv_b,
         'ssd_dt_bias': ssd_dt_bias, 'ssd_a_log': ssd_a_log, 'ssd_d': ssd_d, 'ssd_norm_w': ssd_norm_w,
         'gdn_conv_w': gdn_conv_w, 'gdn_dt_bias': gdn_dt_bias, 'gdn_a_log': gdn_a_log,
         'gdn_norm_w': gdn_norm_w, 'w_out_ab': w_out_ab, 'w_in_nsa': w_in_nsa,
         'cmp_pe_k': cmp_pe_k, 'cmp_w1_k': cmp_w1_k, 'cmp_b1_k': cmp_b1_k, 'cmp_w2_k': cmp_w2_k,
         'cmp_pe_v': cmp_pe_v, 'cmp_w1_v': cmp_w1_v, 'cmp_b1_v': cmp_b1_v, 'cmp_w2_v': cmp_w2_v,
         'w_out_nsa': w_out_nsa, 'w_up': w_up, 'w_down': w_down}
    f32 = jnp.float32
    bp = x_prompt.shape[0]
    empty_kv = jnp.zeros((bp, 0, 2, NSA_KV_HEADS, NSA_HEAD_DIM), f32)
    (y_prompt, ssd_state_p, ssd_conv_p, gdn_state_p, gdn_conv_p,
     cmp_rows_p, slc_rows_p, win_p) = trunk(
        x_prompt,
        jnp.zeros((bp, SSD_HEADS, SSD_HEADDIM, SSD_STATE), f32),
        jnp.zeros((bp, CONV_W - 1, SSD_CONV_DIM), f32),
        jnp.zeros((bp, GDN_V_HEADS, GDN_HEAD_DIM, GDN_HEAD_DIM), f32),
        jnp.zeros((bp, CONV_W - 1, GDN_CONV_DIM), f32),
        empty_kv, empty_kv, empty_kv, w)
    bs, n_pages = page_table.shape
    past_shape = (bs, n_pages * PAGE_SIZE, 2, NSA_KV_HEADS, NSA_HEAD_DIM)
    past_cmp = cache_cmp_kv[page_table].reshape(past_shape)
    past_slc = cache_slc_kv[page_table].reshape(past_shape)
    (y_sample, ssd_state_s, ssd_conv_s, gdn_state_s, gdn_conv_s,
     cmp_rows_s, slc_rows_s, win_s) = trunk(
        x_sample, state_ssd, state_ssd_conv, state_gdn, state_gdn_conv,
        past_cmp, past_slc, state_win_kv, w)
    return (y_prompt, y_sample, ssd_state_p, ssd_state_s, ssd_conv_p, ssd_conv_s,
            gdn_state_p, gdn_state_s, gdn_conv_p, gdn_conv_s,
            cmp_rows_p, cmp_rows_s, slc_rows_p, slc_rows_s, win_p, win_s)
```

```python
import functools

import jax
import jax.numpy as jnp
from jax import lax
from jax.experimental import pallas as pl
from jax.experimental.pallas import tpu as pltpu

F32 = jnp.float32
BF16 = jnp.bfloat16

RMS_EPS = 1e-6
NEG_INF = -1e30
FORCE_BONUS = 1e4
CONV_W = 4
CHUNK = 64
SSD_HEADS, SSD_HEADDIM, SSD_STATE, SSD_GROUPS = 16, 64, 128, 2
SSD_INNER = SSD_HEADS * SSD_HEADDIM
SSD_CONV_DIM = SSD_INNER + 2 * SSD_GROUPS * SSD_STATE
GDN_QK_HEADS, GDN_V_HEADS, GDN_HEAD_DIM = 4, 8, 128
GDN_QK_DIM = GDN_QK_HEADS * GDN_HEAD_DIM
GDN_V_DIM = GDN_V_HEADS * GDN_HEAD_DIM
GDN_CONV_DIM = 2 * GDN_QK_DIM + GDN_V_DIM
NSA_HEADS, NSA_KV_HEADS, NSA_HEAD_DIM = 16, 4, 64
NSA_GROUP = NSA_HEADS // NSA_KV_HEADS
NSA_KV_COLS = NSA_KV_HEADS * NSA_HEAD_DIM
CMP_STRIDE, CMP_LEN, CMP_HIDDEN = 16, 32, 64
SLC_BLOCK, SLC_TOPN, WINDOW = 64, 16, 512
PAGE_SIZE = 128
LANES = 128
VMEM_LIMIT = 56 * 1024 * 1024


def _cparams(sem):
    return pltpu.CompilerParams(dimension_semantics=sem, vmem_limit_bytes=VMEM_LIMIT)


def _dot(a, b):
    return jnp.dot(a, b, preferred_element_type=F32)


def _dot_nt(a, b):
    return lax.dot_general(a, b, (((1,), (1,)), ((), ())), preferred_element_type=F32)


def _dot_tn(a, b):
    return lax.dot_general(a, b, (((0,), (0,)), ((), ())), preferred_element_type=F32)


def _split2(x):
    hi = x.astype(BF16)
    lo = (x - hi.astype(F32)).astype(BF16)
    return hi, lo


def _split3(x):
    hi = x.astype(BF16)
    r = x - hi.astype(F32)
    mid = r.astype(BF16)
    lo = (r - mid.astype(F32)).astype(BF16)
    return hi, mid, lo


def _mm3(a, b):
    ah, al = _split2(a)
    bh, bl = _split2(b)
    return _dot(ah, bh) + (_dot(ah, bl) + _dot(al, bh))


def _rms_unit(x):
    return x * lax.rsqrt(jnp.mean(x * x, axis=-1, keepdims=True) + RMS_EPS)


def _silu(x):
    return x * jax.nn.sigmoid(x)


def _softplus(x):
    return jnp.maximum(x, 0.0) + jnp.log1p(jnp.exp(-jnp.abs(x)))


def _iota2(shape, axis):
    return lax.broadcasted_iota(jnp.int32, shape, axis)


def _norm_proj_body(x_ref, g_ref, *refs, n_out):
    w_refs, o_refs = refs[:n_out], refs[n_out:]
    hn = (_rms_unit(x_ref[...]) * g_ref[...]).astype(BF16)
    for w_ref, o_ref in zip(w_refs, o_refs):
        n = w_ref.shape[1]
        for c0 in range(0, n, 512):
            c1 = min(n, c0 + 512)
            o_ref[:, c0:c1] = _dot(hn, w_ref[:, c0:c1])


def norm_proj(x, g, ws, tm):
    t, d = x.shape
    assert t % tm == 0
    in_specs = [pl.BlockSpec((tm, d), lambda i: (i, 0)), pl.BlockSpec((1, d), lambda i: (0, 0))]
    in_specs += [pl.BlockSpec(w.shape, lambda i: (0, 0)) for w in ws]
    out_specs = [pl.BlockSpec((tm, w.shape[1]), lambda i: (i, 0)) for w in ws]
    out_shape = [jax.ShapeDtypeStruct((t, w.shape[1]), F32) for w in ws]
    return pl.pallas_call(
        functools.partial(_norm_proj_body, n_out=len(ws)),
        grid=(t // tm,), in_specs=in_specs, out_specs=out_specs, out_shape=out_shape,
        compiler_params=_cparams(("parallel",)), name="norm_proj",
    )(x, g.reshape(1, d), *ws)


def _mix_mlp_body(x_ref, *refs, group_sizes, final, hc):
    n_y, n_g = sum(group_sizes), len(group_sizes)
    y_refs = refs[:n_y]
    wo_refs = refs[n_y:n_y + n_g]
    g_ref, wu_ref, wd_ref = refs[n_y + n_g:n_y + n_g + 3]
    rest = refs[n_y + n_g + 3:]
    gf_ref = rest[0] if final else None
    o_ref = rest[-1]
    x1 = x_ref[...]
    i = 0
    for gs, wo_ref in zip(group_sizes, wo_refs):
        y = y_refs[i][...]
        for r in y_refs[i + 1:i + gs]:
            y = y + r[...]
        i += gs
        x1 = x1 + _dot(y.astype(BF16), wo_ref[...])
    hn = (_rms_unit(x1) * g_ref[...]).astype(BF16)
    acc = x1
    hidden = wu_ref.shape[1]
    for c0 in range(0, hidden, hc):
        h = jnp.maximum(_dot(hn, wu_ref[:, c0:c0 + hc]), 0.0)
        acc = acc + _dot((h * h).astype(BF16), wd_ref[c0:c0 + hc, :])
    if final:
        acc = _rms_unit(acc) * gf_ref[...]
    o_ref[...] = acc


def mix_mlp(x, y_groups, w_outs, g, w_up, w_down, g_final, tm, hc=None):
    t, d = x.shape
    hc = hc or w_up.shape[1]
    assert t % tm == 0
    final = g_final is not None
    const = lambda i: (0, 0)
    ys = [y for grp in y_groups for y in grp]
    in_specs = [pl.BlockSpec((tm, d), lambda i: (i, 0))]
    in_specs += [pl.BlockSpec((tm, y.shape[1]), lambda i: (i, 0)) for y in ys]
    in_specs += [pl.BlockSpec(w.shape, const) for w in w_outs]
    in_specs += [pl.BlockSpec((1, d), const), pl.BlockSpec(w_up.shape, const), pl.BlockSpec(w_down.shape, const)]
    args = [x, *ys, *w_outs, g.reshape(1, d), w_up, w_down]
    if final:
        in_specs.append(pl.BlockSpec((1, d), const))
        args.append(g_final.reshape(1, d))
    return pl.pallas_call(
        functools.partial(_mix_mlp_body, group_sizes=tuple(len(grp) for grp in y_groups), final=final, hc=hc),
        grid=(t // tm,), in_specs=in_specs, out_specs=pl.BlockSpec((tm, d), lambda i: (i, 0)),
        out_shape=jax.ShapeDtypeStruct((t, d), F32),
        compiler_params=_cparams(("parallel",)), name="mix_mlp",
    )(*args)


def _causal_conv(x, xp_ref, w_ref, first, conv0_ref):
    c = x.shape[0]

    @pl.when(first)
    def _():
        xp_ref[5:8, :] = conv0_ref[0]

    xp_ref[8:8 + c, :] = x
    y = xp_ref[5:5 + c, :] * w_ref[0:1, :]
    for k in range(1, CONV_W):
        y = y + xp_ref[5 + k:5 + k + c, :] * w_ref[k:k + 1, :]
    xp_ref[5:8, :] = x[c - 3:c, :]
    return y


def _cumsum_pair(col, row):
    c = col.shape[0]
    ri, ci = _iota2((c, c), 0), _iota2((c, c), 1)
    tril = jnp.where(ri >= ci, 1.0, 0.0).astype(BF16)
    triu = jnp.where(ri <= ci, 1.0, 0.0).astype(BF16)
    ch, cm, cl = _split3(col)
    rh, rm, rl = _split3(row)
    cs = _dot(tril, ch) + (_dot(tril, cm) + _dot(tril, cl))
    cst = _dot(rh, triu) + (_dot(rm, triu) + _dot(rl, triu))
    return cs, cst


def _ssd_body(xbc_ref, z_ref, sm_ref, smt_ref, st0_ref, cv0_ref, cw_ref, cb_ref, dtb_r_ref, dtb_c_ref,
              al_r_ref, al_c_ref, dfull_ref, nw_ref, y_ref, st_ref, xp_ref, ybuf_ref, *, l_valid):
    cidx = pl.program_id(1)
    c = CHUNK
    n, p = SSD_STATE, SSD_HEADDIM

    @pl.when(cidx == 0)
    def _():
        st_ref[0] = st0_ref[0]

    conv = _causal_conv(xbc_ref[0], xp_ref, cw_ref, cidx == 0, cv0_ref) + cb_ref[...]
    act = _silu(conv)
    xs = act[:, :SSD_INNER]
    bm = act[:, SSD_INNER:SSD_INNER + SSD_GROUPS * n]
    cm = act[:, SSD_INNER + SSD_GROUPS * n:]

    valid_c = (cidx * c + _iota2((c, 1), 0)) < l_valid
    valid_r = (cidx * c + _iota2((1, c), 1)) < l_valid
    dt = jnp.where(valid_c, _softplus(sm_ref[0][:, 0:SSD_HEADS] + dtb_r_ref[...]), 0.0)
    dtt = jnp.where(valid_r, _softplus(smt_ref[0, 0][0:SSD_HEADS, :] + dtb_c_ref[...]), 0.0)
    cs, cst = _cumsum_pair(dt * (-jnp.exp(al_r_ref[...])), dtt * (-jnp.exp(al_c_ref[...])))
    cs_last = cs[c - 1:c, :]
    ecs = jnp.exp(cs)
    to_end = jnp.exp(cs_last - cs) * dt
    edec = jnp.exp(cs_last)
    causal = _iota2((c, c), 0) >= _iota2((c, c), 1)

    heads_per_group = SSD_HEADS // SSD_GROUPS
    for g in range(SSD_GROUPS):
        bm_g = bm[:, g * n:(g + 1) * n].astype(BF16)
        cm_g = cm[:, g * n:(g + 1) * n].astype(BF16)
        cb = _dot_nt(cm_g, bm_g)
        for r in range(heads_per_group):
            h = g * heads_per_group + r
            seg = cs[:, h:h + 1] - cst[h:h + 1, :]
            decay = jnp.exp(jnp.where(causal, seg, NEG_INF))
            sc = (cb * decay * dtt[h:h + 1, :]).astype(BF16)
            x_h = xs[:, h * p:(h + 1) * p]
            s_h = st_ref[0, h]
            y_diag = _dot(sc, x_h.astype(BF16))
            y_off = _dot_nt(cm_g, s_h.astype(BF16)) * ecs[:, h:h + 1]
            ybuf_ref[:, h * p:(h + 1) * p] = y_diag + y_off
            xw = (x_h * to_end[:, h:h + 1]).astype(BF16)
            st_ref[0, h] = s_h * edec[:, h:h + 1] + _dot_tn(xw, bm_g)

    y = ybuf_ref[...] + dfull_ref[...] * xs
    yz = y * _silu(z_ref[0])
    gw = SSD_INNER // SSD_GROUPS
    for g in range(SSD_GROUPS):
        seg = yz[:, g * gw:(g + 1) * gw]
        y_ref[0, :, g * gw:(g + 1) * gw] = _rms_unit(seg) * nw_ref[:, g * gw:(g + 1) * gw]


def ssd_mixer(xbc, z, small, small_t, state0, conv0, conv_w, conv_b, dt_bias, a_log, d_skip, norm_w, l_valid):
    b, lp, _ = xbc.shape
    nc = lp // CHUNK
    h = SSD_HEADS
    d_full = jnp.repeat(d_skip, SSD_HEADDIM).reshape(1, SSD_INNER)
    const2 = lambda i, j: (0, 0)
    tok = lambda w: pl.BlockSpec((1, CHUNK, w), lambda i, j: (i, j, 0))
    in_specs = [
        tok(SSD_CONV_DIM), tok(SSD_INNER), tok(LANES),
        pl.BlockSpec((1, 1, 32, CHUNK), lambda i, j: (i, j, 0, 0)),
        pl.BlockSpec((1, h, SSD_HEADDIM, SSD_STATE), lambda i, j: (i, 0, 0, 0)),
        pl.BlockSpec((1, CONV_W - 1, SSD_CONV_DIM), lambda i, j: (i, 0, 0)),
        pl.BlockSpec((CONV_W, SSD_CONV_DIM), const2), pl.BlockSpec((1, SSD_CONV_DIM), const2),
        pl.BlockSpec((1, h), const2), pl.BlockSpec((h, 1), const2),
        pl.BlockSpec((1, h), const2), pl.BlockSpec((h, 1), const2),
        pl.BlockSpec((1, SSD_INNER), const2), pl.BlockSpec((1, SSD_INNER), const2),
    ]
    out_specs = [tok(SSD_INNER), pl.BlockSpec((1, h, SSD_HEADDIM, SSD_STATE), lambda i, j: (i, 0, 0, 0))]
    out_shape = [jax.ShapeDtypeStruct((b, lp, SSD_INNER), F32),
                 jax.ShapeDtypeStruct((b, h, SSD_HEADDIM, SSD_STATE), F32)]
    return pl.pallas_call(
        functools.partial(_ssd_body, l_valid=l_valid),
        grid=(b, nc), in_specs=in_specs, out_specs=out_specs, out_shape=out_shape,
        scratch_shapes=[pltpu.VMEM((CHUNK + 8, SSD_CONV_DIM), F32), pltpu.VMEM((CHUNK, SSD_INNER), F32)],
        compiler_params=_cparams(("parallel", "arbitrary")), name="ssd_mixer",
    )(xbc, z, small, small_t, state0, conv0, conv_w, conv_b.reshape(1, -1),
      dt_bias.reshape(1, h), dt_bias.reshape(h, 1), a_log.reshape(1, h), a_log.reshape(h, 1),
      d_full, norm_w.reshape(1, SSD_INNER))


def _unit_lower_inverse(a):
    c = a.shape[0]
    ri, ci = _iota2((c, c), 0), _iota2((c, c), 1)
    eye = jnp.where(ri == ci, 1.0, 0.0)
    same_blk = (ri // 16) == (ci // 16)
    d = jnp.where(same_blk, a, 0.0)
    nn = a - d
    d2 = _mm3(d, d)
    d4 = _mm3(d2, d2)
    d8 = _mm3(d4, d4)
    pinv = eye - d
    pinv = pinv + _mm3(pinv, d2)
    pinv = pinv + _mm3(pinv, d4)
    pinv = pinv + _mm3(pinv, d8)
    e = _mm3(pinv, nn)
    e2 = _mm3(e, e)
    q = eye - e
    q = q + _mm3(q, e2)
    return _mm3(q, pinv)


def _l2norm(x):
    return x * lax.rsqrt(jnp.sum(x * x, axis=-1, keepdims=True) + 1e-6)


def _gdn_body(qkv_ref, z_ref, sm_ref, smt_ref, st0_ref, cv0_ref, cw_ref, dtb_r_ref, dtb_c_ref,
              al_r_ref, al_c_ref, nw_ref, o_ref, st_ref, xp_ref, *, l_valid):
    cidx = pl.program_id(1)
    c = CHUNK
    hd = GDN_HEAD_DIM
    nh = GDN_V_HEADS

    @pl.when(cidx == 0)
    def _():
        st_ref[0] = st0_ref[0]

    act = _silu(_causal_conv(qkv_ref[0], xp_ref, cw_ref, cidx == 0, cv0_ref))
    valid_c = (cidx * c + _iota2((c, 1), 0)) < l_valid
    valid_r = (cidx * c + _iota2((1, c), 1)) < l_valid
    sm = sm_ref[0]
    smt = smt_ref[0, 0]
    beta = jnp.where(valid_c, jax.nn.sigmoid(sm[:, 16:16 + nh]), 0.0)
    g_c = jnp.where(valid_c, -jnp.exp(al_r_ref[...]) * _softplus(sm[:, 24:24 + nh] + dtb_r_ref[...]), 0.0)
    g_r = jnp.where(valid_r, -jnp.exp(al_c_ref[...]) * _softplus(smt[24:24 + nh, :] + dtb_c_ref[...]), 0.0)
    cs, cst = _cumsum_pair(g_c, g_r)
    cs_last = cs[c - 1:c, :]
    ecs = jnp.exp(cs)
    e_end = jnp.exp(cs_last - cs)
    etot = jnp.exp(cs_last)
    ri, ci = _iota2((c, c), 0), _iota2((c, c), 1)
    causal = ri >= ci
    strict = ri > ci
    rep = GDN_V_HEADS // GDN_QK_HEADS

    for hq in range(GDN_QK_HEADS):
        q_h = _l2norm(act[:, hq * hd:(hq + 1) * hd]) * (hd ** -0.5)
        k_h = _l2norm(act[:, GDN_QK_DIM + hq * hd:GDN_QK_DIM + (hq + 1) * hd])
        k16 = k_h.astype(BF16)
        kk = _dot_nt(k16, k16)
        qk0 = _dot_nt(q_h.astype(BF16), k16)
        for r in range(rep):
            h = hq * rep + r
            seg = cs[:, h:h + 1] - cst[h:h + 1, :]
            decay = jnp.exp(jnp.where(causal, seg, NEG_INF))
            b_h = beta[:, h:h + 1]
            a_mat = jnp.where(strict, kk * b_h * decay, 0.0)
            tinv = _unit_lower_inverse(a_mat)
            v_h = act[:, 2 * GDN_QK_DIM + h * hd:2 * GDN_QK_DIM + (h + 1) * hd]
            rhs = jnp.concatenate([v_h * b_h, k_h * (b_h * ecs[:, h:h + 1])], axis=1)
            eye = jnp.where(ri == ci, 1.0, 0.0)
            sol = rhs + _dot((tinv - eye).astype(BF16), rhs.astype(BF16))
            u, w = sol[:, :hd], sol[:, hd:]
            s_h = st_ref[0, h]
            s16 = s_h.astype(BF16)
            v_new = u - _dot(w.astype(BF16), s16)
            v16 = v_new.astype(BF16)
            o = _dot((q_h * ecs[:, h:h + 1]).astype(BF16), s16) + _dot((qk0 * decay).astype(BF16), v16)
            k_dec = (k_h * e_end[:, h:h + 1]).astype(BF16)
            st_ref[0, h] = s_h * etot[:, h:h + 1] + _dot_tn(k_dec, v16)
            o_ref[0, :, h * hd:(h + 1) * hd] = (_rms_unit(o) * nw_ref[...]) * _silu(z_ref[0, :, h * hd:(h + 1) * hd])


def gdn_mixer(qkv, z, small, small_t, state0, conv0, conv_w, dt_bias, a_log, norm_w, l_valid):
    b, lp, _ = qkv.shape
    nc = lp // CHUNK
    h = GDN_V_HEADS
    const2 = lambda i, j: (0, 0)
    tok = lambda w: pl.BlockSpec((1, CHUNK, w), lambda i, j: (i, j, 0))
    in_specs = [
        tok(GDN_CONV_DIM), tok(GDN_V_DIM), tok(LANES),
        pl.BlockSpec((1, 1, 32, CHUNK), lambda i, j: (i, j, 0, 0)),
        pl.BlockSpec((1, h, GDN_HEAD_DIM, GDN_HEAD_DIM), lambda i, j: (i, 0, 0, 0)),
        pl.BlockSpec((1, CONV_W - 1, GDN_CONV_DIM), lambda i, j: (i, 0, 0)),
        pl.BlockSpec((CONV_W, GDN_CONV_DIM), const2),
        pl.BlockSpec((1, h), const2), pl.BlockSpec((h, 1), const2),
        pl.BlockSpec((1, h), const2), pl.BlockSpec((h, 1), const2),
        pl.BlockSpec((1, GDN_HEAD_DIM), const2),
    ]
    out_specs = [tok(GDN_V_DIM), pl.BlockSpec((1, h, GDN_HEAD_DIM, GDN_HEAD_DIM), lambda i, j: (i, 0, 0, 0))]
    out_shape = [jax.ShapeDtypeStruct((b, lp, GDN_V_DIM), F32),
                 jax.ShapeDtypeStruct((b, h, GDN_HEAD_DIM, GDN_HEAD_DIM), F32)]
    return pl.pallas_call(
        functools.partial(_gdn_body, l_valid=l_valid),
        grid=(b, nc), in_specs=in_specs, out_specs=out_specs, out_shape=out_shape,
        scratch_shapes=[pltpu.VMEM((CHUNK + 8, GDN_CONV_DIM), F32)],
        compiler_params=_cparams(("parallel", "arbitrary")), name="gdn_mixer",
    )(qkv, z, small, small_t, state0, conv0, conv_w,
      dt_bias.reshape(1, h), dt_bias.reshape(h, 1), a_log.reshape(1, h), a_log.reshape(h, 1),
      norm_w.reshape(1, GDN_HEAD_DIM))


def _prep_ab_weights(w_in_ab):
    o = [0, SSD_INNER]
    o.append(o[-1] + SSD_CONV_DIM)
    o.append(o[-1] + SSD_HEADS)
    o.append(o[-1] + GDN_CONV_DIM)
    o.append(o[-1] + GDN_V_DIM)
    o.append(o[-1] + 2 * GDN_V_HEADS)
    w = w_in_ab.astype(BF16)
    small = jnp.concatenate([w[:, o[2]:o[3]], w[:, o[5]:o[6]]], axis=1)
    small = jnp.pad(small, ((0, 0), (0, LANES - small.shape[1])))
    return [w[:, o[0]:o[1]], w[:, o[1]:o[2]], w[:, o[3]:o[4]], w[:, o[4]:o[5]], small]


def _pad_time(a, lp):
    return jnp.pad(a, [(0, 0), (0, lp - a.shape[1])] + [(0, 0)] * (a.ndim - 2))


def layer0_mixer(x, ssd_state0, ssd_conv0, gdn_state0, gdn_conv0, p, tm):
    b, l, d = x.shape
    lp = -(-l // CHUNK) * CHUNK
    xp = _pad_time(x, lp).reshape(b * lp, d)
    z_ssd, xbc, qkv, z_gdn, small = norm_proj(xp, p["norm_mix"][0], p["w_ab"], min(tm, b * lp))
    nc = lp // CHUNK
    small_t = small[:, :32].reshape(b, nc, CHUNK, 32).transpose(0, 1, 3, 2)
    r3 = lambda a: a.reshape(b, lp, a.shape[-1])
    xbc, qkv, small = r3(xbc), r3(qkv), r3(small)
    y_ssd, ssd_state = ssd_mixer(xbc, r3(z_ssd), small, small_t, ssd_state0, ssd_conv0, p["ssd_conv_w"],
                                 p["ssd_conv_b"], p["ssd_dt_bias"], p["ssd_a_log"], p["ssd_d"], p["ssd_norm_w"], l)
    y_gdn, gdn_state = gdn_mixer(qkv, r3(z_gdn), small, small_t, gdn_state0, gdn_conv0, p["gdn_conv_w"],
                                 p["gdn_dt_bias"], p["gdn_a_log"], p["gdn_norm_w"], l)
    ssd_conv = jnp.concatenate([ssd_conv0, xbc[:, :l]], axis=1)[:, l:]
    gdn_conv = jnp.concatenate([gdn_conv0, qkv[:, :l]], axis=1)[:, l:]
    y_ssd = y_ssd[:, :l].reshape(b * l, SSD_INNER)
    y_gdn = y_gdn[:, :l].reshape(b * l, GDN_V_DIM)
    return y_ssd, y_gdn, ssd_state, ssd_conv, gdn_state, gdn_conv


def _nsa_lt_body(*refs, n_in, n_prefetch=0):
    refs = refs[n_prefetch:]
    x_refs = refs[:n_in]
    wk_ref, wv_ref, o_ref = refs[n_in:]
    w = 2 * NSA_KV_COLS
    acc_k = acc_v = None
    for j in range(CMP_STRIDE):
        xk = jnp.concatenate([r[0, :, j * w:j * w + NSA_KV_COLS] for r in x_refs], axis=0).astype(BF16)
        xv = jnp.concatenate([r[0, :, j * w + NSA_KV_COLS:(j + 1) * w] for r in x_refs], axis=0).astype(BF16)
        dk, dv = _dot(xk, wk_ref[j]), _dot(xv, wv_ref[j])
        acc_k = dk if acc_k is None else acc_k + dk
        acc_v = dv if acc_v is None else acc_v + dv
    o_ref[0, :, 0:2 * NSA_KV_COLS] = acc_k
    o_ref[0, :, 2 * NSA_KV_COLS:] = acc_v


def _lt_weights(w1):
    w1r = w1.reshape(2, CMP_STRIDE, NSA_HEAD_DIM, CMP_HIDDEN)
    eye = jnp.eye(NSA_KV_HEADS, dtype=w1.dtype)
    lead = jnp.einsum("ab,jdh->jadbh", eye, w1r[0]).reshape(CMP_STRIDE, NSA_KV_COLS, NSA_KV_COLS)
    tail = jnp.einsum("ab,jdh->jadbh", eye, w1r[1]).reshape(CMP_STRIDE, NSA_KV_COLS, NSA_KV_COLS)
    return jnp.concatenate([lead, tail], axis=-1).astype(BF16)


def nsa_lt_dense(kv, wk, wv, tc):
    b, l, w = kv.shape
    nch = l // CMP_STRIDE
    x = kv[:, :nch * CMP_STRIDE].reshape(b, nch, CMP_STRIDE * w)
    tc = min(tc, nch)
    assert nch % tc == 0
    const3 = lambda i, j: (0, 0, 0)
    return pl.pallas_call(
        functools.partial(_nsa_lt_body, n_in=1), grid=(b, nch // tc),
        in_specs=[pl.BlockSpec((1, tc, CMP_STRIDE * w), lambda i, j: (i, j, 0)),
                  pl.BlockSpec(wk.shape, const3), pl.BlockSpec(wv.shape, const3)],
        out_specs=pl.BlockSpec((1, tc, 4 * NSA_KV_COLS), lambda i, j: (i, j, 0)),
        out_shape=jax.ShapeDtypeStruct((b, nch, 4 * NSA_KV_COLS), F32),
        compiler_params=_cparams(("parallel", "parallel")), name="nsa_lt_dense",
    )(x, wk, wv)


def nsa_lt_paged(cache, page_table, wk, wv, n_pg):
    npool, page, w = cache.shape
    b, n_pages = page_table.shape
    rows = page // CMP_STRIDE
    x = cache.reshape(npool, rows, CMP_STRIDE * w)
    assert n_pages % n_pg == 0
    page_spec = lambda i: pl.BlockSpec((1, rows, CMP_STRIDE * w), lambda bi, gi, pt: (pt[bi, gi * n_pg + i], 0, 0))
    const3 = lambda bi, gi, pt: (0, 0, 0)
    grid_spec = pltpu.PrefetchScalarGridSpec(
        num_scalar_prefetch=1, grid=(b, n_pages // n_pg),
        in_specs=[page_spec(i) for i in range(n_pg)] + [pl.BlockSpec(wk.shape, const3), pl.BlockSpec(wv.shape, const3)],
        out_specs=pl.BlockSpec((1, n_pg * rows, 4 * NSA_KV_COLS), lambda bi, gi, pt: (bi, gi, 0)))
    return pl.pallas_call(
        functools.partial(_nsa_lt_body, n_in=n_pg, n_prefetch=1), grid_spec=grid_spec,
        out_shape=jax.ShapeDtypeStruct((b, n_pages * rows, 4 * NSA_KV_COLS), F32),
        compiler_params=_cparams(("parallel", "parallel")), name="nsa_lt_paged",
    )(page_table, *([x] * n_pg), wk, wv)


def _cmp_kv_body(lt_ref, pek_ref, w1k_ref, b1k_ref, w2k_ref, pev_ref, w1v_ref, b1v_ref, w2v_ref,
                 kc_ref, vc_ref, sh_ref):
    nch = lt_ref.shape[1]
    kvc = NSA_KV_COLS

    def branch(off, pe_ref, w1_ref, b1_ref, w2_ref, out_ref):
        pe = jnp.broadcast_to(pe_ref[...], (8, pe_ref.shape[1]))
        pe_term = _mm3(pe, w1_ref[...])[0:1, :] + b1_ref[...]
        pe4 = jnp.concatenate([pe_term] * NSA_KV_HEADS, axis=1)
        sh_ref[0:nch, :] = lt_ref[0, :, off + kvc:off + 2 * kvc]
        sh_ref[nch:nch + 8, :] = jnp.zeros((8, kvc), F32)
        hid = _silu(lt_ref[0, :, off:off + kvc] + sh_ref[1:nch + 1, :] + pe4)
        out_ref[0] = _dot(hid.astype(BF16), w2_ref[...])

    branch(0, pek_ref, w1k_ref, b1k_ref, w2k_ref, kc_ref)
    branch(2 * kvc, pev_ref, w1v_ref, b1v_ref, w2v_ref, vc_ref)


def cmp_kv(lt, p):
    b, nch, _ = lt.shape
    kvc = NSA_KV_COLS
    const2 = lambda i: (0, 0)
    args, specs = [], []
    for s in ("k", "v"):
        pe = p["cmp_pe_" + s].reshape(1, -1)
        w1 = p["cmp_w1_" + s]
        b1 = p["cmp_b1_" + s].reshape(1, -1)
        w2 = jnp.kron(jnp.eye(NSA_KV_HEADS, dtype=F32), p["cmp_w2_" + s]).astype(BF16)
        for a in (pe, w1, b1, w2):
            args.append(a)
            specs.append(pl.BlockSpec(a.shape, const2))
    out_spec = pl.BlockSpec((1, nch, kvc), lambda i: (i, 0, 0))
    return pl.pallas_call(
        _cmp_kv_body, grid=(b,),
        in_specs=[pl.BlockSpec((1, nch, 4 * kvc), lambda i: (i, 0, 0))] + specs,
        out_specs=[out_spec, out_spec],
        out_shape=[jax.ShapeDtypeStruct((b, nch, kvc), F32)] * 2,
        scratch_shapes=[pltpu.VMEM((nch + 8, kvc), F32)],
        compiler_params=_cparams(("parallel",)), name="cmp_kv",
    )(lt, *args)


def _cmp_attn_body(q_ref, gate_ref, kc_ref, vc_ref, oc_ref, sel_ref, *, tq, pos0, n_slc, nbp):
    qi = pl.program_id(1)
    ncp = kc_ref.shape[1]
    hd = NSA_HEAD_DIM
    t_col = pos0 + qi * tq + _iota2((tq, 1), 0)
    cmask = (_iota2((1, ncp), 1) * CMP_STRIDE + (CMP_LEN - 1)) <= t_col
    kc = kc_ref[0].astype(BF16)
    vc = vc_ref[0].astype(BF16)
    gate = jax.nn.sigmoid(gate_ref[0])
    nn = _iota2((ncp, nbp), 0) * CMP_STRIDE
    ss = _iota2((ncp, nbp), 1) * SLC_BLOCK
    overlap = jnp.where((nn < ss + SLC_BLOCK) & (nn + CMP_LEN > ss), 1.0, 0.0).astype(BF16)
    blk = _iota2((1, nbp), 1)
    blk_f = blk.astype(F32)
    cur = lax.shift_right_logical(t_col, 6)
    forced = (blk == 0) | (blk == cur) | (blk == cur - 1)
    bonus = jnp.where(forced, FORCE_BONUS, 0.0)
    causal_blk = blk * SLC_BLOCK <= t_col
    n_pick = min(SLC_TOPN, n_slc)

    for k in range(NSA_KV_HEADS):
        kc_k = kc[:, k * hd:(k + 1) * hd]
        vc_k = vc[:, k * hd:(k + 1) * hd]
        psum = jnp.zeros((tq, ncp), F32)
        for g in range(NSA_GROUP):
            h = k * NSA_GROUP + g
            q_h = (q_ref[0, :, h * hd:(h + 1) * hd] * (hd ** -0.5)).astype(BF16)
            s = jnp.where(cmask, _dot_nt(q_h, kc_k), NEG_INF)
            m = jnp.max(s, axis=-1, keepdims=True)
            e = jnp.where(cmask, jnp.exp(s - m), 0.0)
            den = jnp.sum(e, axis=-1, keepdims=True)
            p = e / jnp.where(den > 0.0, den, 1.0)
            psum = psum + p
            oc_ref[0, :, h * hd:(h + 1) * hd] = _dot(p.astype(BF16), vc_k) * gate[:, h:h + 1]
        ph, plo = _split2(psum)
        imp = _dot(ph, overlap) + _dot(plo, overlap)
        score = jnp.where(causal_blk, imp + bonus, NEG_INF)
        score = jnp.where(blk < n_slc, score, -jnp.inf)

        def pick_one(_, carry):
            work, sel = carry
            m = jnp.max(work, axis=-1, keepdims=True)
            idx = jnp.min(jnp.where(work == m, blk_f, 1e9), axis=-1, keepdims=True)
            hit = blk_f == idx
            return jnp.where(hit, -jnp.inf, work), jnp.where(hit, 1.0, sel)

        _, sel = lax.fori_loop(0, n_pick, pick_one, (score, jnp.zeros((tq, nbp), F32)))
        sel_ref[0, :, k * nbp:(k + 1) * nbp] = sel


def _nbp(n_slc):
    return 64 if n_slc <= 64 else -(-n_slc // LANES) * LANES


def cmp_attn_select(q, gate, kc, vc, tq, pos0, n_slc):
    b, l, dq = q.shape
    ncp = kc.shape[1]
    nbp = _nbp(n_slc)
    assert l % tq == 0
    tok = lambda w: pl.BlockSpec((1, tq, w), lambda i, j: (i, j, 0))
    whole = pl.BlockSpec((1, ncp, NSA_KV_COLS), lambda i, j: (i, 0, 0))
    return pl.pallas_call(
        functools.partial(_cmp_attn_body, tq=tq, pos0=pos0, n_slc=n_slc, nbp=nbp),
        grid=(b, l // tq), in_specs=[tok(dq), tok(LANES), whole, whole],
        out_specs=[tok(dq), tok(NSA_KV_HEADS * nbp)],
        out_shape=[jax.ShapeDtypeStruct((b, l, dq), F32), jax.ShapeDtypeStruct((b, l, NSA_KV_HEADS * nbp), F32)],
        compiler_params=_cparams(("parallel", "parallel")), name="cmp_attn_select",
    )(q, gate, kc, vc)


def _nsa_attn_body(q_ref, gate_ref, kv_ref, *rest, mode, tq, tk, qpos0, kpos0, gate_off, nbp):
    if mode == "slc":
        sel_ref, o_ref, qs_ref, m_ref, l_ref, acc_ref = rest
    else:
        o_ref, qs_ref, m_ref, l_ref, acc_ref = rest
    qi = pl.program_id(1)
    hd = NSA_HEAD_DIM
    nk = kv_ref.shape[1] // tk
    qmin = qpos0 + qi * tq
    t_col = qmin + _iota2((tq, 1), 0)
    for k in range(NSA_KV_HEADS):
        qs_ref[k] = jnp.concatenate(
            [q_ref[0, :, (k * NSA_GROUP + g) * hd:(k * NSA_GROUP + g + 1) * hd] * (hd ** -0.5)
             for g in range(NSA_GROUP)], axis=0)
    m_ref[...] = jnp.full(m_ref.shape, NEG_INF, F32)
    l_ref[...] = jnp.zeros(l_ref.shape, F32)
    acc_ref[...] = jnp.zeros(acc_ref.shape, F32)
    hi = jnp.minimum(nk - 1, (qmin + tq - 1 - kpos0) // tk)
    if mode == "slc":
        lo = 0
    else:
        lo = jnp.maximum(qmin - (WINDOW - 1) - kpos0, 0) // tk

    def step(ki, carry):
        k0 = pl.multiple_of(ki * tk, tk)
        kv = kv_ref[0, pl.ds(k0, tk), :]
        kpos = kpos0 + k0 + _iota2((1, tk), 1)
        if mode == "slc":
            expand = jnp.where(_iota2((nbp, tk), 0) == lax.shift_right_logical(k0 + _iota2((nbp, tk), 1), 6),
                               1.0, 0.0).astype(BF16)
            base = kpos <= t_col
        else:
            dist = t_col - kpos
            base = (dist >= 0) & (dist < WINDOW)
        for k in range(NSA_KV_HEADS):
            kk = kv[:, k * hd:(k + 1) * hd].astype(BF16)
            vv = kv[:, NSA_KV_COLS + k * hd:NSA_KV_COLS + (k + 1) * hd].astype(BF16)
            s = _dot_nt(qs_ref[k].astype(BF16), kk)
            if mode == "slc":
                mask = base & (_dot(sel_ref[0, :, k * nbp:(k + 1) * nbp].astype(BF16), expand) > 0.5)
            else:
                mask = base
            for g in range(NSA_GROUP):
                r = k * NSA_GROUP + g
                s_g = jnp.where(mask, s[g * tq:(g + 1) * tq], NEG_INF)
                m_old = m_ref[r]
                m_new = jnp.maximum(m_old, jnp.max(s_g, axis=-1, keepdims=True))
                alpha = jnp.exp(m_old - m_new)
                p = jnp.where(mask, jnp.exp(s_g - m_new), 0.0)
                l_ref[r] = alpha * l_ref[r] + jnp.sum(p, axis=-1, keepdims=True)
                acc_ref[r] = alpha * acc_ref[r] + _dot(p.astype(BF16), vv)
                m_ref[r] = m_new
        return carry

    lax.fori_loop(lo, hi + 1, step, 0)
    gate = jax.nn.sigmoid(gate_ref[0])
    for r in range(NSA_HEADS):
        l = l_ref[r]
        o = acc_ref[r] / jnp.where(l > 0.0, l, 1.0)
        o_ref[0, :, r * hd:(r + 1) * hd] = o * gate[:, gate_off + r:gate_off + r + 1]


def nsa_attn(mode, q, gate, kv, sel, tq, tk, qpos0, kpos0):
    b, l, dq = q.shape
    nkeys = kv.shape[1]
    assert l % tq == 0 and nkeys % tk == 0
    tok = lambda w: pl.BlockSpec((1, tq, w), lambda i, j: (i, j, 0))
    in_specs = [tok(dq), tok(LANES), pl.BlockSpec((1, nkeys, kv.shape[2]), lambda i, j: (i, 0, 0))]
    args = [q, gate, kv]
    nbp = 0
    if mode == "slc":
        nbp = sel.shape[2] // NSA_KV_HEADS
        in_specs.append(tok(sel.shape[2]))
        args.append(sel)
    return pl.pallas_call(
        functools.partial(_nsa_attn_body, mode=mode, tq=tq, tk=tk, qpos0=qpos0, kpos0=kpos0,
                          gate_off=NSA_HEADS * (1 if mode == "slc" else 2), nbp=nbp),
        grid=(b, l // tq), in_specs=in_specs, out_specs=tok(dq),
        out_shape=jax.ShapeDtypeStruct((b, l, dq), F32),
        scratch_shapes=[pltpu.VMEM((NSA_KV_HEADS, NSA_GROUP * tq, NSA_HEAD_DIM), F32),
                        pltpu.VMEM((NSA_HEADS, tq, 1), F32), pltpu.VMEM((NSA_HEADS, tq, 1), F32),
                        pltpu.VMEM((NSA_HEADS, tq, NSA_HEAD_DIM), F32)],
        compiler_params=_cparams(("parallel", "parallel")), name="nsa_attn_" + mode,
    )(*args)


def _slc_paged_body(pt_ref, q_ref, gate_ref, selrows_ref, kvnew_ref, *rest, n_pg, nbp, tq, l_new, blk_new):
    page_refs = rest[:n_pg]
    o_ref, qbd_ref, m_ref, l_ref, acc_ref = rest[n_pg:]
    gi = pl.program_id(1)
    hd = NSA_HEAD_DIM
    rows = NSA_HEADS * tq

    @pl.when(gi == 0)
    def _():
        qbd_ref[...] = jnp.zeros(qbd_ref.shape, F32)
        for h in range(NSA_HEADS):
            k = h // NSA_GROUP
            qbd_ref[h * tq:(h + 1) * tq, k * hd:(k + 1) * hd] = q_ref[0, :, h * hd:(h + 1) * hd] * (hd ** -0.5)
        m_ref[...] = jnp.full(m_ref.shape, NEG_INF, F32)
        l_ref[...] = jnp.zeros(l_ref.shape, F32)
        acc_ref[...] = jnp.zeros(acc_ref.shape, F32)

    sel = selrows_ref[0].astype(BF16)
    qbd = qbd_ref[...].astype(BF16)

    def update(s, mask, vv):
        s = jnp.where(mask, s, NEG_INF)
        m_old = m_ref[...]
        m_new = jnp.maximum(m_old, jnp.max(s, axis=-1, keepdims=True))
        alpha = jnp.exp(m_old - m_new)
        p = jnp.where(mask, jnp.exp(s - m_new), 0.0)
        l_ref[...] = alpha * l_ref[...] + jnp.sum(p, axis=-1, keepdims=True)
        acc_ref[...] = alpha * acc_ref[...] + _dot(p.astype(BF16), vv)
        m_ref[...] = m_new

    page = page_refs[0].shape[1]
    for i in range(n_pg):
        pg = gi * n_pg + i
        kv = page_refs[i][0]
        tok_blk = lax.shift_right_logical(pg * page + _iota2((nbp, page), 1), 6)
        expand = jnp.where(_iota2((nbp, page), 0) == tok_blk, 1.0, 0.0).astype(BF16)
        mask = _dot(sel, expand) > 0.5
        update(_dot_nt(qbd, kv[:, :NSA_KV_COLS].astype(BF16)), mask, kv[:, NSA_KV_COLS:].astype(BF16))

    @pl.when(gi == pl.num_programs(1) - 1)
    def _():
        kvn = kvnew_ref[0]
        nr = kvn.shape[0]
        j = _iota2((rows, nr), 1)
        iq = jnp.bitwise_and(_iota2((rows, nr), 0), tq - 1)
        picked = jnp.sum(jnp.where(_iota2((rows, nbp), 1) == blk_new, selrows_ref[0], 0.0), axis=-1, keepdims=True)
        mask = (picked > 0.5) & (j <= iq) & (j < l_new)
        update(_dot_nt(qbd, kvn[:, :NSA_KV_COLS].astype(BF16)), mask, kvn[:, NSA_KV_COLS:].astype(BF16))
        gate = jax.nn.sigmoid(gate_ref[0])
        l = l_ref[...]
        o = acc_ref[...] / jnp.where(l > 0.0, l, 1.0)
        for h in range(NSA_HEADS):
            k = h // NSA_GROUP
            o_ref[0, :, h * hd:(h + 1) * hd] = (o[h * tq:(h + 1) * tq, k * hd:(k + 1) * hd]
                                                * gate[:, NSA_HEADS + h:NSA_HEADS + h + 1])


def slc_paged(q, gate, selrows, kv_new, cache, page_table, n_pg, l_new, p0):
    b, tq, dq = q.shape
    npool, page, w = cache.shape
    n_pages = page_table.shape[1]
    nbp = selrows.shape[2]
    rows = NSA_HEADS * tq
    assert n_pages % n_pg == 0 and p0 % SLC_BLOCK == 0 and l_new <= SLC_BLOCK and p0 == n_pages * page
    assert tq & (tq - 1) == 0
    per_b = lambda shape: pl.BlockSpec((1,) + shape, lambda bi, gi, pt: (bi, 0, 0))
    page_spec = lambda i: pl.BlockSpec((1, page, w), lambda bi, gi, pt: (pt[bi, gi * n_pg + i], 0, 0))
    grid_spec = pltpu.PrefetchScalarGridSpec(
        num_scalar_prefetch=1, grid=(b, n_pages // n_pg),
        in_specs=[per_b((tq, dq)), per_b((tq, LANES)), per_b((rows, nbp)), per_b((kv_new.shape[1], w))]
        + [page_spec(i) for i in range(n_pg)],
        out_specs=per_b((tq, dq)),
        scratch_shapes=[pltpu.VMEM((rows, NSA_KV_COLS), F32), pltpu.VMEM((rows, 1), F32),
                        pltpu.VMEM((rows, 1), F32), pltpu.VMEM((rows, NSA_KV_COLS), F32)])
    return pl.pallas_call(
        functools.partial(_slc_paged_body, n_pg=n_pg, nbp=nbp, tq=tq, l_new=l_new, blk_new=p0 // SLC_BLOCK),
        grid_spec=grid_spec, out_shape=jax.ShapeDtypeStruct((b, tq, dq), F32),
        compiler_params=_cparams(("parallel", "arbitrary")), name="slc_paged",
    )(page_table, q, gate, selrows, kv_new, *([cache] * n_pg))


def _prep_nsa_weights(w_in_nsa):
    w = w_in_nsa.astype(BF16)
    dq = NSA_HEADS * NSA_HEAD_DIM
    kvw = 2 * NSA_KV_COLS
    gate = jnp.pad(w[:, dq + 3 * kvw:], ((0, 0), (0, LANES - 3 * NSA_HEADS)))
    return [w[:, :dq], w[:, dq:dq + kvw], w[:, dq + kvw:dq + 2 * kvw], w[:, dq + 2 * kvw:dq + 3 * kvw], gate]


def _kv5(a):
    return a.reshape(a.shape[0], a.shape[1], 2, NSA_KV_HEADS, NSA_HEAD_DIM)


def nsa_prompt(x, p, tm, tq, tk):
    b, l, d = x.shape
    q, kvc, kvs, kvw, gate = norm_proj(x.reshape(b * l, d), p["norm_mix"][1], p["w_nsa"], tm)
    r3 = lambda a: a.reshape(b, l, a.shape[-1])
    q, kvc, kvs, kvw, gate = r3(q), r3(kvc), r3(kvs), r3(kvw), r3(gate)
    lt = nsa_lt_dense(kvc, p["lt_wk"], p["lt_wv"], 256)
    kc, vc = cmp_kv(lt, p)
    n_slc = -(-l // SLC_BLOCK)
    o_c, sel = cmp_attn_select(q, gate, kc, vc, tq, 0, n_slc)
    o_s = nsa_attn("slc", q, gate, kvs, sel, tq, tk, 0, 0)
    o_w = nsa_attn("win", q, gate, kvw, None, tq, tk, 0, 0)
    keep = min(WINDOW, l)
    flat = lambda a: a.reshape(b * l, a.shape[-1])
    return [flat(o_c), flat(o_s), flat(o_w)], _kv5(kvc), _kv5(kvs), _kv5(kvw[:, l - keep:])


def nsa_sample(x, cache_cmp, cache_slc, past_win, page_table, p, n_pg):
    b, lq, d = x.shape
    tq = 8
    assert lq <= tq
    npool, page = cache_cmp.shape[:2]
    w = 2 * NSA_KV_COLS
    p0 = page_table.shape[1] * page
    xp = _pad_time(x, tq).reshape(b * tq, d)
    q, kvc, kvs, kvw, gate = norm_proj(xp, p["norm_mix"][1], p["w_nsa"], b * tq)
    r3 = lambda a: a.reshape(b, tq, a.shape[-1])
    q, kvc, kvs, kvw, gate = r3(q), r3(kvc), r3(kvs), r3(kvw), r3(gate)
    lt = nsa_lt_paged(cache_cmp.reshape(npool, page, w), page_table, p["lt_wk"], p["lt_wv"], n_pg)
    kc, vc = cmp_kv(lt, p)
    n_slc = -(-(p0 + lq) // SLC_BLOCK)
    o_c, sel = cmp_attn_select(q, gate, kc, vc, tq, p0, n_slc)
    nbp = sel.shape[2] // NSA_KV_HEADS
    selrows = sel.reshape(b, tq, NSA_KV_HEADS, 1, nbp).transpose(0, 2, 3, 1, 4)
    selrows = jnp.broadcast_to(selrows, (b, NSA_KV_HEADS, NSA_GROUP, tq, nbp)).reshape(b, NSA_HEADS * tq, nbp)
    o_s = slc_paged(q, gate, selrows, kvs, cache_slc.reshape(npool, page, w), page_table, n_pg, lq, p0)
    n_past = past_win.shape[1]
    win_all = jnp.concatenate([past_win.reshape(b, n_past, w), kvw[:, :lq]], axis=1)
    nkeys = -(-(n_past + lq) // 8) * 8
    o_w = nsa_attn("win", q, gate, _pad_time(win_all, nkeys), None, tq, nkeys, p0, p0 - n_past)
    keep = min(WINDOW, n_past + lq)
    flat = lambda a: a[:, :lq].reshape(b * lq, a.shape[-1])
    return ([flat(o_c), flat(o_s), flat(o_w)], _kv5(kvc[:, :lq]), _kv5(kvs[:, :lq]),
            _kv5(win_all[:, n_past + lq - keep:]))


def _trunk(x, ssd_state0, ssd_conv0, gdn_state0, gdn_conv0, nsa_fn, p, tm):
    b, l, d = x.shape
    t = b * l
    tm = min(tm, t)
    y_ssd, y_gdn, ssd_state, ssd_conv, gdn_state, gdn_conv = layer0_mixer(
        x, ssd_state0, ssd_conv0, gdn_state0, gdn_conv0, p, tm)
    x1 = mix_mlp(x.reshape(t, d), [[y_ssd], [y_gdn]], p["w_out_ab"], p["norm_mlp"][0], p["w_up"][0],
                 p["w_down"][0], None, tm)
    branches, cmp_rows, slc_rows, win_rows = nsa_fn(x1.reshape(b, l, d))
    y = mix_mlp(x1, [branches], [p["w_out_nsa"]], p["norm_mlp"][1], p["w_up"][1], p["w_down"][1],
                p["norm_final"], tm)
    return y.reshape(b, l, d), ssd_state, ssd_conv, gdn_state, gdn_conv, cmp_rows, slc_rows, win_rows


def kernel(x_prompt, x_sample, state_ssd, state_ssd_conv, state_gdn, state_gdn_conv, cache_cmp_kv, cache_slc_kv,
           state_win_kv, page_table, norm_mix, norm_mlp, norm_final, w_in_ab, ssd_conv_w, ssd_conv_b, ssd_dt_bias,
           ssd_a_log, ssd_d, ssd_norm_w, gdn_conv_w, gdn_dt_bias, gdn_a_log, gdn_norm_w, w_out_ab, w_in_nsa,
           cmp_pe_k, cmp_w1_k, cmp_b1_k, cmp_w2_k, cmp_pe_v, cmp_w1_v, cmp_b1_v, cmp_w2_v, w_out_nsa, w_up, w_down):
    wo = w_out_ab.astype(BF16)
    p = dict(
        norm_mix=norm_mix, norm_mlp=norm_mlp, norm_final=norm_final, w_ab=_prep_ab_weights(w_in_ab),
        ssd_conv_w=ssd_conv_w, ssd_conv_b=ssd_conv_b, ssd_dt_bias=ssd_dt_bias, ssd_a_log=ssd_a_log, ssd_d=ssd_d,
        ssd_norm_w=ssd_norm_w, gdn_conv_w=gdn_conv_w, gdn_dt_bias=gdn_dt_bias, gdn_a_log=gdn_a_log,
        gdn_norm_w=gdn_norm_w, w_out_ab=[wo[:SSD_INNER], wo[SSD_INNER:]], w_nsa=_prep_nsa_weights(w_in_nsa),
        lt_wk=_lt_weights(cmp_w1_k), lt_wv=_lt_weights(cmp_w1_v),
        cmp_pe_k=cmp_pe_k, cmp_w1_k=cmp_w1_k, cmp_b1_k=cmp_b1_k, cmp_w2_k=cmp_w2_k,
        cmp_pe_v=cmp_pe_v, cmp_w1_v=cmp_w1_v, cmp_b1_v=cmp_b1_v, cmp_w2_v=cmp_w2_v,
        w_out_nsa=w_out_nsa.astype(BF16), w_up=w_up.astype(BF16), w_down=w_down.astype(BF16))
    bp = x_prompt.shape[0]
    zeros = lambda *s: jnp.zeros(s, F32)
    tm = 256
    out_p = _trunk(x_prompt, zeros(bp, SSD_HEADS, SSD_HEADDIM, SSD_STATE), zeros(bp, CONV_W - 1, SSD_CONV_DIM),
                   zeros(bp, GDN_V_HEADS, GDN_HEAD_DIM, GDN_HEAD_DIM), zeros(bp, CONV_W - 1, GDN_CONV_DIM),
                   lambda x1: nsa_prompt(x1, p, tm, 128, 256), p, tm)
    out_s = _trunk(x_sample, state_ssd, state_ssd_conv, state_gdn, state_gdn_conv,
                   lambda x1: nsa_sample(x1, cache_cmp_kv, cache_slc_kv, state_win_kv, page_table, p, 16), p, tm)
    return tuple(v for pair in zip(out_p, out_s) for v in pair)
```

```python
import functools

import jax
import jax.numpy as jnp
from jax import lax
from jax.experimental import pallas as pl
from jax.experimental.pallas import tpu as pltpu

F32 = jnp.float32
BF16 = jnp.bfloat16

RMS_EPS = 1e-6
NEG_INF = -1e30
MASKED = 2.0 * NEG_INF
FORCE_BONUS = 1e4
CONV_W = 4
CHUNK = 64
SSD_HEADS, SSD_HEADDIM, SSD_STATE, SSD_GROUPS = 16, 64, 128, 2
SSD_INNER = SSD_HEADS * SSD_HEADDIM
SSD_CONV_DIM = SSD_INNER + 2 * SSD_GROUPS * SSD_STATE
GDN_QK_HEADS, GDN_V_HEADS, GDN_HEAD_DIM = 4, 8, 128
GDN_QK_DIM = GDN_QK_HEADS * GDN_HEAD_DIM
GDN_V_DIM = GDN_V_HEADS * GDN_HEAD_DIM
GDN_CONV_DIM = 2 * GDN_QK_DIM + GDN_V_DIM
NSA_HEADS, NSA_KV_HEADS, NSA_HEAD_DIM = 16, 4, 64
NSA_GROUP = NSA_HEADS // NSA_KV_HEADS
NSA_KV_COLS = NSA_KV_HEADS * NSA_HEAD_DIM
CMP_STRIDE, CMP_LEN, CMP_HIDDEN = 16, 32, 64
SLC_BLOCK, SLC_TOPN, WINDOW = 64, 16, 512
PAGE_SIZE = 128
LANES = 128
VMEM_LIMIT = 56 * 1024 * 1024


def _cparams(sem):
    return pltpu.CompilerParams(dimension_semantics=sem, vmem_limit_bytes=VMEM_LIMIT)


def _dot(a, b):
    return jnp.dot(a, b, preferred_element_type=F32)


def _dot_nt(a, b):
    return lax.dot_general(a, b, (((1,), (1,)), ((), ())), preferred_element_type=F32)


def _dot_tn(a, b):
    return lax.dot_general(a, b, (((0,), (0,)), ((), ())), preferred_element_type=F32)


def _split2(x):
    hi = x.astype(BF16)
    lo = (x - hi.astype(F32)).astype(BF16)
    return hi, lo


def _split3(x):
    hi = x.astype(BF16)
    r = x - hi.astype(F32)
    mid = r.astype(BF16)
    lo = (r - mid.astype(F32)).astype(BF16)
    return hi, mid, lo


def _mm3(a, b):
    ah, al = _split2(a)
    bh, bl = _split2(b)
    return _dot(ah, bh) + (_dot(ah, bl) + _dot(al, bh))


def _rms_unit(x):
    return x * lax.rsqrt(jnp.mean(x * x, axis=-1, keepdims=True) + RMS_EPS)


def _silu(x):
    return x * jax.nn.sigmoid(x)


def _softplus(x):
    return jnp.maximum(x, 0.0) + jnp.log1p(jnp.exp(-jnp.abs(x)))


def _iota2(shape, axis):
    return lax.broadcasted_iota(jnp.int32, shape, axis)


def _norm_proj_body(x_ref, g_ref, *refs, n_out, n_t):
    n_w = n_out + n_t
    w_refs, o_refs = refs[:n_w], refs[n_w:]
    hn = (_rms_unit(x_ref[...]) * g_ref[...]).astype(BF16)
    for w_ref, o_ref in zip(w_refs[:n_out], o_refs[:n_out]):
        n = w_ref.shape[1]
        for c0 in range(0, n, 512):
            c1 = min(n, c0 + 512)
            o_ref[:, c0:c1] = _dot(hn, w_ref[:, c0:c1])
    for w_ref, o_ref in zip(w_refs[n_out:], o_refs[n_out:]):
        n = w_ref.shape[0]
        for c0 in range(0, n, 512):
            c1 = min(n, c0 + 512)
            o_ref[0, c0:c1, :] = _dot_nt(w_ref[c0:c1, :], hn)


def norm_proj(x, g, ws, tm, wts=()):
    t, d = x.shape
    assert t % tm == 0
    in_specs = [pl.BlockSpec((tm, d), lambda i: (i, 0)), pl.BlockSpec((1, d), lambda i: (0, 0))]
    in_specs += [pl.BlockSpec(w.shape, lambda i: (0, 0)) for w in (*ws, *wts)]
    out_specs = [pl.BlockSpec((tm, w.shape[1]), lambda i: (i, 0)) for w in ws]
    out_specs += [pl.BlockSpec((1, w.shape[0], tm), lambda i: (i, 0, 0)) for w in wts]
    out_shape = [jax.ShapeDtypeStruct((t, w.shape[1]), F32) for w in ws]
    out_shape += [jax.ShapeDtypeStruct((t // tm, w.shape[0], tm), F32) for w in wts]
    return pl.pallas_call(
        functools.partial(_norm_proj_body, n_out=len(ws), n_t=len(wts)),
        grid=(t // tm,), in_specs=in_specs, out_specs=out_specs, out_shape=out_shape,
        compiler_params=_cparams(("parallel",)), name="norm_proj",
    )(x, g.reshape(1, d), *ws, *wts)


def _mix_mlp_body(x_ref, *refs, group_sizes, transposed, final, hc):
    n_y, n_g = sum(group_sizes), len(group_sizes)
    y_refs = refs[:n_y]
    wo_refs = refs[n_y:n_y + n_g]
    g_ref, wu_ref, wd_ref = refs[n_y + n_g:n_y + n_g + 3]
    rest = refs[n_y + n_g + 3:]
    gf_ref = rest[0] if final else None
    o_ref = rest[-1]
    x1 = x_ref[...]
    i = 0
    for gs, wo_ref in zip(group_sizes, wo_refs):
        rd = (lambda r: r[0]) if transposed else (lambda r: r[...])
        y = rd(y_refs[i])
        for r in y_refs[i + 1:i + gs]:
            y = y + rd(r)
        i += gs
        x1 = x1 + (_dot_tn if transposed else _dot)(y.astype(BF16), wo_ref[...])
    hn = (_rms_unit(x1) * g_ref[...]).astype(BF16)
    acc = x1
    hidden = wu_ref.shape[1]
    for c0 in range(0, hidden, hc):
        h = jnp.maximum(_dot(hn, wu_ref[:, c0:c0 + hc]), 0.0)
        acc = acc + _dot((h * h).astype(BF16), wd_ref[c0:c0 + hc, :])
    if final:
        acc = _rms_unit(acc) * gf_ref[...]
    o_ref[...] = acc


def mix_mlp(x, y_groups, w_outs, g, w_up, w_down, g_final, tm, hc=None, transposed=False):
    t, d = x.shape
    hc = hc or w_up.shape[1]
    assert t % tm == 0
    final = g_final is not None
    const = lambda i: (0, 0)
    ys = [y for grp in y_groups for y in grp]
    in_specs = [pl.BlockSpec((tm, d), lambda i: (i, 0))]
    if transposed:
        assert all(y.shape[0] == t // tm and y.shape[2] == tm for y in ys)
        in_specs += [pl.BlockSpec((1, y.shape[1], tm), lambda i: (i, 0, 0)) for y in ys]
    else:
        in_specs += [pl.BlockSpec((tm, y.shape[1]), lambda i: (i, 0)) for y in ys]
    in_specs += [pl.BlockSpec(w.shape, const) for w in w_outs]
    in_specs += [pl.BlockSpec((1, d), const), pl.BlockSpec(w_up.shape, const), pl.BlockSpec(w_down.shape, const)]
    args = [x, *ys, *w_outs, g.reshape(1, d), w_up, w_down]
    if final:
        in_specs.append(pl.BlockSpec((1, d), const))
        args.append(g_final.reshape(1, d))
    return pl.pallas_call(
        functools.partial(_mix_mlp_body, group_sizes=tuple(len(grp) for grp in y_groups), transposed=transposed,
                          final=final, hc=hc),
        grid=(t // tm,), in_specs=in_specs, out_specs=pl.BlockSpec((tm, d), lambda i: (i, 0)),
        out_shape=jax.ShapeDtypeStruct((t, d), F32),
        compiler_params=_cparams(("parallel",)), name="mix_mlp",
    )(*args)


def _causal_conv(x, xp_ref, w_ref, first, conv0_ref):
    c = x.shape[0]

    @pl.when(first)
    def _():
        xp_ref[5:8, :] = conv0_ref[0]

    xp_ref[8:8 + c, :] = x
    y = xp_ref[5:5 + c, :] * w_ref[0:1, :]
    for k in range(1, CONV_W):
        y = y + xp_ref[5 + k:5 + k + c, :] * w_ref[k:k + 1, :]
    xp_ref[5:8, :] = x[c - 3:c, :]
    return y


def _cumsum_pair(col, row):
    c = col.shape[0]
    ri, ci = _iota2((c, c), 0), _iota2((c, c), 1)
    tril = jnp.where(ri >= ci, 1.0, 0.0).astype(BF16)
    triu = jnp.where(ri <= ci, 1.0, 0.0).astype(BF16)
    ch, cm, cl = _split3(col)
    rh, rm, rl = _split3(row)
    cs = _dot(tril, ch) + (_dot(tril, cm) + _dot(tril, cl))
    cst = _dot(rh, triu) + (_dot(rm, triu) + _dot(rl, triu))
    return cs, cst


def _ssd_body(xbc_ref, z_ref, sm_ref, smt_ref, st0_ref, cv0_ref, cw_ref, cb_ref, dtb_r_ref, dtb_c_ref,
              al_r_ref, al_c_ref, dfull_ref, nw_ref, y_ref, st_ref, xp_ref, ybuf_ref, *, l_valid):
    cidx = pl.program_id(1)
    c = CHUNK
    n, p = SSD_STATE, SSD_HEADDIM

    @pl.when(cidx == 0)
    def _():
        st_ref[0] = st0_ref[0]

    conv = _causal_conv(xbc_ref[0], xp_ref, cw_ref, cidx == 0, cv0_ref) + cb_ref[...]
    act = _silu(conv)
    xs = act[:, :SSD_INNER]
    bm = act[:, SSD_INNER:SSD_INNER + SSD_GROUPS * n]
    cm = act[:, SSD_INNER + SSD_GROUPS * n:]

    valid_c = (cidx * c + _iota2((c, 1), 0)) < l_valid
    valid_r = (cidx * c + _iota2((1, c), 1)) < l_valid
    dt = jnp.where(valid_c, _softplus(sm_ref[0][:, 0:SSD_HEADS] + dtb_r_ref[...]), 0.0)
    dtt = jnp.where(valid_r, _softplus(smt_ref[0, 0][0:SSD_HEADS, :] + dtb_c_ref[...]), 0.0)
    cs, cst = _cumsum_pair(dt * (-jnp.exp(al_r_ref[...])), dtt * (-jnp.exp(al_c_ref[...])))
    cs_last = cs[c - 1:c, :]
    ecs = jnp.exp(cs)
    to_end = jnp.exp(cs_last - cs) * dt
    edec = jnp.exp(cs_last)
    causal = _iota2((c, c), 0) >= _iota2((c, c), 1)

    heads_per_group = SSD_HEADS // SSD_GROUPS
    heads = range(SSD_HEADS)
    bm_l = [bm[:, g * n:(g + 1) * n].astype(BF16) for g in range(SSD_GROUPS)]
    cm_l = [cm[:, g * n:(g + 1) * n].astype(BF16) for g in range(SSD_GROUPS)]
    cb_l = [_dot_nt(c_g, b_g) for c_g, b_g in zip(cm_l, bm_l)]
    grp = lambda h: h // heads_per_group
    x_l = [xs[:, h * p:(h + 1) * p] for h in heads]
    s_l = [st_ref[0, h] for h in heads]
    sc_l = []
    for h in heads:
        seg = cs[:, h:h + 1] - cst[h:h + 1, :]
        decay = jnp.exp(jnp.where(causal, seg, NEG_INF))
        sc_l.append((cb_l[grp(h)] * decay * dtt[h:h + 1, :]).astype(BF16))
    yd_l = [_dot(sc_l[h], x_l[h].astype(BF16)) for h in heads]
    yo_l = [_dot_nt(cm_l[grp(h)], s_l[h].astype(BF16)) * ecs[:, h:h + 1] for h in heads]
    for h in heads:
        ybuf_ref[:, h * p:(h + 1) * p] = yd_l[h] + yo_l[h]
    for h in heads:
        xw = (x_l[h] * to_end[:, h:h + 1]).astype(BF16)
        st_ref[0, h] = s_l[h] * edec[:, h:h + 1] + _dot_tn(xw, bm_l[grp(h)])

    y = ybuf_ref[...] + dfull_ref[...] * xs
    yz = y * _silu(z_ref[0])
    gw = SSD_INNER // SSD_GROUPS
    for g in range(SSD_GROUPS):
        seg = yz[:, g * gw:(g + 1) * gw]
        y_ref[0, :, g * gw:(g + 1) * gw] = _rms_unit(seg) * nw_ref[:, g * gw:(g + 1) * gw]


def ssd_mixer(xbc, z, small, small_t, state0, conv0, conv_w, conv_b, dt_bias, a_log, d_skip, norm_w, l_valid):
    b, lp, _ = xbc.shape
    nc = lp // CHUNK
    h = SSD_HEADS
    d_full = jnp.repeat(d_skip, SSD_HEADDIM).reshape(1, SSD_INNER)
    const2 = lambda i, j: (0, 0)
    tok = lambda w: pl.BlockSpec((1, CHUNK, w), lambda i, j: (i, j, 0))
    in_specs = [
        tok(SSD_CONV_DIM), tok(SSD_INNER), tok(LANES),
        pl.BlockSpec((1, 1, 32, CHUNK), lambda i, j: (i, j, 0, 0)),
        pl.BlockSpec((1, h, SSD_HEADDIM, SSD_STATE), lambda i, j: (i, 0, 0, 0)),
        pl.BlockSpec((1, CONV_W - 1, SSD_CONV_DIM), lambda i, j: (i, 0, 0)),
        pl.BlockSpec((CONV_W, SSD_CONV_DIM), const2), pl.BlockSpec((1, SSD_CONV_DIM), const2),
        pl.BlockSpec((1, h), const2), pl.BlockSpec((h, 1), const2),
        pl.BlockSpec((1, h), const2), pl.BlockSpec((h, 1), const2),
        pl.BlockSpec((1, SSD_INNER), const2), pl.BlockSpec((1, SSD_INNER), const2),
    ]
    out_specs = [tok(SSD_INNER), pl.BlockSpec((1, h, SSD_HEADDIM, SSD_STATE), lambda i, j: (i, 0, 0, 0))]
    out_shape = [jax.ShapeDtypeStruct((b, lp, SSD_INNER), F32),
                 jax.ShapeDtypeStruct((b, h, SSD_HEADDIM, SSD_STATE), F32)]
    return pl.pallas_call(
        functools.partial(_ssd_body, l_valid=l_valid),
        grid=(b, nc), in_specs=in_specs, out_specs=out_specs, out_shape=out_shape,
        scratch_shapes=[pltpu.VMEM((CHUNK + 8, SSD_CONV_DIM), F32), pltpu.VMEM((CHUNK, SSD_INNER), F32)],
        compiler_params=_cparams(("parallel", "arbitrary")), name="ssd_mixer",
    )(xbc, z, small, small_t, state0, conv0, conv_w, conv_b.reshape(1, -1),
      dt_bias.reshape(1, h), dt_bias.reshape(h, 1), a_log.reshape(1, h), a_log.reshape(h, 1),
      d_full, norm_w.reshape(1, SSD_INNER))


def _unit_lower_inverses(mats):
    c = mats[0].shape[0]
    ri, ci = _iota2((c, c), 0), _iota2((c, c), 1)
    eye = jnp.where(ri == ci, 1.0, 0.0)
    same_blk = lax.shift_right_logical(ri, 4) == lax.shift_right_logical(ci, 4)
    d = [jnp.where(same_blk, a, 0.0) for a in mats]
    nn = [a - x for a, x in zip(mats, d)]
    d2 = [_mm3(x, x) for x in d]
    d4 = [_mm3(x, x) for x in d2]
    pinv = [eye - x for x in d]
    pinv = [x + _mm3(x, y) for x, y in zip(pinv, d2)]
    d8 = [_mm3(x, x) for x in d4]
    pinv = [x + _mm3(x, y) for x, y in zip(pinv, d4)]
    pinv = [x + _mm3(x, y) for x, y in zip(pinv, d8)]
    e = [_mm3(x, y) for x, y in zip(pinv, nn)]
    e2 = [_mm3(x, x) for x in e]
    q = [eye - x for x in e]
    q = [x + _mm3(x, y) for x, y in zip(q, e2)]
    return [_mm3(x, y) for x, y in zip(q, pinv)]


def _l2norm(x):
    return x * lax.rsqrt(jnp.sum(x * x, axis=-1, keepdims=True) + 1e-6)


def _gdn_body(qkv_ref, z_ref, sm_ref, smt_ref, st0_ref, cv0_ref, cw_ref, dtb_r_ref, dtb_c_ref,
              al_r_ref, al_c_ref, nw_ref, o_ref, st_ref, xp_ref, *, l_valid):
    cidx = pl.program_id(1)
    c = CHUNK
    hd = GDN_HEAD_DIM
    nh = GDN_V_HEADS

    @pl.when(cidx == 0)
    def _():
        st_ref[0] = st0_ref[0]

    act = _silu(_causal_conv(qkv_ref[0], xp_ref, cw_ref, cidx == 0, cv0_ref))
    valid_c = (cidx * c + _iota2((c, 1), 0)) < l_valid
    valid_r = (cidx * c + _iota2((1, c), 1)) < l_valid
    sm = sm_ref[0]
    smt = smt_ref[0, 0]
    beta = jnp.where(valid_c, jax.nn.sigmoid(sm[:, 16:16 + nh]), 0.0)
    g_c = jnp.where(valid_c, -jnp.exp(al_r_ref[...]) * _softplus(sm[:, 24:24 + nh] + dtb_r_ref[...]), 0.0)
    g_r = jnp.where(valid_r, -jnp.exp(al_c_ref[...]) * _softplus(smt[24:24 + nh, :] + dtb_c_ref[...]), 0.0)
    cs, cst = _cumsum_pair(g_c, g_r)
    cs_last = cs[c - 1:c, :]
    ecs = jnp.exp(cs)
    e_end = jnp.exp(cs_last - cs)
    etot = jnp.exp(cs_last)
    ri, ci = _iota2((c, c), 0), _iota2((c, c), 1)
    causal = ri >= ci
    strict = ri > ci
    rep = GDN_V_HEADS // GDN_QK_HEADS

    eye = jnp.where(ri == ci, 1.0, 0.0)
    q_l, k_l, qk_l, decay_l, a_l = [], [], [], [], []
    for hq in range(GDN_QK_HEADS):
        q_h = _l2norm(act[:, hq * hd:(hq + 1) * hd]) * (hd ** -0.5)
        k_h = _l2norm(act[:, GDN_QK_DIM + hq * hd:GDN_QK_DIM + (hq + 1) * hd])
        k16 = k_h.astype(BF16)
        kk = _dot_nt(k16, k16)
        qk0 = _dot_nt(q_h.astype(BF16), k16)
        q_l.append(q_h)
        k_l.append(k_h)
        for r in range(rep):
            h = hq * rep + r
            seg = cs[:, h:h + 1] - cst[h:h + 1, :]
            decay = jnp.exp(jnp.where(causal, seg, NEG_INF))
            decay_l.append(decay)
            qk_l.append((qk0 * decay).astype(BF16))
            a_l.append(jnp.where(strict, kk * beta[:, h:h + 1] * decay, 0.0))
    tinv_l = _unit_lower_inverses(a_l)
    heads = range(nh)
    rhs_l = []
    for h in heads:
        b_h = beta[:, h:h + 1]
        v_h = act[:, 2 * GDN_QK_DIM + h * hd:2 * GDN_QK_DIM + (h + 1) * hd]
        rhs_l.append(jnp.concatenate([v_h * b_h, k_l[h // rep] * (b_h * ecs[:, h:h + 1])], axis=1))
    sol_l = [rhs_l[h] + _dot((tinv_l[h] - eye).astype(BF16), rhs_l[h].astype(BF16)) for h in heads]
    s_l = [st_ref[0, h] for h in heads]
    s16_l = [s.astype(BF16) for s in s_l]
    o1_l = [_dot((q_l[h // rep] * ecs[:, h:h + 1]).astype(BF16), s16_l[h]) for h in heads]
    v16_l = [(sol_l[h][:, :hd] - _dot(sol_l[h][:, hd:].astype(BF16), s16_l[h])).astype(BF16) for h in heads]
    o_l = [o1_l[h] + _dot(qk_l[h], v16_l[h]) for h in heads]
    for h in heads:
        k_dec = (k_l[h // rep] * e_end[:, h:h + 1]).astype(BF16)
        st_ref[0, h] = s_l[h] * etot[:, h:h + 1] + _dot_tn(k_dec, v16_l[h])
    for h in heads:
        o_ref[0, :, h * hd:(h + 1) * hd] = ((_rms_unit(o_l[h]) * nw_ref[...])
                                            * _silu(z_ref[0, :, h * hd:(h + 1) * hd]))


def gdn_mixer(qkv, z, small, small_t, state0, conv0, conv_w, dt_bias, a_log, norm_w, l_valid):
    b, lp, _ = qkv.shape
    nc = lp // CHUNK
    h = GDN_V_HEADS
    const2 = lambda i, j: (0, 0)
    tok = lambda w: pl.BlockSpec((1, CHUNK, w), lambda i, j: (i, j, 0))
    in_specs = [
        tok(GDN_CONV_DIM), tok(GDN_V_DIM), tok(LANES),
        pl.BlockSpec((1, 1, 32, CHUNK), lambda i, j: (i, j, 0, 0)),
        pl.BlockSpec((1, h, GDN_HEAD_DIM, GDN_HEAD_DIM), lambda i, j: (i, 0, 0, 0)),
        pl.BlockSpec((1, CONV_W - 1, GDN_CONV_DIM), lambda i, j: (i, 0, 0)),
        pl.BlockSpec((CONV_W, GDN_CONV_DIM), const2),
        pl.BlockSpec((1, h), const2), pl.BlockSpec((h, 1), const2),
        pl.BlockSpec((1, h), const2), pl.BlockSpec((h, 1), const2),
        pl.BlockSpec((1, GDN_HEAD_DIM), const2),
    ]
    out_specs = [tok(GDN_V_DIM), pl.BlockSpec((1, h, GDN_HEAD_DIM, GDN_HEAD_DIM), lambda i, j: (i, 0, 0, 0))]
    out_shape = [jax.ShapeDtypeStruct((b, lp, GDN_V_DIM), F32),
                 jax.ShapeDtypeStruct((b, h, GDN_HEAD_DIM, GDN_HEAD_DIM), F32)]
    return pl.pallas_call(
        functools.partial(_gdn_body, l_valid=l_valid),
        grid=(b, nc), in_specs=in_specs, out_specs=out_specs, out_shape=out_shape,
        scratch_shapes=[pltpu.VMEM((CHUNK + 8, GDN_CONV_DIM), F32)],
        compiler_params=_cparams(("parallel", "arbitrary")), name="gdn_mixer",
    )(qkv, z, small, small_t, state0, conv0, conv_w,
      dt_bias.reshape(1, h), dt_bias.reshape(h, 1), a_log.reshape(1, h), a_log.reshape(h, 1),
      norm_w.reshape(1, GDN_HEAD_DIM))


def _prep_ab_weights(w_in_ab):
    o = [0, SSD_INNER]
    o.append(o[-1] + SSD_CONV_DIM)
    o.append(o[-1] + SSD_HEADS)
    o.append(o[-1] + GDN_CONV_DIM)
    o.append(o[-1] + GDN_V_DIM)
    o.append(o[-1] + 2 * GDN_V_HEADS)
    w = w_in_ab.astype(BF16)
    small = jnp.concatenate([w[:, o[2]:o[3]], w[:, o[5]:o[6]]], axis=1)
    small = jnp.pad(small, ((0, 0), (0, LANES - small.shape[1])))
    return [w[:, o[0]:o[1]], w[:, o[1]:o[2]], w[:, o[3]:o[4]], w[:, o[4]:o[5]], small]


def _pad_time(a, lp):
    return jnp.pad(a, [(0, 0), (0, lp - a.shape[1])] + [(0, 0)] * (a.ndim - 2))


def layer0_mixer(x, ssd_state0, ssd_conv0, gdn_state0, gdn_conv0, p, tm):
    b, l, d = x.shape
    lp = -(-l // CHUNK) * CHUNK
    xp = _pad_time(x, lp).reshape(b * lp, d)
    z_ssd, xbc, qkv, z_gdn, small = norm_proj(xp, p["norm_mix"][0], p["w_ab"], min(tm, b * lp))
    nc = lp // CHUNK
    small_t = small[:, :32].reshape(b, nc, CHUNK, 32).transpose(0, 1, 3, 2)
    r3 = lambda a: a.reshape(b, lp, a.shape[-1])
    xbc, qkv, small = r3(xbc), r3(qkv), r3(small)
    y_ssd, ssd_state = ssd_mixer(xbc, r3(z_ssd), small, small_t, ssd_state0, ssd_conv0, p["ssd_conv_w"],
                                 p["ssd_conv_b"], p["ssd_dt_bias"], p["ssd_a_log"], p["ssd_d"], p["ssd_norm_w"], l)
    y_gdn, gdn_state = gdn_mixer(qkv, r3(z_gdn), small, small_t, gdn_state0, gdn_conv0, p["gdn_conv_w"],
                                 p["gdn_dt_bias"], p["gdn_a_log"], p["gdn_norm_w"], l)
    ssd_conv = jnp.concatenate([ssd_conv0, xbc[:, :l]], axis=1)[:, l:]
    gdn_conv = jnp.concatenate([gdn_conv0, qkv[:, :l]], axis=1)[:, l:]
    y_ssd = y_ssd[:, :l].reshape(b * l, SSD_INNER)
    y_gdn = y_gdn[:, :l].reshape(b * l, GDN_V_DIM)
    return y_ssd, y_gdn, ssd_state, ssd_conv, gdn_state, gdn_conv


def _nsa_lt_body(*refs, n_in, n_prefetch=0):
    refs = refs[n_prefetch:]
    x_refs = refs[:n_in]
    wk_ref, wv_ref, o_ref = refs[n_in:]
    w = 2 * NSA_KV_COLS
    acc_k = acc_v = None
    for j in range(CMP_STRIDE):
        xk = jnp.concatenate([r[0, :, j * w:j * w + NSA_KV_COLS] for r in x_refs], axis=0).astype(BF16)
        xv = jnp.concatenate([r[0, :, j * w + NSA_KV_COLS:(j + 1) * w] for r in x_refs], axis=0).astype(BF16)
        dk, dv = _dot(xk, wk_ref[j]), _dot(xv, wv_ref[j])
        acc_k = dk if acc_k is None else acc_k + dk
        acc_v = dv if acc_v is None else acc_v + dv
    o_ref[0, :, 0:2 * NSA_KV_COLS] = acc_k
    o_ref[0, :, 2 * NSA_KV_COLS:] = acc_v


def _lt_weights(w1):
    w1r = w1.reshape(2, CMP_STRIDE, NSA_HEAD_DIM, CMP_HIDDEN)
    eye = jnp.eye(NSA_KV_HEADS, dtype=w1.dtype)
    lead = jnp.einsum("ab,jdh->jadbh", eye, w1r[0]).reshape(CMP_STRIDE, NSA_KV_COLS, NSA_KV_COLS)
    tail = jnp.einsum("ab,jdh->jadbh", eye, w1r[1]).reshape(CMP_STRIDE, NSA_KV_COLS, NSA_KV_COLS)
    return jnp.concatenate([lead, tail], axis=-1).astype(BF16)


def nsa_lt_dense(kv, wk, wv, tc):
    b, l, w = kv.shape
    nch = l // CMP_STRIDE
    x = kv[:, :nch * CMP_STRIDE].reshape(b, nch, CMP_STRIDE * w)
    tc = min(tc, nch)
    assert nch % tc == 0
    const3 = lambda i, j: (0, 0, 0)
    return pl.pallas_call(
        functools.partial(_nsa_lt_body, n_in=1), grid=(b, nch // tc),
        in_specs=[pl.BlockSpec((1, tc, CMP_STRIDE * w), lambda i, j: (i, j, 0)),
                  pl.BlockSpec(wk.shape, const3), pl.BlockSpec(wv.shape, const3)],
        out_specs=pl.BlockSpec((1, tc, 4 * NSA_KV_COLS), lambda i, j: (i, j, 0)),
        out_shape=jax.ShapeDtypeStruct((b, nch, 4 * NSA_KV_COLS), F32),
        compiler_params=_cparams(("parallel", "parallel")), name="nsa_lt_dense",
    )(x, wk, wv)


def nsa_lt_paged(cache, page_table, wk, wv, n_pg):
    npool, page, w = cache.shape
    b, n_pages = page_table.shape
    rows = page // CMP_STRIDE
    x = cache.reshape(npool, rows, CMP_STRIDE * w)
    assert n_pages % n_pg == 0
    page_spec = lambda i: pl.BlockSpec((1, rows, CMP_STRIDE * w), lambda bi, gi, pt: (pt[bi, gi * n_pg + i], 0, 0))
    const3 = lambda bi, gi, pt: (0, 0, 0)
    grid_spec = pltpu.PrefetchScalarGridSpec(
        num_scalar_prefetch=1, grid=(b, n_pages // n_pg),
        in_specs=[page_spec(i) for i in range(n_pg)] + [pl.BlockSpec(wk.shape, const3), pl.BlockSpec(wv.shape, const3)],
        out_specs=pl.BlockSpec((1, n_pg * rows, 4 * NSA_KV_COLS), lambda bi, gi, pt: (bi, gi, 0)))
    return pl.pallas_call(
        functools.partial(_nsa_lt_body, n_in=n_pg, n_prefetch=1), grid_spec=grid_spec,
        out_shape=jax.ShapeDtypeStruct((b, n_pages * rows, 4 * NSA_KV_COLS), F32),
        compiler_params=_cparams(("parallel", "parallel")), name="nsa_lt_paged",
    )(page_table, *([x] * n_pg), wk, wv)


def _cmp_kv_body(lt_ref, pek_ref, w1k_ref, b1k_ref, w2k_ref, pev_ref, w1v_ref, b1v_ref, w2v_ref, w2vt_ref,
                 kc_ref, vc_ref, vct_ref, sh_ref):
    nch = lt_ref.shape[1]
    kvc = NSA_KV_COLS

    def branch(off, pe_ref, w1_ref, b1_ref, w2_ref, out_ref):
        pe = jnp.broadcast_to(pe_ref[...], (8, pe_ref.shape[1]))
        pe_term = _mm3(pe, w1_ref[...])[0:1, :] + b1_ref[...]
        pe4 = jnp.concatenate([pe_term] * NSA_KV_HEADS, axis=1)
        sh_ref[0:nch, :] = lt_ref[0, :, off + kvc:off + 2 * kvc]
        sh_ref[nch:nch + 8, :] = jnp.zeros((8, kvc), F32)
        hid = _silu(lt_ref[0, :, off:off + kvc] + sh_ref[1:nch + 1, :] + pe4).astype(BF16)
        out_ref[0] = _dot(hid, w2_ref[...])
        return hid

    branch(0, pek_ref, w1k_ref, b1k_ref, w2k_ref, kc_ref)
    hid_v = branch(2 * kvc, pev_ref, w1v_ref, b1v_ref, w2v_ref, vc_ref)
    vct_ref[0] = _dot_nt(w2vt_ref[...], hid_v)


def cmp_kv(lt, p):
    b, nch, _ = lt.shape
    kvc = NSA_KV_COLS
    const2 = lambda i: (0, 0)
    args, specs = [], []
    for s in ("k", "v"):
        pe = p["cmp_pe_" + s].reshape(1, -1)
        w1 = p["cmp_w1_" + s]
        b1 = p["cmp_b1_" + s].reshape(1, -1)
        w2 = jnp.kron(jnp.eye(NSA_KV_HEADS, dtype=F32), p["cmp_w2_" + s]).astype(BF16)
        for a in (pe, w1, b1, w2):
            args.append(a)
            specs.append(pl.BlockSpec(a.shape, const2))
    args.append(jnp.kron(jnp.eye(NSA_KV_HEADS, dtype=F32), p["cmp_w2_v"].T).astype(BF16))
    specs.append(pl.BlockSpec(args[-1].shape, const2))
    out_spec = pl.BlockSpec((1, nch, kvc), lambda i: (i, 0, 0))
    return pl.pallas_call(
        _cmp_kv_body, grid=(b,),
        in_specs=[pl.BlockSpec((1, nch, 4 * kvc), lambda i: (i, 0, 0))] + specs,
        out_specs=[out_spec, out_spec, pl.BlockSpec((1, kvc, nch), lambda i: (i, 0, 0))],
        out_shape=[jax.ShapeDtypeStruct((b, nch, kvc), F32)] * 2 + [jax.ShapeDtypeStruct((b, kvc, nch), F32)],
        scratch_shapes=[pltpu.VMEM((nch + 8, kvc), F32)],
        compiler_params=_cparams(("parallel",)), name="cmp_kv",
    )(lt, *args)


def _cmp_attn_body(q_ref, gate_ref, kc_ref, vc_ref, oc_ref, sel_ref, *, tq, pos0, n_slc, nbp):
    qi = pl.program_id(1)
    ncp = kc_ref.shape[1]
    hd = NSA_HEAD_DIM
    t_col = pos0 + qi * tq + _iota2((tq, 1), 0)
    cmask = (_iota2((1, ncp), 1) * CMP_STRIDE + (CMP_LEN - 1)) <= t_col
    kc = kc_ref[0].astype(BF16)
    vc = vc_ref[0].astype(BF16)
    gate = jax.nn.sigmoid(gate_ref[0])
    nn = _iota2((ncp, nbp), 0) * CMP_STRIDE
    ss = _iota2((ncp, nbp), 1) * SLC_BLOCK
    overlap = jnp.where((nn < ss + SLC_BLOCK) & (nn + CMP_LEN > ss), 1.0, 0.0).astype(BF16)
    blk = _iota2((1, nbp), 1)
    blk_f = blk.astype(F32)
    cur = lax.shift_right_logical(t_col, 6)
    forced = (blk == 0) | (blk == cur) | (blk == cur - 1)
    bonus = jnp.where(forced, FORCE_BONUS, 0.0)
    causal_blk = blk * SLC_BLOCK <= t_col
    n_pick = min(SLC_TOPN, n_slc)

    for k in range(NSA_KV_HEADS):
        kc_k = kc[:, k * hd:(k + 1) * hd]
        vc_k = vc[:, k * hd:(k + 1) * hd]
        psum = jnp.zeros((tq, ncp), F32)
        for g in range(NSA_GROUP):
            h = k * NSA_GROUP + g
            q_h = (q_ref[0, :, h * hd:(h + 1) * hd] * (hd ** -0.5)).astype(BF16)
            s = jnp.where(cmask, _dot_nt(q_h, kc_k), NEG_INF)
            m = jnp.max(s, axis=-1, keepdims=True)
            e = jnp.where(cmask, jnp.exp(s - m), 0.0)
            den = jnp.sum(e, axis=-1, keepdims=True)
            p = e / jnp.where(den > 0.0, den, 1.0)
            psum = psum + p
            oc_ref[0, :, h * hd:(h + 1) * hd] = _dot(p.astype(BF16), vc_k) * gate[:, h:h + 1]
        ph, plo = _split2(psum)
        imp = _dot(ph, overlap) + _dot(plo, overlap)
        score = jnp.where(causal_blk, imp + bonus, NEG_INF)
        score = jnp.where(blk < n_slc, score, -jnp.inf)

        def pick_one(_, carry):
            work, sel = carry
            m = jnp.max(work, axis=-1, keepdims=True)
            idx = jnp.min(jnp.where(work == m, blk_f, 1e9), axis=-1, keepdims=True)
            hit = blk_f == idx
            return jnp.where(hit, -jnp.inf, work), jnp.where(hit, 1.0, sel)

        _, sel = lax.fori_loop(0, n_pick, pick_one, (score, jnp.zeros((tq, nbp), F32)))
        sel_ref[0, :, k * nbp:(k + 1) * nbp] = sel


def _nbp(n_slc):
    return 64 if n_slc <= 64 else -(-n_slc // LANES) * LANES


def cmp_attn_select(q, gate, kc, vc, tq, pos0, n_slc):
    b, l, dq = q.shape
    ncp = kc.shape[1]
    nbp = _nbp(n_slc)
    assert l % tq == 0
    tok = lambda w: pl.BlockSpec((1, tq, w), lambda i, j: (i, j, 0))
    whole = pl.BlockSpec((1, ncp, NSA_KV_COLS), lambda i, j: (i, 0, 0))
    return pl.pallas_call(
        functools.partial(_cmp_attn_body, tq=tq, pos0=pos0, n_slc=n_slc, nbp=nbp),
        grid=(b, l // tq), in_specs=[tok(dq), tok(LANES), whole, whole],
        out_specs=[tok(dq), tok(NSA_KV_HEADS * nbp)],
        out_shape=[jax.ShapeDtypeStruct((b, l, dq), F32), jax.ShapeDtypeStruct((b, l, NSA_KV_HEADS * nbp), F32)],
        compiler_params=_cparams(("parallel", "parallel")), name="cmp_attn_select",
    )(q, gate, kc, vc)


def _nsa_attn_body(q_ref, gate_ref, kv_ref, *rest, mode, tq, tk, qpos0, kpos0, gate_off, nbp):
    if mode == "slc":
        sel_ref, o_ref, qs_ref, m_ref, l_ref, acc_ref = rest
    else:
        o_ref, qs_ref, m_ref, l_ref, acc_ref = rest
    qi = pl.program_id(1)
    hd = NSA_HEAD_DIM
    nk = kv_ref.shape[1] // tk
    qmin = qpos0 + qi * tq
    t_col = qmin + _iota2((tq, 1), 0)
    for k in range(NSA_KV_HEADS):
        qs_ref[k] = jnp.concatenate(
            [q_ref[0, :, (k * NSA_GROUP + g) * hd:(k * NSA_GROUP + g + 1) * hd] * (hd ** -0.5)
             for g in range(NSA_GROUP)], axis=0)
    m_ref[...] = jnp.full(m_ref.shape, NEG_INF, F32)
    l_ref[...] = jnp.zeros(l_ref.shape, F32)
    acc_ref[...] = jnp.zeros(acc_ref.shape, F32)
    hi = jnp.minimum(nk - 1, (qmin + tq - 1 - kpos0) // tk)
    if mode == "slc":
        lo = 0
    else:
        lo = jnp.maximum(qmin - (WINDOW - 1) - kpos0, 0) // tk

    def step(ki, carry):
        k0 = pl.multiple_of(ki * tk, tk)
        kv = kv_ref[0, pl.ds(k0, tk), :]
        kpos = kpos0 + k0 + _iota2((1, tk), 1)
        if mode == "slc":
            expand = jnp.where(_iota2((nbp, tk), 0) == lax.shift_right_logical(k0 + _iota2((nbp, tk), 1), 6),
                               1.0, 0.0).astype(BF16)
            base = kpos <= t_col
        else:
            dist = t_col - kpos
            base = (dist >= 0) & (dist < WINDOW)
        for k in range(NSA_KV_HEADS):
            kk = kv[:, k * hd:(k + 1) * hd].astype(BF16)
            vv = kv[:, NSA_KV_COLS + k * hd:NSA_KV_COLS + (k + 1) * hd].astype(BF16)
            s = _dot_nt(qs_ref[k].astype(BF16), kk)
            if mode == "slc":
                mask = base & (_dot(sel_ref[0, :, k * nbp:(k + 1) * nbp].astype(BF16), expand) > 0.5)
            else:
                mask = base
            for g in range(NSA_GROUP):
                r = k * NSA_GROUP + g
                s_g = jnp.where(mask, s[g * tq:(g + 1) * tq], NEG_INF)
                m_old = m_ref[r]
                m_new = jnp.maximum(m_old, jnp.max(s_g, axis=-1, keepdims=True))
                alpha = jnp.exp(m_old - m_new)
                p = jnp.where(mask, jnp.exp(s_g - m_new), 0.0)
                l_ref[r] = alpha * l_ref[r] + jnp.sum(p, axis=-1, keepdims=True)
                acc_ref[r] = alpha * acc_ref[r] + _dot(p.astype(BF16), vv)
                m_ref[r] = m_new
        return carry

    lax.fori_loop(lo, hi + 1, step, 0)
    gate = jax.nn.sigmoid(gate_ref[0])
    for r in range(NSA_HEADS):
        l = l_ref[r]
        o = acc_ref[r] / jnp.where(l > 0.0, l, 1.0)
        o_ref[0, :, r * hd:(r + 1) * hd] = o * gate[:, gate_off + r:gate_off + r + 1]


def nsa_attn(mode, q, gate, kv, sel, tq, tk, qpos0, kpos0):
    b, l, dq = q.shape
    nkeys = kv.shape[1]
    assert l % tq == 0 and nkeys % tk == 0
    tok = lambda w: pl.BlockSpec((1, tq, w), lambda i, j: (i, j, 0))
    in_specs = [tok(dq), tok(LANES), pl.BlockSpec((1, nkeys, kv.shape[2]), lambda i, j: (i, 0, 0))]
    args = [q, gate, kv]
    nbp = 0
    if mode == "slc":
        nbp = sel.shape[2] // NSA_KV_HEADS
        in_specs.append(tok(sel.shape[2]))
        args.append(sel)
    return pl.pallas_call(
        functools.partial(_nsa_attn_body, mode=mode, tq=tq, tk=tk, qpos0=qpos0, kpos0=kpos0,
                          gate_off=NSA_HEADS * (1 if mode == "slc" else 2), nbp=nbp),
        grid=(b, l // tq), in_specs=in_specs, out_specs=tok(dq),
        out_shape=jax.ShapeDtypeStruct((b, l, dq), F32),
        scratch_shapes=[pltpu.VMEM((NSA_KV_HEADS, NSA_GROUP * tq, NSA_HEAD_DIM), F32),
                        pltpu.VMEM((NSA_HEADS, tq, 1), F32), pltpu.VMEM((NSA_HEADS, tq, 1), F32),
                        pltpu.VMEM((NSA_HEADS, tq, NSA_HEAD_DIM), F32)],
        compiler_params=_cparams(("parallel", "parallel")), name="nsa_attn_" + mode,
    )(*args)


def _slc_paged_body(pt_ref, q_ref, gate_ref, selrows_ref, kvnew_ref, *rest, n_pg, nbp, tq, l_new, blk_new):
    page_refs = rest[:n_pg]
    o_ref, qbd_ref, m_ref, l_ref, acc_ref = rest[n_pg:]
    gi = pl.program_id(1)
    hd = NSA_HEAD_DIM
    rows = NSA_HEADS * tq

    @pl.when(gi == 0)
    def _():
        qbd_ref[...] = jnp.zeros(qbd_ref.shape, F32)
        for h in range(NSA_HEADS):
            k = h // NSA_GROUP
            qbd_ref[h * tq:(h + 1) * tq, k * hd:(k + 1) * hd] = q_ref[0, :, h * hd:(h + 1) * hd] * (hd ** -0.5)
        m_ref[...] = jnp.full(m_ref.shape, NEG_INF, F32)
        l_ref[...] = jnp.zeros(l_ref.shape, F32)
        acc_ref[...] = jnp.zeros(acc_ref.shape, F32)

    sel = selrows_ref[0].astype(BF16)
    qbd = qbd_ref[...].astype(BF16)

    def update(s, mask, vv):
        s = jnp.where(mask, s, NEG_INF)
        m_old = m_ref[...]
        m_new = jnp.maximum(m_old, jnp.max(s, axis=-1, keepdims=True))
        alpha = jnp.exp(m_old - m_new)
        p = jnp.where(mask, jnp.exp(s - m_new), 0.0)
        l_ref[...] = alpha * l_ref[...] + jnp.sum(p, axis=-1, keepdims=True)
        acc_ref[...] = alpha * acc_ref[...] + _dot(p.astype(BF16), vv)
        m_ref[...] = m_new

    page = page_refs[0].shape[1]
    for i in range(n_pg):
        pg = gi * n_pg + i
        kv = page_refs[i][0]
        tok_blk = lax.shift_right_logical(pg * page + _iota2((nbp, page), 1), 6)
        expand = jnp.where(_iota2((nbp, page), 0) == tok_blk, 1.0, 0.0).astype(BF16)
        mask = _dot(sel, expand) > 0.5
        update(_dot_nt(qbd, kv[:, :NSA_KV_COLS].astype(BF16)), mask, kv[:, NSA_KV_COLS:].astype(BF16))

    @pl.when(gi == pl.num_programs(1) - 1)
    def _():
        kvn = kvnew_ref[0]
        nr = kvn.shape[0]
        j = _iota2((rows, nr), 1)
        iq = jnp.bitwise_and(_iota2((rows, nr), 0), tq - 1)
        picked = jnp.sum(jnp.where(_iota2((rows, nbp), 1) == blk_new, selrows_ref[0], 0.0), axis=-1, keepdims=True)
        mask = (picked > 0.5) & (j <= iq) & (j < l_new)
        update(_dot_nt(qbd, kvn[:, :NSA_KV_COLS].astype(BF16)), mask, kvn[:, NSA_KV_COLS:].astype(BF16))
        gate = jax.nn.sigmoid(gate_ref[0])
        l = l_ref[...]
        o = acc_ref[...] / jnp.where(l > 0.0, l, 1.0)
        for h in range(NSA_HEADS):
            k = h // NSA_GROUP
            o_ref[0, :, h * hd:(h + 1) * hd] = (o[h * tq:(h + 1) * tq, k * hd:(k + 1) * hd]
                                                * gate[:, NSA_HEADS + h:NSA_HEADS + h + 1])


def slc_paged(q, gate, selrows, kv_new, cache, page_table, n_pg, l_new, p0):
    b, tq, dq = q.shape
    npool, page, w = cache.shape
    n_pages = page_table.shape[1]
    nbp = selrows.shape[2]
    rows = NSA_HEADS * tq
    assert n_pages % n_pg == 0 and p0 % SLC_BLOCK == 0 and l_new <= SLC_BLOCK and p0 == n_pages * page
    assert tq & (tq - 1) == 0
    per_b = lambda shape: pl.BlockSpec((1,) + shape, lambda bi, gi, pt: (bi, 0, 0))
    page_spec = lambda i: pl.BlockSpec((1, page, w), lambda bi, gi, pt: (pt[bi, gi * n_pg + i], 0, 0))
    grid_spec = pltpu.PrefetchScalarGridSpec(
        num_scalar_prefetch=1, grid=(b, n_pages // n_pg),
        in_specs=[per_b((tq, dq)), per_b((tq, LANES)), per_b((rows, nbp)), per_b((kv_new.shape[1], w))]
        + [page_spec(i) for i in range(n_pg)],
        out_specs=per_b((tq, dq)),
        scratch_shapes=[pltpu.VMEM((rows, NSA_KV_COLS), F32), pltpu.VMEM((rows, 1), F32),
                        pltpu.VMEM((rows, 1), F32), pltpu.VMEM((rows, NSA_KV_COLS), F32)])
    return pl.pallas_call(
        functools.partial(_slc_paged_body, n_pg=n_pg, nbp=nbp, tq=tq, l_new=l_new, blk_new=p0 // SLC_BLOCK),
        grid_spec=grid_spec, out_shape=jax.ShapeDtypeStruct((b, tq, dq), F32),
        compiler_params=_cparams(("parallel", "arbitrary")), name="slc_paged",
    )(page_table, q, gate, selrows, kv_new, *([cache] * n_pg))


def _cmp_attn_t_body(qt_ref, gt_ref, kc_ref, vct_ref, oct_ref, selt_ref, *, tq, pos0, n_slc, nbp):
    qi = pl.program_id(1)
    ncp = kc_ref.shape[1]
    hd = NSA_HEAD_DIM
    t_row = pos0 + qi * tq + _iota2((1, tq), 1)
    cmask = (_iota2((ncp, 1), 0) * CMP_STRIDE + (CMP_LEN - 1)) <= t_row
    kc = kc_ref[0].astype(BF16)
    vct = vct_ref[0].astype(BF16)
    gate = jax.nn.sigmoid(gt_ref[0, 0:NSA_HEADS, :])
    ss = _iota2((nbp, ncp), 0) * SLC_BLOCK
    nn = _iota2((nbp, ncp), 1) * CMP_STRIDE
    overlap_t = jnp.where((nn < ss + SLC_BLOCK) & (nn + CMP_LEN > ss), 1.0, 0.0).astype(BF16)
    blk = _iota2((nbp, 1), 0)
    blk_f = blk.astype(F32)
    cur = lax.shift_right_logical(t_row, 6)
    forced = (blk == 0) | (blk == cur) | (blk == cur - 1)
    bonus = jnp.where(forced, FORCE_BONUS, 0.0)
    causal_blk = blk * SLC_BLOCK <= t_row
    n_pick = min(SLC_TOPN, n_slc)

    for k in range(NSA_KV_HEADS):
        kc_k = kc[:, k * hd:(k + 1) * hd]
        vct_k = vct[k * hd:(k + 1) * hd, :]
        hs = [k * NSA_GROUP + g for g in range(NSA_GROUP)]
        q_l = [(qt_ref[0, h * hd:(h + 1) * hd, :] * (hd ** -0.5)).astype(BF16) for h in hs]
        s_l = [jnp.where(cmask, _dot(kc_k, q_h), NEG_INF) for q_h in q_l]
        e_l = [jnp.where(cmask, jnp.exp(s - jnp.max(s, axis=0, keepdims=True)), 0.0) for s in s_l]
        den_l = [jnp.sum(e, axis=0, keepdims=True) for e in e_l]
        p_l = [e * (1.0 / jnp.where(den > 0.0, den, 1.0)) for e, den in zip(e_l, den_l)]
        o_l = [_dot(vct_k, p.astype(BF16)) for p in p_l]
        for h, o in zip(hs, o_l):
            oct_ref[0, h * hd:(h + 1) * hd, :] = o * gate[h:h + 1, :]
        psum = (p_l[0] + p_l[1]) + (p_l[2] + p_l[3])
        ph, plo = _split2(psum)
        imp = _dot(overlap_t, ph) + _dot(overlap_t, plo)
        score = jnp.where(causal_blk, imp + bonus, NEG_INF)
        score = jnp.where(blk < n_slc, score, -jnp.inf)

        def pick_one(_, carry):
            work, sel = carry
            m = jnp.max(work, axis=0, keepdims=True)
            idx = jnp.min(jnp.where(work == m, blk_f, 1e9), axis=0, keepdims=True)
            hit = blk_f == idx
            return jnp.where(hit, -jnp.inf, work), jnp.where(hit, 1.0, sel)

        _, sel = lax.fori_loop(0, n_pick, pick_one, (score, jnp.zeros((nbp, tq), F32)))
        selt_ref[0, k * nbp:(k + 1) * nbp, :] = sel


def cmp_attn_select_t(qt, gate_t, kc, vct, b, tq, pos0, n_slc):
    nt, dq, _ = qt.shape
    nq = nt // b
    ncp = kc.shape[1]
    nbp = _nbp(n_slc)
    tile = lambda rows: pl.BlockSpec((1, rows, tq), lambda i, j: (i * nq + j, 0, 0))
    return pl.pallas_call(
        functools.partial(_cmp_attn_t_body, tq=tq, pos0=pos0, n_slc=n_slc, nbp=nbp),
        grid=(b, nq),
        in_specs=[tile(dq), tile(LANES), pl.BlockSpec((1, ncp, NSA_KV_COLS), lambda i, j: (i, 0, 0)),
                  pl.BlockSpec((1, NSA_KV_COLS, ncp), lambda i, j: (i, 0, 0))],
        out_specs=[tile(dq), tile(NSA_KV_HEADS * nbp)],
        out_shape=[jax.ShapeDtypeStruct((nt, dq, tq), F32), jax.ShapeDtypeStruct((nt, NSA_KV_HEADS * nbp, tq), F32)],
        compiler_params=_cparams(("parallel", "parallel")), name="cmp_attn_select_t",
    )(qt, gate_t, kc, vct)


def _nsa_attn_t_body(qt_ref, gt_ref, kv_ref, vt_ref, *rest, mode, tq, tk, gate_off, nbp):
    if mode == "slc":
        selt_ref, o_ref, qs_ref, m_ref, l_ref, acc_ref = rest
    else:
        o_ref, qs_ref, m_ref, l_ref, acc_ref = rest
    qi = pl.program_id(1)
    hd = NSA_HEAD_DIM
    nk = kv_ref.shape[1] // tk
    qmin = qi * tq
    t_row = qmin + _iota2((1, tq), 1)
    for r in range(NSA_HEADS):
        qs_ref[r] = (qt_ref[0, r * hd:(r + 1) * hd, :] * (hd ** -0.5)).astype(BF16)
    m_ref[...] = jnp.full(m_ref.shape, NEG_INF, F32)
    l_ref[...] = jnp.zeros(l_ref.shape, F32)
    acc_ref[...] = jnp.zeros(acc_ref.shape, F32)
    hi = jnp.minimum(nk - 1, (qmin + tq - 1) // tk)
    lo = 0 if mode == "slc" else jnp.maximum(qmin - (WINDOW - 1), 0) // tk

    if mode == "slc":
        selb = [((1.0 - selt_ref[0, k * nbp:(k + 1) * nbp, :]) * MASKED).astype(BF16) for k in range(NSA_KV_HEADS)]

    def step(ki, carry):
        k0 = pl.multiple_of(ki * tk, tk)
        krows = kv_ref[0, pl.ds(k0, tk), 0:NSA_KV_COLS].astype(BF16)
        vt = vt_ref[ki].astype(BF16)
        kpos = k0 + _iota2((tk, 1), 0)
        if mode == "slc":
            expand_t = jnp.where(_iota2((tk, nbp), 1) == lax.shift_right_logical(k0 + _iota2((tk, nbp), 0), 6),
                                 1.0, 0.0).astype(BF16)
            cbias = jnp.where(kpos <= t_row, 0.0, MASKED)
        else:
            dist = t_row - kpos
            cbias = jnp.where((dist >= 0) & (dist < WINDOW), 0.0, MASKED)
        for k in range(NSA_KV_HEADS):
            k_k = krows[:, k * hd:(k + 1) * hd]
            vt_k = vt[k * hd:(k + 1) * hd, :]
            bias = cbias + _dot(expand_t, selb[k]) if mode == "slc" else cbias
            rs = [k * NSA_GROUP + g for g in range(NSA_GROUP)]
            s_l = [_dot(k_k, qs_ref[r]) + bias for r in rs]
            m_old = [m_ref[r] for r in rs]
            m_new = [jnp.maximum(mo, jnp.max(s, axis=0, keepdims=True)) for mo, s in zip(m_old, s_l)]
            p_l = [jnp.exp(s - mn) for s, mn in zip(s_l, m_new)]
            pv_l = [_dot(vt_k, p.astype(BF16)) for p in p_l]
            for r, mo, mn, p, pv in zip(rs, m_old, m_new, p_l, pv_l):
                alpha = jnp.exp(mo - mn)
                l_ref[r] = alpha * l_ref[r] + jnp.sum(p, axis=0, keepdims=True)
                acc_ref[r] = alpha * acc_ref[r] + pv
                m_ref[r] = mn
        return carry

    lax.fori_loop(lo, hi + 1, step, 0)
    gate = jax.nn.sigmoid(gt_ref[0, gate_off:gate_off + NSA_HEADS, :])
    for r in range(NSA_HEADS):
        l = l_ref[r]
        o_ref[0, r * hd:(r + 1) * hd, :] = acc_ref[r] * (gate[r:r + 1, :] / jnp.where(l > 0.0, l, 1.0))


def nsa_attn_t(mode, qt, gate_t, kv, vt, selt, b, tq, tk):
    nt, dq, _ = qt.shape
    nq = nt // b
    l = kv.shape[1]
    nk = l // tk
    assert l % tk == 0 and vt.shape == (b * nk, NSA_KV_COLS, tk)
    tile = lambda rows: pl.BlockSpec((1, rows, tq), lambda i, j: (i * nq + j, 0, 0))
    in_specs = [tile(dq), tile(LANES), pl.BlockSpec((1, l, kv.shape[2]), lambda i, j: (i, 0, 0)),
                pl.BlockSpec((nk, NSA_KV_COLS, tk), lambda i, j: (i, 0, 0))]
    args = [qt, gate_t, kv, vt]
    nbp = 0
    if mode == "slc":
        nbp = selt.shape[1] // NSA_KV_HEADS
        in_specs.append(tile(selt.shape[1]))
        args.append(selt)
    return pl.pallas_call(
        functools.partial(_nsa_attn_t_body, mode=mode, tq=tq, tk=tk,
                          gate_off=NSA_HEADS * (1 if mode == "slc" else 2), nbp=nbp),
        grid=(b, nq), in_specs=in_specs, out_specs=tile(dq),
        out_shape=jax.ShapeDtypeStruct((nt, dq, tq), F32),
        scratch_shapes=[pltpu.VMEM((NSA_HEADS, NSA_HEAD_DIM, tq), BF16),
                        pltpu.VMEM((NSA_HEADS, 1, tq), F32), pltpu.VMEM((NSA_HEADS, 1, tq), F32),
                        pltpu.VMEM((NSA_HEADS, NSA_HEAD_DIM, tq), F32)],
        compiler_params=_cparams(("parallel", "parallel")), name="nsa_attn_t_" + mode,
    )(*args)


def _prep_nsa_weights(w_in_nsa):
    w = w_in_nsa.astype(BF16)
    dq = NSA_HEADS * NSA_HEAD_DIM
    kvw = 2 * NSA_KV_COLS
    gate = jnp.pad(w[:, dq + 3 * kvw:], ((0, 0), (0, LANES - 3 * NSA_HEADS)))
    return [w[:, :dq], w[:, dq:dq + kvw], w[:, dq + kvw:dq + 2 * kvw], w[:, dq + 2 * kvw:dq + 3 * kvw], gate]


def _kv5(a):
    return a.reshape(a.shape[0], a.shape[1], 2, NSA_KV_HEADS, NSA_HEAD_DIM)


def nsa_prompt(x, p, tm):
    b, l, d = x.shape
    assert l % tm == 0
    wq, wc, ws, ww, wg = p["w_nsa"]
    v_rows = slice(NSA_KV_COLS, 2 * NSA_KV_COLS)
    wts = [wq.T, ws.T[v_rows], ww.T[v_rows], wg.T]
    kvc, kvs, kvw, qt, vst, vwt, gt = norm_proj(x.reshape(b * l, d), p["norm_mix"][1], [wc, ws, ww], tm, wts)
    r3 = lambda a: a.reshape(b, l, a.shape[-1])
    kvc, kvs, kvw = r3(kvc), r3(kvs), r3(kvw)
    lt = nsa_lt_dense(kvc, p["lt_wk"], p["lt_wv"], 256)
    kc, _, vct = cmp_kv(lt, p)
    n_slc = -(-l // SLC_BLOCK)
    o_c, selt = cmp_attn_select_t(qt, gt, kc, vct, b, tm, 0, n_slc)
    o_s = nsa_attn_t("slc", qt, gt, kvs, vst, selt, b, tm, tm)
    o_w = nsa_attn_t("win", qt, gt, kvw, vwt, None, b, tm, tm)
    keep = min(WINDOW, l)
    return [o_c, o_s, o_w], _kv5(kvc), _kv5(kvs), _kv5(kvw[:, l - keep:])


def nsa_sample(x, cache_cmp, cache_slc, past_win, page_table, p, n_pg):
    b, lq, d = x.shape
    tq = 8
    assert lq <= tq
    npool, page = cache_cmp.shape[:2]
    w = 2 * NSA_KV_COLS
    p0 = page_table.shape[1] * page
    xp = _pad_time(x, tq).reshape(b * tq, d)
    q, kvc, kvs, kvw, gate = norm_proj(xp, p["norm_mix"][1], p["w_nsa"], b * tq)
    r3 = lambda a: a.reshape(b, tq, a.shape[-1])
    q, kvc, kvs, kvw, gate = r3(q), r3(kvc), r3(kvs), r3(kvw), r3(gate)
    lt = nsa_lt_paged(cache_cmp.reshape(npool, page, w), page_table, p["lt_wk"], p["lt_wv"], n_pg)
    kc, vc, _ = cmp_kv(lt, p)
    n_slc = -(-(p0 + lq) // SLC_BLOCK)
    o_c, sel = cmp_attn_select(q, gate, kc, vc, tq, p0, n_slc)
    nbp = sel.shape[2] // NSA_KV_HEADS
    selrows = sel.reshape(b, tq, NSA_KV_HEADS, 1, nbp).transpose(0, 2, 3, 1, 4)
    selrows = jnp.broadcast_to(selrows, (b, NSA_KV_HEADS, NSA_GROUP, tq, nbp)).reshape(b, NSA_HEADS * tq, nbp)
    o_s = slc_paged(q, gate, selrows, kvs, cache_slc.reshape(npool, page, w), page_table, n_pg, lq, p0)
    n_past = past_win.shape[1]
    win_all = jnp.concatenate([past_win.reshape(b, n_past, w), kvw[:, :lq]], axis=1)
    nkeys = -(-(n_past + lq) // 8) * 8
    o_w = nsa_attn("win", q, gate, _pad_time(win_all, nkeys), None, tq, nkeys, p0, p0 - n_past)
    keep = min(WINDOW, n_past + lq)
    flat = lambda a: a[:, :lq].reshape(b * lq, a.shape[-1])
    return ([flat(o_c), flat(o_s), flat(o_w)], _kv5(kvc[:, :lq]), _kv5(kvs[:, :lq]),
            _kv5(win_all[:, n_past + lq - keep:]))


def _trunk(x, ssd_state0, ssd_conv0, gdn_state0, gdn_conv0, nsa_fn, p, tm):
    b, l, d = x.shape
    t = b * l
    tm = min(tm, t)
    y_ssd, y_gdn, ssd_state, ssd_conv, gdn_state, gdn_conv = layer0_mixer(
        x, ssd_state0, ssd_conv0, gdn_state0, gdn_conv0, p, tm)
    x1 = mix_mlp(x.reshape(t, d), [[y_ssd], [y_gdn]], p["w_out_ab"], p["norm_mlp"][0], p["w_up"][0],
                 p["w_down"][0], None, tm)
    branches, cmp_rows, slc_rows, win_rows = nsa_fn(x1.reshape(b, l, d))
    y = mix_mlp(x1, [branches], [p["w_out_nsa"]], p["norm_mlp"][1], p["w_up"][1], p["w_down"][1],
                p["norm_final"], tm, transposed=branches[0].ndim == 3)
    return y.reshape(b, l, d), ssd_state, ssd_conv, gdn_state, gdn_conv, cmp_rows, slc_rows, win_rows


def kernel(x_prompt, x_sample, state_ssd, state_ssd_conv, state_gdn, state_gdn_conv, cache_cmp_kv, cache_slc_kv,
           state_win_kv, page_table, norm_mix, norm_mlp, norm_final, w_in_ab, ssd_conv_w, ssd_conv_b, ssd_dt_bias,
           ssd_a_log, ssd_d, ssd_norm_w, gdn_conv_w, gdn_dt_bias, gdn_a_log, gdn_norm_w, w_out_ab, w_in_nsa,
           cmp_pe_k, cmp_w1_k, cmp_b1_k, cmp_w2_k, cmp_pe_v, cmp_w1_v, cmp_b1_v, cmp_w2_v, w_out_nsa, w_up, w_down):
    wo = w_out_ab.astype(BF16)
    p = dict(
        norm_mix=norm_mix, norm_mlp=norm_mlp, norm_final=norm_final, w_ab=_prep_ab_weights(w_in_ab),
        ssd_conv_w=ssd_conv_w, ssd_conv_b=ssd_conv_b, ssd_dt_bias=ssd_dt_bias, ssd_a_log=ssd_a_log, ssd_d=ssd_d,
        ssd_norm_w=ssd_norm_w, gdn_conv_w=gdn_conv_w, gdn_dt_bias=gdn_dt_bias, gdn_a_log=gdn_a_log,
        gdn_norm_w=gdn_norm_w, w_out_ab=[wo[:SSD_INNER], wo[SSD_INNER:]], w_nsa=_prep_nsa_weights(w_in_nsa),
        lt_wk=_lt_weights(cmp_w1_k), lt_wv=_lt_weights(cmp_w1_v),
        cmp_pe_k=cmp_pe_k, cmp_w1_k=cmp_w1_k, cmp_b1_k=cmp_b1_k, cmp_w2_k=cmp_w2_k,
        cmp_pe_v=cmp_pe_v, cmp_w1_v=cmp_w1_v, cmp_b1_v=cmp_b1_v, cmp_w2_v=cmp_w2_v,
        w_out_nsa=w_out_nsa.astype(BF16), w_up=w_up.astype(BF16), w_down=w_down.astype(BF16))
    bp = x_prompt.shape[0]
    zeros = lambda *s: jnp.zeros(s, F32)
    tm = 256
    out_p = _trunk(x_prompt, zeros(bp, SSD_HEADS, SSD_HEADDIM, SSD_STATE), zeros(bp, CONV_W - 1, SSD_CONV_DIM),
                   zeros(bp, GDN_V_HEADS, GDN_HEAD_DIM, GDN_HEAD_DIM), zeros(bp, CONV_W - 1, GDN_CONV_DIM),
                   lambda x1: nsa_prompt(x1, p, tm), p, tm)
    out_s = _trunk(x_sample, state_ssd, state_ssd_conv, state_gdn, state_gdn_conv,
                   lambda x1: nsa_sample(x1, cache_cmp_kv, cache_slc_kv, state_win_kv, page_table, p, 16), p, tm)
    return tuple(v for pair in zip(out_p, out_s) for v in pair)
```

```python
import functools

import jax
import jax.numpy as jnp
from jax import lax
from jax.experimental import pallas as pl
from jax.experimental.pallas import tpu as pltpu

F32 = jnp.float32
BF16 = jnp.bfloat16

RMS_EPS = 1e-6
NEG_INF = -1e30
MASKED = 2.0 * NEG_INF
FORCE_BONUS = 1e4
LOG2E = 1.4426950408889634
CONV_W = 4
CHUNK = 64
SSD_HEADS, SSD_HEADDIM, SSD_STATE, SSD_GROUPS = 16, 64, 128, 2
SSD_INNER = SSD_HEADS * SSD_HEADDIM
SSD_CONV_DIM = SSD_INNER + 2 * SSD_GROUPS * SSD_STATE
GDN_QK_HEADS, GDN_V_HEADS, GDN_HEAD_DIM = 4, 8, 128
GDN_QK_DIM = GDN_QK_HEADS * GDN_HEAD_DIM
GDN_V_DIM = GDN_V_HEADS * GDN_HEAD_DIM
GDN_CONV_DIM = 2 * GDN_QK_DIM + GDN_V_DIM
NSA_HEADS, NSA_KV_HEADS, NSA_HEAD_DIM = 16, 4, 64
NSA_GROUP = NSA_HEADS // NSA_KV_HEADS
NSA_KV_COLS = NSA_KV_HEADS * NSA_HEAD_DIM
CMP_STRIDE, CMP_LEN, CMP_HIDDEN = 16, 32, 64
SLC_BLOCK, SLC_TOPN, WINDOW = 64, 16, 512
PAGE_SIZE = 128
LANES = 128
VMEM_LIMIT = 56 * 1024 * 1024


def _cparams(sem):
    return pltpu.CompilerParams(dimension_semantics=sem, vmem_limit_bytes=VMEM_LIMIT)


def _dot(a, b):
    return jnp.dot(a, b, preferred_element_type=F32)


def _dot_nt(a, b):
    return lax.dot_general(a, b, (((1,), (1,)), ((), ())), preferred_element_type=F32)


def _dot_tn(a, b):
    return lax.dot_general(a, b, (((0,), (0,)), ((), ())), preferred_element_type=F32)


def _split2(x):
    hi = x.astype(BF16)
    lo = (x - hi.astype(F32)).astype(BF16)
    return hi, lo


def _split3(x):
    hi = x.astype(BF16)
    r = x - hi.astype(F32)
    mid = r.astype(BF16)
    lo = (r - mid.astype(F32)).astype(BF16)
    return hi, mid, lo


def _mm3(a, b):
    ah, al = _split2(a)
    bh, bl = _split2(b)
    return _dot(ah, bh) + (_dot(ah, bl) + _dot(al, bh))


def _rms_unit(x):
    return x * lax.rsqrt(jnp.mean(x * x, axis=-1, keepdims=True) + RMS_EPS)


def _silu(x):
    return x * jax.nn.sigmoid(x)


def _softplus(x):
    return jnp.maximum(x, 0.0) + jnp.log1p(jnp.exp(-jnp.abs(x)))


def _iota2(shape, axis):
    return lax.broadcasted_iota(jnp.int32, shape, axis)


def _norm_proj_body(x_ref, g_ref, *refs, n_out, n_t):
    n_w = n_out + n_t
    w_refs, o_refs = refs[:n_w], refs[n_w:]
    hn = (_rms_unit(x_ref[...]) * g_ref[...]).astype(BF16)
    for w_ref, o_ref in zip(w_refs[:n_out], o_refs[:n_out]):
        n = w_ref.shape[1]
        for c0 in range(0, n, 512):
            c1 = min(n, c0 + 512)
            o_ref[:, c0:c1] = _dot(hn, w_ref[:, c0:c1])
    for w_ref, o_ref in zip(w_refs[n_out:], o_refs[n_out:]):
        n = w_ref.shape[0]
        for c0 in range(0, n, 512):
            c1 = min(n, c0 + 512)
            o_ref[0, c0:c1, :] = _dot_nt(w_ref[c0:c1, :], hn)


def norm_proj(x, g, ws, tm, wts=()):
    t, d = x.shape
    assert t % tm == 0
    in_specs = [pl.BlockSpec((tm, d), lambda i: (i, 0)), pl.BlockSpec((1, d), lambda i: (0, 0))]
    in_specs += [pl.BlockSpec(w.shape, lambda i: (0, 0)) for w in (*ws, *wts)]
    out_specs = [pl.BlockSpec((tm, w.shape[1]), lambda i: (i, 0)) for w in ws]
    out_specs += [pl.BlockSpec((1, w.shape[0], tm), lambda i: (i, 0, 0)) for w in wts]
    out_shape = [jax.ShapeDtypeStruct((t, w.shape[1]), F32) for w in ws]
    out_shape += [jax.ShapeDtypeStruct((t // tm, w.shape[0], tm), F32) for w in wts]
    return pl.pallas_call(
        functools.partial(_norm_proj_body, n_out=len(ws), n_t=len(wts)),
        grid=(t // tm,), in_specs=in_specs, out_specs=out_specs, out_shape=out_shape,
        compiler_params=_cparams(("parallel",)), name="norm_proj",
    )(x, g.reshape(1, d), *ws, *wts)


def _mix_mlp_body(x_ref, *refs, group_sizes, transposed, final, hc):
    n_y, n_g = sum(group_sizes), len(group_sizes)
    y_refs = refs[:n_y]
    wo_refs = refs[n_y:n_y + n_g]
    g_ref, wu_ref, wd_ref = refs[n_y + n_g:n_y + n_g + 3]
    rest = refs[n_y + n_g + 3:]
    gf_ref = rest[0] if final else None
    o_ref = rest[-1]
    x1 = x_ref[...]
    i = 0
    for gs, wo_ref in zip(group_sizes, wo_refs):
        rd = (lambda r: r[0]) if transposed else (lambda r: r[...])
        y = rd(y_refs[i])
        for r in y_refs[i + 1:i + gs]:
            y = y + rd(r)
        i += gs
        x1 = x1 + (_dot_tn if transposed else _dot)(y.astype(BF16), wo_ref[...])
    hn = (_rms_unit(x1) * g_ref[...]).astype(BF16)
    acc = x1
    hidden = wu_ref.shape[1]
    for c0 in range(0, hidden, hc):
        h = jnp.maximum(_dot(hn, wu_ref[:, c0:c0 + hc]), 0.0)
        acc = acc + _dot((h * h).astype(BF16), wd_ref[c0:c0 + hc, :])
    if final:
        acc = _rms_unit(acc) * gf_ref[...]
    o_ref[...] = acc


def mix_mlp(x, y_groups, w_outs, g, w_up, w_down, g_final, tm, hc=None, transposed=False):
    t, d = x.shape
    hc = hc or w_up.shape[1]
    assert t % tm == 0
    final = g_final is not None
    const = lambda i: (0, 0)
    ys = [y for grp in y_groups for y in grp]
    in_specs = [pl.BlockSpec((tm, d), lambda i: (i, 0))]
    if transposed:
        assert all(y.shape[0] == t // tm and y.shape[2] == tm for y in ys)
        in_specs += [pl.BlockSpec((1, y.shape[1], tm), lambda i: (i, 0, 0)) for y in ys]
    else:
        in_specs += [pl.BlockSpec((tm, y.shape[1]), lambda i: (i, 0)) for y in ys]
    in_specs += [pl.BlockSpec(w.shape, const) for w in w_outs]
    in_specs += [pl.BlockSpec((1, d), const), pl.BlockSpec(w_up.shape, const), pl.BlockSpec(w_down.shape, const)]
    args = [x, *ys, *w_outs, g.reshape(1, d), w_up, w_down]
    if final:
        in_specs.append(pl.BlockSpec((1, d), const))
        args.append(g_final.reshape(1, d))
    return pl.pallas_call(
        functools.partial(_mix_mlp_body, group_sizes=tuple(len(grp) for grp in y_groups), transposed=transposed,
                          final=final, hc=hc),
        grid=(t // tm,), in_specs=in_specs, out_specs=pl.BlockSpec((tm, d), lambda i: (i, 0)),
        out_shape=jax.ShapeDtypeStruct((t, d), F32),
        compiler_params=_cparams(("parallel",)), name="mix_mlp",
    )(*args)


def _causal_conv(x, xp_ref, w_ref, first, conv0_ref):
    c = x.shape[0]

    @pl.when(first)
    def _():
        xp_ref[5:8, :] = conv0_ref[0]

    xp_ref[8:8 + c, :] = x
    y = xp_ref[5:5 + c, :] * w_ref[0:1, :]
    for k in range(1, CONV_W):
        y = y + xp_ref[5 + k:5 + k + c, :] * w_ref[k:k + 1, :]
    xp_ref[5:8, :] = x[c - 3:c, :]
    return y


def _cumsum_pair(col, row):
    c = col.shape[0]
    ri, ci = _iota2((c, c), 0), _iota2((c, c), 1)
    tril = jnp.where(ri >= ci, 1.0, 0.0).astype(BF16)
    triu = jnp.where(ri <= ci, 1.0, 0.0).astype(BF16)
    ch, cm, cl = _split3(col)
    rh, rm, rl = _split3(row)
    cs = _dot(tril, ch) + (_dot(tril, cm) + _dot(tril, cl))
    cst = _dot(rh, triu) + (_dot(rm, triu) + _dot(rl, triu))
    return cs, cst


def _ssd_body(xbc_ref, z_ref, sm_ref, smt_ref, st0_ref, cv0_ref, cw_ref, cb_ref, dtb_r_ref, dtb_c_ref,
              al_r_ref, al_c_ref, dfull_ref, nw_ref, y_ref, st_ref, xp_ref, ybuf_ref, *, l_valid):
    cidx = pl.program_id(1)
    c = CHUNK
    n, p = SSD_STATE, SSD_HEADDIM

    @pl.when(cidx == 0)
    def _():
        st_ref[0] = st0_ref[0]

    conv = _causal_conv(xbc_ref[0], xp_ref, cw_ref, cidx == 0, cv0_ref) + cb_ref[...]
    act = _silu(conv)
    xs = act[:, :SSD_INNER]
    bm = act[:, SSD_INNER:SSD_INNER + SSD_GROUPS * n]
    cm = act[:, SSD_INNER + SSD_GROUPS * n:]

    valid_c = (cidx * c + _iota2((c, 1), 0)) < l_valid
    valid_r = (cidx * c + _iota2((1, c), 1)) < l_valid
    dt = jnp.where(valid_c, _softplus(sm_ref[0][:, 0:SSD_HEADS] + dtb_r_ref[...]), 0.0)
    dtt = jnp.where(valid_r, _softplus(smt_ref[0, 0][0:SSD_HEADS, :] + dtb_c_ref[...]), 0.0)
    cs, cst = _cumsum_pair(dt * (-jnp.exp(al_r_ref[...])), dtt * (-jnp.exp(al_c_ref[...])))
    cs_last = cs[c - 1:c, :]
    ecs = jnp.exp(cs)
    to_end = jnp.exp(cs_last - cs) * dt
    edec = jnp.exp(cs_last)
    causal = _iota2((c, c), 0) >= _iota2((c, c), 1)

    heads_per_group = SSD_HEADS // SSD_GROUPS
    heads = range(SSD_HEADS)
    bm_l = [bm[:, g * n:(g + 1) * n].astype(BF16) for g in range(SSD_GROUPS)]
    cm_l = [cm[:, g * n:(g + 1) * n].astype(BF16) for g in range(SSD_GROUPS)]
    cb_l = [_dot_nt(c_g, b_g) for c_g, b_g in zip(cm_l, bm_l)]
    grp = lambda h: h // heads_per_group
    x_l = [xs[:, h * p:(h + 1) * p] for h in heads]
    s_l = [st_ref[0, h] for h in heads]
    sc_l = []
    for h in heads:
        seg = cs[:, h:h + 1] - cst[h:h + 1, :]
        decay = jnp.exp(jnp.where(causal, seg, NEG_INF))
        sc_l.append((cb_l[grp(h)] * decay * dtt[h:h + 1, :]).astype(BF16))
    yd_l = [_dot(sc_l[h], x_l[h].astype(BF16)) for h in heads]
    yo_l = [_dot_nt(cm_l[grp(h)], s_l[h].astype(BF16)) * ecs[:, h:h + 1] for h in heads]
    for h in heads:
        ybuf_ref[:, h * p:(h + 1) * p] = yd_l[h] + yo_l[h]
    for h in heads:
        xw = (x_l[h] * to_end[:, h:h + 1]).astype(BF16)
        st_ref[0, h] = s_l[h] * edec[:, h:h + 1] + _dot_tn(xw, bm_l[grp(h)])

    y = ybuf_ref[...] + dfull_ref[...] * xs
    yz = y * _silu(z_ref[0])
    gw = SSD_INNER // SSD_GROUPS
    for g in range(SSD_GROUPS):
        seg = yz[:, g * gw:(g + 1) * gw]
        y_ref[0, :, g * gw:(g + 1) * gw] = _rms_unit(seg) * nw_ref[:, g * gw:(g + 1) * gw]


def ssd_mixer(xbc, z, small, small_t, state0, conv0, conv_w, conv_b, dt_bias, a_log, d_skip, norm_w, l_valid):
    b, lp, _ = xbc.shape
    nc = lp // CHUNK
    h = SSD_HEADS
    d_full = jnp.repeat(d_skip, SSD_HEADDIM).reshape(1, SSD_INNER)
    const2 = lambda i, j: (0, 0)
    tok = lambda w: pl.BlockSpec((1, CHUNK, w), lambda i, j: (i, j, 0))
    in_specs = [
        tok(SSD_CONV_DIM), tok(SSD_INNER), tok(LANES),
        pl.BlockSpec((1, 1, 32, CHUNK), lambda i, j: (i, j, 0, 0)),
        pl.BlockSpec((1, h, SSD_HEADDIM, SSD_STATE), lambda i, j: (i, 0, 0, 0)),
        pl.BlockSpec((1, CONV_W - 1, SSD_CONV_DIM), lambda i, j: (i, 0, 0)),
        pl.BlockSpec((CONV_W, SSD_CONV_DIM), const2), pl.BlockSpec((1, SSD_CONV_DIM), const2),
        pl.BlockSpec((1, h), const2), pl.BlockSpec((h, 1), const2),
        pl.BlockSpec((1, h), const2), pl.BlockSpec((h, 1), const2),
        pl.BlockSpec((1, SSD_INNER), const2), pl.BlockSpec((1, SSD_INNER), const2),
    ]
    out_specs = [tok(SSD_INNER), pl.BlockSpec((1, h, SSD_HEADDIM, SSD_STATE), lambda i, j: (i, 0, 0, 0))]
    out_shape = [jax.ShapeDtypeStruct((b, lp, SSD_INNER), F32),
                 jax.ShapeDtypeStruct((b, h, SSD_HEADDIM, SSD_STATE), F32)]
    return pl.pallas_call(
        functools.partial(_ssd_body, l_valid=l_valid),
        grid=(b, nc), in_specs=in_specs, out_specs=out_specs, out_shape=out_shape,
        scratch_shapes=[pltpu.VMEM((CHUNK + 8, SSD_CONV_DIM), F32), pltpu.VMEM((CHUNK, SSD_INNER), F32)],
        compiler_params=_cparams(("parallel", "arbitrary")), name="ssd_mixer",
    )(xbc, z, small, small_t, state0, conv0, conv_w, conv_b.reshape(1, -1),
      dt_bias.reshape(1, h), dt_bias.reshape(h, 1), a_log.reshape(1, h), a_log.reshape(h, 1),
      d_full, norm_w.reshape(1, SSD_INNER))


def _unit_lower_inverses(mats):
    c = mats[0].shape[0]
    ri, ci = _iota2((c, c), 0), _iota2((c, c), 1)
    eye = jnp.where(ri == ci, 1.0, 0.0)
    same_blk = lax.shift_right_logical(ri, 4) == lax.shift_right_logical(ci, 4)
    d = [jnp.where(same_blk, a, 0.0) for a in mats]
    nn = [a - x for a, x in zip(mats, d)]
    d2 = [_mm3(x, x) for x in d]
    d4 = [_mm3(x, x) for x in d2]
    pinv = [eye - x for x in d]
    pinv = [x + _mm3(x, y) for x, y in zip(pinv, d2)]
    d8 = [_mm3(x, x) for x in d4]
    pinv = [x + _mm3(x, y) for x, y in zip(pinv, d4)]
    pinv = [x + _mm3(x, y) for x, y in zip(pinv, d8)]
    e = [_mm3(x, y) for x, y in zip(pinv, nn)]
    e2 = [_mm3(x, x) for x in e]
    q = [eye - x for x in e]
    q = [x + _mm3(x, y) for x, y in zip(q, e2)]
    return [_mm3(x, y) for x, y in zip(q, pinv)]


def _l2norm(x):
    return x * lax.rsqrt(jnp.sum(x * x, axis=-1, keepdims=True) + 1e-6)


def _gdn_body(qkv_ref, z_ref, sm_ref, smt_ref, st0_ref, cv0_ref, cw_ref, dtb_r_ref, dtb_c_ref,
              al_r_ref, al_c_ref, nw_ref, o_ref, st_ref, xp_ref, *, l_valid):
    cidx = pl.program_id(1)
    c = CHUNK
    hd = GDN_HEAD_DIM
    nh = GDN_V_HEADS

    @pl.when(cidx == 0)
    def _():
        st_ref[0] = st0_ref[0]

    act = _silu(_causal_conv(qkv_ref[0], xp_ref, cw_ref, cidx == 0, cv0_ref))
    valid_c = (cidx * c + _iota2((c, 1), 0)) < l_valid
    valid_r = (cidx * c + _iota2((1, c), 1)) < l_valid
    sm = sm_ref[0]
    smt = smt_ref[0, 0]
    beta = jnp.where(valid_c, jax.nn.sigmoid(sm[:, 16:16 + nh]), 0.0)
    g_c = jnp.where(valid_c, -jnp.exp(al_r_ref[...]) * _softplus(sm[:, 24:24 + nh] + dtb_r_ref[...]), 0.0)
    g_r = jnp.where(valid_r, -jnp.exp(al_c_ref[...]) * _softplus(smt[24:24 + nh, :] + dtb_c_ref[...]), 0.0)
    cs, cst = _cumsum_pair(g_c, g_r)
    cs_last = cs[c - 1:c, :]
    ecs = jnp.exp(cs)
    e_end = jnp.exp(cs_last - cs)
    etot = jnp.exp(cs_last)
    ri, ci = _iota2((c, c), 0), _iota2((c, c), 1)
    causal = ri >= ci
    strict = ri > ci
    rep = GDN_V_HEADS // GDN_QK_HEADS

    eye = jnp.where(ri == ci, 1.0, 0.0)
    q_l, k_l, qk_l, decay_l, a_l = [], [], [], [], []
    for hq in range(GDN_QK_HEADS):
        q_h = _l2norm(act[:, hq * hd:(hq + 1) * hd]) * (hd ** -0.5)
        k_h = _l2norm(act[:, GDN_QK_DIM + hq * hd:GDN_QK_DIM + (hq + 1) * hd])
        k16 = k_h.astype(BF16)
        kk = _dot_nt(k16, k16)
        qk0 = _dot_nt(q_h.astype(BF16), k16)
        q_l.append(q_h)
        k_l.append(k_h)
        for r in range(rep):
            h = hq * rep + r
            seg = cs[:, h:h + 1] - cst[h:h + 1, :]
            decay = jnp.exp(jnp.where(causal, seg, NEG_INF))
            decay_l.append(decay)
            qk_l.append((qk0 * decay).astype(BF16))
            a_l.append(jnp.where(strict, kk * beta[:, h:h + 1] * decay, 0.0))
    tinv_l = _unit_lower_inverses(a_l)
    heads = range(nh)
    rhs_l = []
    for h in heads:
        b_h = beta[:, h:h + 1]
        v_h = act[:, 2 * GDN_QK_DIM + h * hd:2 * GDN_QK_DIM + (h + 1) * hd]
        rhs_l.append(jnp.concatenate([v_h * b_h, k_l[h // rep] * (b_h * ecs[:, h:h + 1])], axis=1))
    sol_l = [rhs_l[h] + _dot((tinv_l[h] - eye).astype(BF16), rhs_l[h].astype(BF16)) for h in heads]
    s_l = [st_ref[0, h] for h in heads]
    s16_l = [s.astype(BF16) for s in s_l]
    o1_l = [_dot((q_l[h // rep] * ecs[:, h:h + 1]).astype(BF16), s16_l[h]) for h in heads]
    v16_l = [(sol_l[h][:, :hd] - _dot(sol_l[h][:, hd:].astype(BF16), s16_l[h])).astype(BF16) for h in heads]
    o_l = [o1_l[h] + _dot(qk_l[h], v16_l[h]) for h in heads]
    for h in heads:
        k_dec = (k_l[h // rep] * e_end[:, h:h + 1]).astype(BF16)
        st_ref[0, h] = s_l[h] * etot[:, h:h + 1] + _dot_tn(k_dec, v16_l[h])
    for h in heads:
        o_ref[0, :, h * hd:(h + 1) * hd] = ((_rms_unit(o_l[h]) * nw_ref[...])
                                            * _silu(z_ref[0, :, h * hd:(h + 1) * hd]))


def gdn_mixer(qkv, z, small, small_t, state0, conv0, conv_w, dt_bias, a_log, norm_w, l_valid):
    b, lp, _ = qkv.shape
    nc = lp // CHUNK
    h = GDN_V_HEADS
    const2 = lambda i, j: (0, 0)
    tok = lambda w: pl.BlockSpec((1, CHUNK, w), lambda i, j: (i, j, 0))
    in_specs = [
        tok(GDN_CONV_DIM), tok(GDN_V_DIM), tok(LANES),
        pl.BlockSpec((1, 1, 32, CHUNK), lambda i, j: (i, j, 0, 0)),
        pl.BlockSpec((1, h, GDN_HEAD_DIM, GDN_HEAD_DIM), lambda i, j: (i, 0, 0, 0)),
        pl.BlockSpec((1, CONV_W - 1, GDN_CONV_DIM), lambda i, j: (i, 0, 0)),
        pl.BlockSpec((CONV_W, GDN_CONV_DIM), const2),
        pl.BlockSpec((1, h), const2), pl.BlockSpec((h, 1), const2),
        pl.BlockSpec((1, h), const2), pl.BlockSpec((h, 1), const2),
        pl.BlockSpec((1, GDN_HEAD_DIM), const2),
    ]
    out_specs = [tok(GDN_V_DIM), pl.BlockSpec((1, h, GDN_HEAD_DIM, GDN_HEAD_DIM), lambda i, j: (i, 0, 0, 0))]
    out_shape = [jax.ShapeDtypeStruct((b, lp, GDN_V_DIM), F32),
                 jax.ShapeDtypeStruct((b, h, GDN_HEAD_DIM, GDN_HEAD_DIM), F32)]
    return pl.pallas_call(
        functools.partial(_gdn_body, l_valid=l_valid),
        grid=(b, nc), in_specs=in_specs, out_specs=out_specs, out_shape=out_shape,
        scratch_shapes=[pltpu.VMEM((CHUNK + 8, GDN_CONV_DIM), F32)],
        compiler_params=_cparams(("parallel", "arbitrary")), name="gdn_mixer",
    )(qkv, z, small, small_t, state0, conv0, conv_w,
      dt_bias.reshape(1, h), dt_bias.reshape(h, 1), a_log.reshape(1, h), a_log.reshape(h, 1),
      norm_w.reshape(1, GDN_HEAD_DIM))


def _prep_ab_weights(w_in_ab):
    o = [0, SSD_INNER]
    o.append(o[-1] + SSD_CONV_DIM)
    o.append(o[-1] + SSD_HEADS)
    o.append(o[-1] + GDN_CONV_DIM)
    o.append(o[-1] + GDN_V_DIM)
    o.append(o[-1] + 2 * GDN_V_HEADS)
    w = w_in_ab.astype(BF16)
    small = jnp.concatenate([w[:, o[2]:o[3]], w[:, o[5]:o[6]]], axis=1)
    small = jnp.pad(small, ((0, 0), (0, LANES - small.shape[1])))
    return [w[:, o[0]:o[1]], w[:, o[1]:o[2]], w[:, o[3]:o[4]], w[:, o[4]:o[5]], small]


def _pad_time(a, lp):
    return jnp.pad(a, [(0, 0), (0, lp - a.shape[1])] + [(0, 0)] * (a.ndim - 2))


def layer0_mixer(x, ssd_state0, ssd_conv0, gdn_state0, gdn_conv0, p, tm):
    b, l, d = x.shape
    lp = -(-l // CHUNK) * CHUNK
    xp = _pad_time(x, lp).reshape(b * lp, d)
    z_ssd, xbc, qkv, z_gdn, small = norm_proj(xp, p["norm_mix"][0], p["w_ab"], min(tm, b * lp))
    nc = lp // CHUNK
    small_t = small[:, :32].reshape(b, nc, CHUNK, 32).transpose(0, 1, 3, 2)
    r3 = lambda a: a.reshape(b, lp, a.shape[-1])
    xbc, qkv, small = r3(xbc), r3(qkv), r3(small)
    y_ssd, ssd_state = ssd_mixer(xbc, r3(z_ssd), small, small_t, ssd_state0, ssd_conv0, p["ssd_conv_w"],
                                 p["ssd_conv_b"], p["ssd_dt_bias"], p["ssd_a_log"], p["ssd_d"], p["ssd_norm_w"], l)
    y_gdn, gdn_state = gdn_mixer(qkv, r3(z_gdn), small, small_t, gdn_state0, gdn_conv0, p["gdn_conv_w"],
                                 p["gdn_dt_bias"], p["gdn_a_log"], p["gdn_norm_w"], l)
    ssd_conv = jnp.concatenate([ssd_conv0, xbc[:, :l]], axis=1)[:, l:]
    gdn_conv = jnp.concatenate([gdn_conv0, qkv[:, :l]], axis=1)[:, l:]
    y_ssd = y_ssd[:, :l].reshape(b * l, SSD_INNER)
    y_gdn = y_gdn[:, :l].reshape(b * l, GDN_V_DIM)
    return y_ssd, y_gdn, ssd_state, ssd_conv, gdn_state, gdn_conv


def _nsa_lt_body(*refs, n_in):
    x_refs = refs[:n_in]
    wk_ref, wv_ref, o_ref = refs[n_in:]
    w = 2 * NSA_KV_COLS
    acc_k = acc_v = None
    for j in range(CMP_STRIDE):
        xk = jnp.concatenate([r[0, :, j * w:j * w + NSA_KV_COLS] for r in x_refs], axis=0).astype(BF16)
        xv = jnp.concatenate([r[0, :, j * w + NSA_KV_COLS:(j + 1) * w] for r in x_refs], axis=0).astype(BF16)
        dk, dv = _dot(xk, wk_ref[j]), _dot(xv, wv_ref[j])
        acc_k = dk if acc_k is None else acc_k + dk
        acc_v = dv if acc_v is None else acc_v + dv
    o_ref[0, :, 0:2 * NSA_KV_COLS] = acc_k
    o_ref[0, :, 2 * NSA_KV_COLS:] = acc_v


def _lt_weights(w1):
    w1r = w1.reshape(2, CMP_STRIDE, NSA_HEAD_DIM, CMP_HIDDEN)
    eye = jnp.eye(NSA_KV_HEADS, dtype=w1.dtype)
    lead = jnp.einsum("ab,jdh->jadbh", eye, w1r[0]).reshape(CMP_STRIDE, NSA_KV_COLS, NSA_KV_COLS)
    tail = jnp.einsum("ab,jdh->jadbh", eye, w1r[1]).reshape(CMP_STRIDE, NSA_KV_COLS, NSA_KV_COLS)
    return jnp.concatenate([lead, tail], axis=-1).astype(BF16)


def nsa_lt_dense(kv, wk, wv, tc):
    b, l, w = kv.shape
    nch = l // CMP_STRIDE
    x = kv[:, :nch * CMP_STRIDE].reshape(b, nch, CMP_STRIDE * w)
    tc = min(tc, nch)
    assert nch % tc == 0
    const3 = lambda i, j: (0, 0, 0)
    return pl.pallas_call(
        functools.partial(_nsa_lt_body, n_in=1), grid=(b, nch // tc),
        in_specs=[pl.BlockSpec((1, tc, CMP_STRIDE * w), lambda i, j: (i, j, 0)),
                  pl.BlockSpec(wk.shape, const3), pl.BlockSpec(wv.shape, const3)],
        out_specs=pl.BlockSpec((1, tc, 4 * NSA_KV_COLS), lambda i, j: (i, j, 0)),
        out_shape=jax.ShapeDtypeStruct((b, nch, 4 * NSA_KV_COLS), F32),
        compiler_params=_cparams(("parallel", "parallel")), name="nsa_lt_dense",
    )(x, wk, wv)


def _nsa_lt_paged_body(pt_ref, *refs, n_pg):
    page_refs = refs[:n_pg]
    wk_ref, wv_ref, o_ref, x_ref = refs[n_pg:]
    page = page_refs[0].shape[2]
    n_col = page_refs[0].shape[1] // LANES
    for i, r in enumerate(page_refs):
        for c in range(n_col):
            x_ref[c, i * page:(i + 1) * page, :] = r[0, c * LANES:(c + 1) * LANES, :].T
    rows = n_pg * page // CMP_STRIDE
    half = n_col // 2
    acc_k = acc_v = None
    for j in range(CMP_STRIDE):
        xj = [x_ref[c, pl.ds(j, rows, stride=CMP_STRIDE), :] for c in range(n_col)]
        dk = _dot(jnp.concatenate(xj[:half], axis=1).astype(BF16), wk_ref[j])
        dv = _dot(jnp.concatenate(xj[half:], axis=1).astype(BF16), wv_ref[j])
        acc_k = dk if acc_k is None else acc_k + dk
        acc_v = dv if acc_v is None else acc_v + dv
    o_ref[0, :, 0:2 * NSA_KV_COLS] = acc_k
    o_ref[0, :, 2 * NSA_KV_COLS:] = acc_v


def nsa_lt_paged(cache_t, page_table, wk, wv, n_pg):
    npool, w, page = cache_t.shape
    b, n_pages = page_table.shape
    rows = page // CMP_STRIDE
    assert n_pages % n_pg == 0
    page_spec = lambda i: pl.BlockSpec((1, w, page), lambda bi, gi, pt: (pt[bi, gi * n_pg + i], 0, 0))
    const3 = lambda bi, gi, pt: (0, 0, 0)
    grid_spec = pltpu.PrefetchScalarGridSpec(
        num_scalar_prefetch=1, grid=(b, n_pages // n_pg),
        in_specs=[page_spec(i) for i in range(n_pg)] + [pl.BlockSpec(wk.shape, const3), pl.BlockSpec(wv.shape, const3)],
        out_specs=pl.BlockSpec((1, n_pg * rows, 4 * NSA_KV_COLS), lambda bi, gi, pt: (bi, gi, 0)),
        scratch_shapes=[pltpu.VMEM((w // LANES, n_pg * page, LANES), F32)])
    return pl.pallas_call(
        functools.partial(_nsa_lt_paged_body, n_pg=n_pg), grid_spec=grid_spec,
        out_shape=jax.ShapeDtypeStruct((b, n_pages * rows, 4 * NSA_KV_COLS), F32),
        compiler_params=_cparams(("parallel", "parallel")), name="nsa_lt_paged",
    )(page_table, *([cache_t] * n_pg), wk, wv)


def _cmp_kv_body(lt_ref, pek_ref, w1k_ref, b1k_ref, w2k_ref, pev_ref, w1v_ref, b1v_ref, w2v_ref, w2vt_ref,
                 kc_ref, vc_ref, vct_ref, sh_ref):
    nch = lt_ref.shape[1]
    kvc = NSA_KV_COLS

    def branch(off, pe_ref, w1_ref, b1_ref, w2_ref, out_ref):
        pe = jnp.broadcast_to(pe_ref[...], (8, pe_ref.shape[1]))
        pe_term = _mm3(pe, w1_ref[...])[0:1, :] + b1_ref[...]
        pe4 = jnp.concatenate([pe_term] * NSA_KV_HEADS, axis=1)
        sh_ref[0:nch, :] = lt_ref[0, :, off + kvc:off + 2 * kvc]
        sh_ref[nch:nch + 8, :] = jnp.zeros((8, kvc), F32)
        hid = _silu(lt_ref[0, :, off:off + kvc] + sh_ref[1:nch + 1, :] + pe4).astype(BF16)
        out_ref[0] = _dot(hid, w2_ref[...])
        return hid

    branch(0, pek_ref, w1k_ref, b1k_ref, w2k_ref, kc_ref)
    hid_v = branch(2 * kvc, pev_ref, w1v_ref, b1v_ref, w2v_ref, vc_ref)
    vct_ref[0] = _dot_nt(w2vt_ref[...], hid_v)


def cmp_kv(lt, p):
    b, nch, _ = lt.shape
    kvc = NSA_KV_COLS
    const2 = lambda i: (0, 0)
    args, specs = [], []
    for s in ("k", "v"):
        pe = p["cmp_pe_" + s].reshape(1, -1)
        w1 = p["cmp_w1_" + s]
        b1 = p["cmp_b1_" + s].reshape(1, -1)
        w2 = jnp.kron(jnp.eye(NSA_KV_HEADS, dtype=F32), p["cmp_w2_" + s]).astype(BF16)
        for a in (pe, w1, b1, w2):
            args.append(a)
            specs.append(pl.BlockSpec(a.shape, const2))
    args.append(jnp.kron(jnp.eye(NSA_KV_HEADS, dtype=F32), p["cmp_w2_v"].T).astype(BF16))
    specs.append(pl.BlockSpec(args[-1].shape, const2))
    out_spec = pl.BlockSpec((1, nch, kvc), lambda i: (i, 0, 0))
    return pl.pallas_call(
        _cmp_kv_body, grid=(b,),
        in_specs=[pl.BlockSpec((1, nch, 4 * kvc), lambda i: (i, 0, 0))] + specs,
        out_specs=[out_spec, out_spec, pl.BlockSpec((1, kvc, nch), lambda i: (i, 0, 0))],
        out_shape=[jax.ShapeDtypeStruct((b, nch, kvc), F32)] * 2 + [jax.ShapeDtypeStruct((b, kvc, nch), F32)],
        scratch_shapes=[pltpu.VMEM((nch + 8, kvc), F32)],
        compiler_params=_cparams(("parallel",)), name="cmp_kv",
    )(lt, *args)


def _cmp_attn_body(q_ref, gate_ref, kc_ref, vc_ref, oc_ref, sel_ref, *, tq, pos0, n_slc, nbp):
    qi = pl.program_id(1)
    ncp = kc_ref.shape[1]
    hd = NSA_HEAD_DIM
    t_col = pos0 + qi * tq + _iota2((tq, 1), 0)
    cmask = (_iota2((1, ncp), 1) * CMP_STRIDE + (CMP_LEN - 1)) <= t_col
    kc = kc_ref[0].astype(BF16)
    vc = vc_ref[0].astype(BF16)
    gate = jax.nn.sigmoid(gate_ref[0])
    nn = _iota2((ncp, nbp), 0) * CMP_STRIDE
    ss = _iota2((ncp, nbp), 1) * SLC_BLOCK
    overlap = jnp.where((nn < ss + SLC_BLOCK) & (nn + CMP_LEN > ss), 1.0, 0.0).astype(BF16)
    blk = _iota2((1, nbp), 1)
    blk_f = blk.astype(F32)
    cur = lax.shift_right_logical(t_col, 6)
    forced = (blk == 0) | (blk == cur) | (blk == cur - 1)
    bonus = jnp.where(forced, FORCE_BONUS, 0.0)
    causal_blk = blk * SLC_BLOCK <= t_col
    n_pick = min(SLC_TOPN, n_slc)

    scores = []
    for k in range(NSA_KV_HEADS):
        kc_k = kc[:, k * hd:(k + 1) * hd]
        vc_k = vc[:, k * hd:(k + 1) * hd]
        hs = [k * NSA_GROUP + g for g in range(NSA_GROUP)]
        q_l = [(q_ref[0, :, h * hd:(h + 1) * hd] * (hd ** -0.5)).astype(BF16) for h in hs]
        s_l = [jnp.where(cmask, _dot_nt(q_h, kc_k), NEG_INF) for q_h in q_l]
        e_l = [jnp.where(cmask, jnp.exp(s - jnp.max(s, axis=-1, keepdims=True)), 0.0) for s in s_l]
        den_l = [jnp.sum(e, axis=-1, keepdims=True) for e in e_l]
        p_l = [e * (1.0 / jnp.where(den > 0.0, den, 1.0)) for e, den in zip(e_l, den_l)]
        o_l = [_dot(p.astype(BF16), vc_k) for p in p_l]
        for h, o in zip(hs, o_l):
            oc_ref[0, :, h * hd:(h + 1) * hd] = o * gate[:, h:h + 1]
        ph, plo = _split2((p_l[0] + p_l[1]) + (p_l[2] + p_l[3]))
        imp = _dot(ph, overlap) + _dot(plo, overlap)
        score = jnp.where(causal_blk, imp + bonus, NEG_INF)
        scores.append(jnp.where(blk < n_slc, score, -jnp.inf))

    def pick_one(_, carry):
        work, sel = carry
        m = jnp.max(work, axis=-1, keepdims=True)
        idx = jnp.min(jnp.where(work == m, blk_f, 1e9), axis=-1, keepdims=True)
        hit = blk_f == idx
        return jnp.where(hit, -jnp.inf, work), jnp.where(hit, 1.0, sel)

    work0 = jnp.concatenate(scores, axis=0)
    _, sel = lax.fori_loop(0, n_pick, pick_one, (work0, jnp.zeros(work0.shape, F32)))
    for k in range(NSA_KV_HEADS):
        sel_ref[0, :, k * nbp:(k + 1) * nbp] = sel[k * tq:(k + 1) * tq]


def _nbp(n_slc):
    return 64 if n_slc <= 64 else -(-n_slc // LANES) * LANES


def cmp_attn_select(q, gate, kc, vc, tq, pos0, n_slc):
    b, l, dq = q.shape
    ncp = kc.shape[1]
    nbp = _nbp(n_slc)
    assert l % tq == 0
    tok = lambda w: pl.BlockSpec((1, tq, w), lambda i, j: (i, j, 0))
    whole = pl.BlockSpec((1, ncp, NSA_KV_COLS), lambda i, j: (i, 0, 0))
    return pl.pallas_call(
        functools.partial(_cmp_attn_body, tq=tq, pos0=pos0, n_slc=n_slc, nbp=nbp),
        grid=(b, l // tq), in_specs=[tok(dq), tok(LANES), whole, whole],
        out_specs=[tok(dq), tok(NSA_KV_HEADS * nbp)],
        out_shape=[jax.ShapeDtypeStruct((b, l, dq), F32), jax.ShapeDtypeStruct((b, l, NSA_KV_HEADS * nbp), F32)],
        compiler_params=_cparams(("parallel", "parallel")), name="cmp_attn_select",
    )(q, gate, kc, vc)


def _nsa_attn_body(q_ref, gate_ref, kv_ref, *rest, mode, tq, tk, qpos0, kpos0, gate_off, nbp):
    if mode == "slc":
        sel_ref, o_ref, qs_ref, m_ref, l_ref, acc_ref = rest
    else:
        o_ref, qs_ref, m_ref, l_ref, acc_ref = rest
    qi = pl.program_id(1)
    hd = NSA_HEAD_DIM
    nk = kv_ref.shape[1] // tk
    qmin = qpos0 + qi * tq
    t_col = qmin + _iota2((tq, 1), 0)
    for k in range(NSA_KV_HEADS):
        qs_ref[k] = jnp.concatenate(
            [q_ref[0, :, (k * NSA_GROUP + g) * hd:(k * NSA_GROUP + g + 1) * hd] * (hd ** -0.5)
             for g in range(NSA_GROUP)], axis=0)
    m_ref[...] = jnp.full(m_ref.shape, NEG_INF, F32)
    l_ref[...] = jnp.zeros(l_ref.shape, F32)
    acc_ref[...] = jnp.zeros(acc_ref.shape, F32)
    hi = jnp.minimum(nk - 1, (qmin + tq - 1 - kpos0) // tk)
    if mode == "slc":
        lo = 0
    else:
        lo = jnp.maximum(qmin - (WINDOW - 1) - kpos0, 0) // tk

    def step(ki, carry):
        k0 = pl.multiple_of(ki * tk, tk)
        kv = kv_ref[0, pl.ds(k0, tk), :]
        kpos = kpos0 + k0 + _iota2((1, tk), 1)
        if mode == "slc":
            expand = jnp.where(_iota2((nbp, tk), 0) == lax.shift_right_logical(k0 + _iota2((nbp, tk), 1), 6),
                               1.0, 0.0).astype(BF16)
            base = kpos <= t_col
        else:
            dist = t_col - kpos
            base = (dist >= 0) & (dist < WINDOW)
        for k in range(NSA_KV_HEADS):
            kk = kv[:, k * hd:(k + 1) * hd].astype(BF16)
            vv = kv[:, NSA_KV_COLS + k * hd:NSA_KV_COLS + (k + 1) * hd].astype(BF16)
            s = _dot_nt(qs_ref[k].astype(BF16), kk)
            if mode == "slc":
                mask = base & (_dot(sel_ref[0, :, k * nbp:(k + 1) * nbp].astype(BF16), expand) > 0.5)
            else:
                mask = base
            for g in range(NSA_GROUP):
                r = k * NSA_GROUP + g
                s_g = jnp.where(mask, s[g * tq:(g + 1) * tq], NEG_INF)
                m_old = m_ref[r]
                m_new = jnp.maximum(m_old, jnp.max(s_g, axis=-1, keepdims=True))
                alpha = jnp.exp(m_old - m_new)
                p = jnp.where(mask, jnp.exp(s_g - m_new), 0.0)
                l_ref[r] = alpha * l_ref[r] + jnp.sum(p, axis=-1, keepdims=True)
                acc_ref[r] = alpha * acc_ref[r] + _dot(p.astype(BF16), vv)
                m_ref[r] = m_new
        return carry

    lax.fori_loop(lo, hi + 1, step, 0)
    gate = jax.nn.sigmoid(gate_ref[0])
    for r in range(NSA_HEADS):
        l = l_ref[r]
        o = acc_ref[r] / jnp.where(l > 0.0, l, 1.0)
        o_ref[0, :, r * hd:(r + 1) * hd] = o * gate[:, gate_off + r:gate_off + r + 1]


def nsa_attn(mode, q, gate, kv, sel, tq, tk, qpos0, kpos0):
    b, l, dq = q.shape
    nkeys = kv.shape[1]
    assert l % tq == 0 and nkeys % tk == 0
    tok = lambda w: pl.BlockSpec((1, tq, w), lambda i, j: (i, j, 0))
    in_specs = [tok(dq), tok(LANES), pl.BlockSpec((1, nkeys, kv.shape[2]), lambda i, j: (i, 0, 0))]
    args = [q, gate, kv]
    nbp = 0
    if mode == "slc":
        nbp = sel.shape[2] // NSA_KV_HEADS
        in_specs.append(tok(sel.shape[2]))
        args.append(sel)
    return pl.pallas_call(
        functools.partial(_nsa_attn_body, mode=mode, tq=tq, tk=tk, qpos0=qpos0, kpos0=kpos0,
                          gate_off=NSA_HEADS * (1 if mode == "slc" else 2), nbp=nbp),
        grid=(b, l // tq), in_specs=in_specs, out_specs=tok(dq),
        out_shape=jax.ShapeDtypeStruct((b, l, dq), F32),
        scratch_shapes=[pltpu.VMEM((NSA_KV_HEADS, NSA_GROUP * tq, NSA_HEAD_DIM), F32),
                        pltpu.VMEM((NSA_HEADS, tq, 1), F32), pltpu.VMEM((NSA_HEADS, tq, 1), F32),
                        pltpu.VMEM((NSA_HEADS, tq, NSA_HEAD_DIM), F32)],
        compiler_params=_cparams(("parallel", "parallel")), name="nsa_attn_" + mode,
    )(*args)


def _slc_paged_body(pt_ref, q_ref, gate_ref, selgrp_ref, selnew_ref, kvnew_ref, *rest, n_pg, tq, l_new):
    page_refs = rest[:n_pg]
    o_ref, qbd_ref, m_ref, l_ref, acc_ref = rest[n_pg:]
    gi = pl.program_id(1)
    hd = NSA_HEAD_DIM
    rows = NSA_HEADS * tq

    @pl.when(gi == 0)
    def _():
        qbd_ref[...] = jnp.zeros(qbd_ref.shape, F32)
        for h in range(NSA_HEADS):
            k = h // NSA_GROUP
            qbd_ref[h * tq:(h + 1) * tq, k * hd:(k + 1) * hd] = q_ref[0, :, h * hd:(h + 1) * hd] * (hd ** -0.5)
        m_ref[...] = jnp.full(m_ref.shape, NEG_INF, F32)
        l_ref[...] = jnp.zeros(l_ref.shape, F32)
        acc_ref[...] = jnp.zeros(acc_ref.shape, F32)

    qbd = qbd_ref[...].astype(BF16)

    def update(s, vv, v_transposed):
        m_old = m_ref[...]
        m_new = jnp.maximum(m_old, jnp.max(s, axis=-1, keepdims=True))
        alpha = jnp.exp(m_old - m_new)
        p = jnp.exp(s - m_new)
        l_ref[...] = alpha * l_ref[...] + jnp.sum(p, axis=-1, keepdims=True)
        acc_ref[...] = alpha * acc_ref[...] + (_dot_nt if v_transposed else _dot)(p.astype(BF16), vv)
        m_ref[...] = m_new

    page = page_refs[0].shape[2]
    wide = n_pg * page
    nblk = wide // SLC_BLOCK
    kt = jnp.concatenate([r[0, 0:NSA_KV_COLS, :] for r in page_refs], axis=1).astype(BF16)
    vt = jnp.concatenate([r[0, NSA_KV_COLS:, :] for r in page_refs], axis=1).astype(BF16)
    expand = jnp.where(_iota2((nblk, wide), 0) == lax.shift_right_logical(_iota2((nblk, wide), 1), 6),
                       1.0, 0.0).astype(BF16)
    selb = ((1.0 - selgrp_ref[0, 0]) * MASKED).astype(BF16)
    update(_dot(qbd, kt) + _dot(selb, expand), vt, True)

    @pl.when(gi == pl.num_programs(1) - 1)
    def _():
        kvn = kvnew_ref[0]
        nr = kvn.shape[0]
        j = _iota2((rows, nr), 1)
        iq = jnp.bitwise_and(_iota2((rows, nr), 0), tq - 1)
        mask = (selnew_ref[0] > 0.5) & (j <= iq) & (j < l_new)
        s_new = _dot_nt(qbd, kvn[:, :NSA_KV_COLS].astype(BF16)) + jnp.where(mask, 0.0, MASKED)
        update(s_new, kvn[:, NSA_KV_COLS:].astype(BF16), False)
        gate = jax.nn.sigmoid(gate_ref[0])
        l = l_ref[...]
        o = acc_ref[...] / jnp.where(l > 0.0, l, 1.0)
        for h in range(NSA_HEADS):
            k = h // NSA_GROUP
            o_ref[0, :, h * hd:(h + 1) * hd] = (o[h * tq:(h + 1) * tq, k * hd:(k + 1) * hd]
                                                * gate[:, NSA_HEADS + h:NSA_HEADS + h + 1])


def slc_paged(q, gate, selrows, kv_new, cache_t, page_table, n_pg, l_new, p0):
    b, tq, dq = q.shape
    npool, w, page = cache_t.shape
    n_pages = page_table.shape[1]
    rows = NSA_HEADS * tq
    n_grp = n_pages // n_pg
    nblk = n_pg * page // SLC_BLOCK
    assert n_pages % n_pg == 0 and p0 % SLC_BLOCK == 0 and l_new <= SLC_BLOCK and p0 == n_pages * page
    assert tq & (tq - 1) == 0
    blk_new = p0 // SLC_BLOCK
    sel_grp = selrows[:, :, :blk_new].reshape(b, rows, n_grp, nblk).transpose(0, 2, 1, 3)
    sel_new = selrows[:, :, blk_new:blk_new + 1]
    per_b = lambda shape: pl.BlockSpec((1,) + shape, lambda bi, gi, pt: (bi, 0, 0))
    page_spec = lambda i: pl.BlockSpec((1, w, page), lambda bi, gi, pt: (pt[bi, gi * n_pg + i], 0, 0))
    grid_spec = pltpu.PrefetchScalarGridSpec(
        num_scalar_prefetch=1, grid=(b, n_grp),
        in_specs=[per_b((tq, dq)), per_b((tq, LANES)),
                  pl.BlockSpec((1, 1, rows, nblk), lambda bi, gi, pt: (bi, gi, 0, 0)),
                  per_b((rows, 1)), per_b((kv_new.shape[1], w))]
        + [page_spec(i) for i in range(n_pg)],
        out_specs=per_b((tq, dq)),
        scratch_shapes=[pltpu.VMEM((rows, NSA_KV_COLS), F32), pltpu.VMEM((rows, 1), F32),
                        pltpu.VMEM((rows, 1), F32), pltpu.VMEM((rows, NSA_KV_COLS), F32)])
    return pl.pallas_call(
        functools.partial(_slc_paged_body, n_pg=n_pg, tq=tq, l_new=l_new),
        grid_spec=grid_spec, out_shape=jax.ShapeDtypeStruct((b, tq, dq), F32),
        compiler_params=_cparams(("parallel", "arbitrary")), name="slc_paged",
    )(page_table, q, gate, sel_grp, sel_new, kv_new, *([cache_t] * n_pg))


def _cmp_attn_t_body(qt_ref, gt_ref, kc_ref, vct_ref, oct_ref, selt_ref, *, tq, pos0, n_slc, nbp):
    qi = pl.program_id(1)
    ncp = kc_ref.shape[1]
    hd = NSA_HEAD_DIM
    t_row = pos0 + qi * tq + _iota2((1, tq), 1)
    cmask = (_iota2((ncp, 1), 0) * CMP_STRIDE + (CMP_LEN - 1)) <= t_row
    kc = kc_ref[0].astype(BF16)
    vct = vct_ref[0].astype(BF16)
    gate = jax.nn.sigmoid(gt_ref[0, 0:NSA_HEADS, :])
    ss = _iota2((nbp, ncp), 0) * SLC_BLOCK
    nn = _iota2((nbp, ncp), 1) * CMP_STRIDE
    overlap_t = jnp.where((nn < ss + SLC_BLOCK) & (nn + CMP_LEN > ss), 1.0, 0.0).astype(BF16)
    blk = _iota2((nbp, 1), 0)
    blk_f = blk.astype(F32)
    cur = lax.shift_right_logical(t_row, 6)
    forced = (blk == 0) | (blk == cur) | (blk == cur - 1)
    bonus = jnp.where(forced, FORCE_BONUS, 0.0)
    causal_blk = blk * SLC_BLOCK <= t_row
    n_pick = min(SLC_TOPN, n_slc)

    scores = []
    for k in range(NSA_KV_HEADS):
        kc_k = kc[:, k * hd:(k + 1) * hd]
        vct_k = vct[k * hd:(k + 1) * hd, :]
        hs = [k * NSA_GROUP + g for g in range(NSA_GROUP)]
        q_l = [(qt_ref[0, h * hd:(h + 1) * hd, :] * (hd ** -0.5)).astype(BF16) for h in hs]
        s_l = [jnp.where(cmask, _dot(kc_k, q_h), NEG_INF) for q_h in q_l]
        e_l = [jnp.where(cmask, jnp.exp(s - jnp.max(s, axis=0, keepdims=True)), 0.0) for s in s_l]
        den_l = [jnp.sum(e, axis=0, keepdims=True) for e in e_l]
        p_l = [e * (1.0 / jnp.where(den > 0.0, den, 1.0)) for e, den in zip(e_l, den_l)]
        o_l = [_dot(vct_k, p.astype(BF16)) for p in p_l]
        for h, o in zip(hs, o_l):
            oct_ref[0, h * hd:(h + 1) * hd, :] = o * gate[h:h + 1, :]
        psum = (p_l[0] + p_l[1]) + (p_l[2] + p_l[3])
        ph, plo = _split2(psum)
        imp = _dot(overlap_t, ph) + _dot(overlap_t, plo)
        score = jnp.where(causal_blk, imp + bonus, NEG_INF)
        scores.append(jnp.where(blk < n_slc, score, -jnp.inf))

    def pick_one(_, carry):
        work, sel = carry
        m = jnp.max(work, axis=0, keepdims=True)
        idx = jnp.min(jnp.where(work == m, blk_f, 1e9), axis=0, keepdims=True)
        hit = blk_f == idx
        return jnp.where(hit, -jnp.inf, work), jnp.where(hit, 1.0, sel)

    work0 = jnp.concatenate(scores, axis=1)
    _, sel = lax.fori_loop(0, n_pick, pick_one, (work0, jnp.zeros(work0.shape, F32)))
    for k in range(NSA_KV_HEADS):
        selt_ref[0, k * nbp:(k + 1) * nbp, :] = sel[:, k * tq:(k + 1) * tq]


def cmp_attn_select_t(qt, gate_t, kc, vct, b, tq, pos0, n_slc):
    nt, dq, _ = qt.shape
    nq = nt // b
    ncp = kc.shape[1]
    nbp = _nbp(n_slc)
    tile = lambda rows: pl.BlockSpec((1, rows, tq), lambda i, j: (i * nq + j, 0, 0))
    return pl.pallas_call(
        functools.partial(_cmp_attn_t_body, tq=tq, pos0=pos0, n_slc=n_slc, nbp=nbp),
        grid=(b, nq),
        in_specs=[tile(dq), tile(LANES), pl.BlockSpec((1, ncp, NSA_KV_COLS), lambda i, j: (i, 0, 0)),
                  pl.BlockSpec((1, NSA_KV_COLS, ncp), lambda i, j: (i, 0, 0))],
        out_specs=[tile(dq), tile(NSA_KV_HEADS * nbp)],
        out_shape=[jax.ShapeDtypeStruct((nt, dq, tq), F32), jax.ShapeDtypeStruct((nt, NSA_KV_HEADS * nbp, tq), F32)],
        compiler_params=_cparams(("parallel", "parallel")), name="cmp_attn_select_t",
    )(qt, gate_t, kc, vct)


def _nsa_attn_t_body(qt_ref, gt_ref, kv_ref, vt_ref, *rest, mode, tq, tk, gate_off, nbp):
    if mode == "slc":
        selt_ref, o_ref, qs_ref, m_ref, l_ref, acc_ref = rest
    else:
        o_ref, qs_ref, m_ref, l_ref, acc_ref = rest
    qi = pl.program_id(1)
    hd = NSA_HEAD_DIM
    nk = kv_ref.shape[1] // tk
    qmin = qi * tq
    t_row = qmin + _iota2((1, tq), 1)
    for r in range(NSA_HEADS):
        qs_ref[r, 0:hd, :] = (qt_ref[0, r * hd:(r + 1) * hd, :] * (hd ** -0.5 * LOG2E)).astype(BF16)
        if mode == "slc":
            k = r // NSA_GROUP
            qs_ref[r, hd:hd + nbp, :] = ((1.0 - selt_ref[0, k * nbp:(k + 1) * nbp, :]) * MASKED).astype(BF16)
    m_ref[...] = jnp.full(m_ref.shape, NEG_INF, F32)
    l_ref[...] = jnp.zeros(l_ref.shape, F32)
    acc_ref[...] = jnp.zeros(acc_ref.shape, F32)
    hi = jnp.minimum(nk - 1, (qmin + tq - 1) // tk)

    heads_per_stage = 2 * NSA_GROUP

    def tile(ki, positional):
        k0 = pl.multiple_of(ki * tk, tk)
        krows = kv_ref[0, pl.ds(k0, tk), 0:NSA_KV_COLS].astype(BF16)
        vt = vt_ref[ki].astype(BF16)
        kpos = k0 + _iota2((tk, 1), 0)
        if mode == "slc":
            expand_t = jnp.where(_iota2((tk, nbp), 1) == lax.shift_right_logical(k0 + _iota2((tk, nbp), 0), 6),
                                 1.0, 0.0).astype(BF16)
            cbias = jnp.where(kpos <= t_row, 0.0, MASKED) if positional else None
        else:
            dist = t_row - kpos
            cbias = jnp.where((dist >= 0) & (dist < WINDOW), 0.0, MASKED)
        k_l, vt_l = [], []
        for k in range(NSA_KV_HEADS):
            k_k = krows[:, k * hd:(k + 1) * hd]
            k_l.append(jnp.concatenate([k_k, expand_t], axis=1) if mode == "slc" else k_k)
            vt_l.append(vt[k * hd:(k + 1) * hd, :])
        for r0 in range(0, NSA_HEADS, heads_per_stage):
            rs = list(range(r0, r0 + heads_per_stage))
            s_l = [_dot(k_l[r // NSA_GROUP], qs_ref[r]) for r in rs]
            if cbias is not None:
                s_l = [s + cbias for s in s_l]
            m_old = [m_ref[r] for r in rs]
            m_new = [jnp.maximum(mo, jnp.max(s, axis=0, keepdims=True)) for mo, s in zip(m_old, s_l)]
            p_l = [jnp.exp2(s - mn) for s, mn in zip(s_l, m_new)]
            pv_l = [_dot(vt_l[r // NSA_GROUP], p.astype(BF16)) for r, p in zip(rs, p_l)]
            for r, mo, mn, p, pv in zip(rs, m_old, m_new, p_l, pv_l):
                alpha = jnp.exp2(mo - mn)
                l_ref[r] = alpha * l_ref[r] + jnp.sum(p, axis=0, keepdims=True)
                acc_ref[r] = alpha * acc_ref[r] + pv
                m_ref[r] = mn

    def step(ki, carry):
        tile(ki, mode != "slc")
        return carry

    if mode == "slc":
        lax.fori_loop(0, hi, step, 0)
        tile(hi, True)
    else:
        lax.fori_loop(jnp.maximum(qmin - (WINDOW - 1), 0) // tk, hi + 1, step, 0)
    gate = jax.nn.sigmoid(gt_ref[0, gate_off:gate_off + NSA_HEADS, :])
    for r in range(NSA_HEADS):
        l = l_ref[r]
        o_ref[0, r * hd:(r + 1) * hd, :] = acc_ref[r] * (gate[r:r + 1, :] / jnp.where(l > 0.0, l, 1.0))


def nsa_attn_t(mode, qt, gate_t, kv, vt, selt, b, tq, tk):
    nt, dq, _ = qt.shape
    nq = nt // b
    l = kv.shape[1]
    nk = l // tk
    assert l % tk == 0 and vt.shape == (b * nk, NSA_KV_COLS, tk) and (mode != "slc" or tq == tk)
    tile = lambda rows: pl.BlockSpec((1, rows, tq), lambda i, j: (i * nq + j, 0, 0))
    in_specs = [tile(dq), tile(LANES), pl.BlockSpec((1, l, kv.shape[2]), lambda i, j: (i, 0, 0)),
                pl.BlockSpec((nk, NSA_KV_COLS, tk), lambda i, j: (i, 0, 0))]
    args = [qt, gate_t, kv, vt]
    nbp = 0
    if mode == "slc":
        nbp = selt.shape[1] // NSA_KV_HEADS
        in_specs.append(tile(selt.shape[1]))
        args.append(selt)
    return pl.pallas_call(
        functools.partial(_nsa_attn_t_body, mode=mode, tq=tq, tk=tk,
                          gate_off=NSA_HEADS * (1 if mode == "slc" else 2), nbp=nbp),
        grid=(b, nq), in_specs=in_specs, out_specs=tile(dq),
        out_shape=jax.ShapeDtypeStruct((nt, dq, tq), F32),
        scratch_shapes=[pltpu.VMEM((NSA_HEADS, NSA_HEAD_DIM + nbp, tq), BF16),
                        pltpu.VMEM((NSA_HEADS, 1, tq), F32), pltpu.VMEM((NSA_HEADS, 1, tq), F32),
                        pltpu.VMEM((NSA_HEADS, NSA_HEAD_DIM, tq), F32)],
        compiler_params=_cparams(("parallel", "parallel")), name="nsa_attn_t_" + mode,
    )(*args)


def _prep_nsa_weights(w_in_nsa):
    w = w_in_nsa.astype(BF16)
    dq = NSA_HEADS * NSA_HEAD_DIM
    kvw = 2 * NSA_KV_COLS
    gate = jnp.pad(w[:, dq + 3 * kvw:], ((0, 0), (0, LANES - 3 * NSA_HEADS)))
    return [w[:, :dq], w[:, dq:dq + kvw], w[:, dq + kvw:dq + 2 * kvw], w[:, dq + 2 * kvw:dq + 3 * kvw], gate]


def _kv5(a):
    return a.reshape(a.shape[0], a.shape[1], 2, NSA_KV_HEADS, NSA_HEAD_DIM)


def nsa_prompt(x, p, tm):
    b, l, d = x.shape
    assert l % tm == 0
    wq, wc, ws, ww, wg = p["w_nsa"]
    v_rows = slice(NSA_KV_COLS, 2 * NSA_KV_COLS)
    wts = [wq.T, ws.T[v_rows], ww.T[v_rows], wg.T]
    kvc, kvs, kvw, qt, vst, vwt, gt = norm_proj(x.reshape(b * l, d), p["norm_mix"][1], [wc, ws, ww], tm, wts)
    r3 = lambda a: a.reshape(b, l, a.shape[-1])
    kvc, kvs, kvw = r3(kvc), r3(kvs), r3(kvw)
    lt = nsa_lt_dense(kvc, p["lt_wk"], p["lt_wv"], 256)
    kc, _, vct = cmp_kv(lt, p)
    n_slc = -(-l // SLC_BLOCK)
    o_c, selt = cmp_attn_select_t(qt, gt, kc, vct, b, tm, 0, n_slc)
    o_s = nsa_attn_t("slc", qt, gt, kvs, vst, selt, b, tm, tm)
    o_w = nsa_attn_t("win", qt, gt, kvw, vwt, None, b, tm, tm)
    keep = min(WINDOW, l)
    return [o_c, o_s, o_w], _kv5(kvc), _kv5(kvs), _kv5(kvw[:, l - keep:])


def nsa_sample(x, cache_cmp, cache_slc, past_win, page_table, p, n_pg):
    b, lq, d = x.shape
    tq = 8
    assert lq <= tq
    npool, page = cache_cmp.shape[:2]
    w = 2 * NSA_KV_COLS
    p0 = page_table.shape[1] * page
    xp = _pad_time(x, tq).reshape(b * tq, d)
    q, kvc, kvs, kvw, gate = norm_proj(xp, p["norm_mix"][1], p["w_nsa"], b * tq)
    r3 = lambda a: a.reshape(b, tq, a.shape[-1])
    q, kvc, kvs, kvw, gate = r3(q), r3(kvc), r3(kvs), r3(kvw), r3(gate)
    pages_t = lambda c: c.reshape(npool, page, w).transpose(0, 2, 1)
    lt = nsa_lt_paged(pages_t(cache_cmp), page_table, p["lt_wk"], p["lt_wv"], n_pg)
    kc, vc, _ = cmp_kv(lt, p)
    n_slc = -(-(p0 + lq) // SLC_BLOCK)
    o_c, sel = cmp_attn_select(q, gate, kc, vc, tq, p0, n_slc)
    nbp = sel.shape[2] // NSA_KV_HEADS
    selrows = sel.reshape(b, tq, NSA_KV_HEADS, 1, nbp).transpose(0, 2, 3, 1, 4)
    selrows = jnp.broadcast_to(selrows, (b, NSA_KV_HEADS, NSA_GROUP, tq, nbp)).reshape(b, NSA_HEADS * tq, nbp)
    o_s = slc_paged(q, gate, selrows, kvs, pages_t(cache_slc), page_table, n_pg, lq, p0)
    n_past = past_win.shape[1]
    win_all = jnp.concatenate([past_win.reshape(b, n_past, w), kvw[:, :lq]], axis=1)
    nkeys = -(-(n_past + lq) // 8) * 8
    o_w = nsa_attn("win", q, gate, _pad_time(win_all, nkeys), None, tq, nkeys, p0, p0 - n_past)
    keep = min(WINDOW, n_past + lq)
    flat = lambda a: a[:, :lq].reshape(b * lq, a.shape[-1])
    return ([flat(o_c), flat(o_s), flat(o_w)], _kv5(kvc[:, :lq]), _kv5(kvs[:, :lq]),
            _kv5(win_all[:, n_past + lq - keep:]))


def _trunk(x, ssd_state0, ssd_conv0, gdn_state0, gdn_conv0, nsa_fn, p, tm):
    b, l, d = x.shape
    t = b * l
    tm = min(tm, t)
    y_ssd, y_gdn, ssd_state, ssd_conv, gdn_state, gdn_conv = layer0_mixer(
        x, ssd_state0, ssd_conv0, gdn_state0, gdn_conv0, p, tm)
    x1 = mix_mlp(x.reshape(t, d), [[y_ssd], [y_gdn]], p["w_out_ab"], p["norm_mlp"][0], p["w_up"][0],
                 p["w_down"][0], None, tm)
    branches, cmp_rows, slc_rows, win_rows = nsa_fn(x1.reshape(b, l, d))
    y = mix_mlp(x1, [branches], [p["w_out_nsa"]], p["norm_mlp"][1], p["w_up"][1], p["w_down"][1],
                p["norm_final"], tm, transposed=branches[0].ndim == 3)
    return y.reshape(b, l, d), ssd_state, ssd_conv, gdn_state, gdn_conv, cmp_rows, slc_rows, win_rows


def kernel(x_prompt, x_sample, state_ssd, state_ssd_conv, state_gdn, state_gdn_conv, cache_cmp_kv, cache_slc_kv,
           state_win_kv, page_table, norm_mix, norm_mlp, norm_final, w_in_ab, ssd_conv_w, ssd_conv_b, ssd_dt_bias,
           ssd_a_log, ssd_d, ssd_norm_w, gdn_conv_w, gdn_dt_bias, gdn_a_log, gdn_norm_w, w_out_ab, w_in_nsa,
           cmp_pe_k, cmp_w1_k, cmp_b1_k, cmp_w2_k, cmp_pe_v, cmp_w1_v, cmp_b1_v, cmp_w2_v, w_out_nsa, w_up, w_down):
    wo = w_out_ab.astype(BF16)
    p = dict(
        norm_mix=norm_mix, norm_mlp=norm_mlp, norm_final=norm_final, w_ab=_prep_ab_weights(w_in_ab),
        ssd_conv_w=ssd_conv_w, ssd_conv_b=ssd_conv_b, ssd_dt_bias=ssd_dt_bias, ssd_a_log=ssd_a_log, ssd_d=ssd_d,
        ssd_norm_w=ssd_norm_w, gdn_conv_w=gdn_conv_w, gdn_dt_bias=gdn_dt_bias, gdn_a_log=gdn_a_log,
        gdn_norm_w=gdn_norm_w, w_out_ab=[wo[:SSD_INNER], wo[SSD_INNER:]], w_nsa=_prep_nsa_weights(w_in_nsa),
        lt_wk=_lt_weights(cmp_w1_k), lt_wv=_lt_weights(cmp_w1_v),
        cmp_pe_k=cmp_pe_k, cmp_w1_k=cmp_w1_k, cmp_b1_k=cmp_b1_k, cmp_w2_k=cmp_w2_k,
        cmp_pe_v=cmp_pe_v, cmp_w1_v=cmp_w1_v, cmp_b1_v=cmp_b1_v, cmp_w2_v=cmp_w2_v,
        w_out_nsa=w_out_nsa.astype(BF16), w_up=w_up.astype(BF16), w_down=w_down.astype(BF16))
    bp = x_prompt.shape[0]
    zeros = lambda *s: jnp.zeros(s, F32)
    tm = 256
    out_p = _trunk(x_prompt, zeros(bp, SSD_HEADS, SSD_HEADDIM, SSD_STATE), zeros(bp, CONV_W - 1, SSD_CONV_DIM),
                   zeros(bp, GDN_V_HEADS, GDN_HEAD_DIM, GDN_HEAD_DIM), zeros(bp, CONV_W - 1, GDN_CONV_DIM),
                   lambda x1: nsa_prompt(x1, p, tm), p, tm)
    out_s = _trunk(x_sample, state_ssd, state_ssd_conv, state_gdn, state_gdn_conv,
                   lambda x1: nsa_sample(x1, cache_cmp_kv, cache_slc_kv, state_win_kv, page_table, p, 16), p, tm)
    return tuple(v for pair in zip(out_p, out_s) for v in pair)
```

```python
import functools

import jax
import jax.numpy as jnp
from jax import lax
from jax.experimental import pallas as pl
from jax.experimental.pallas import tpu as pltpu

F32 = jnp.float32
BF16 = jnp.bfloat16

RMS_EPS = 1e-6
NEG_INF = -1e30
MASKED = 2.0 * NEG_INF
FORCE_BONUS = 1e4
LOG2E = 1.4426950408889634
CONV_W = 4
CHUNK = 64
SSD_HEADS, SSD_HEADDIM, SSD_STATE, SSD_GROUPS = 16, 64, 128, 2
SSD_INNER = SSD_HEADS * SSD_HEADDIM
SSD_CONV_DIM = SSD_INNER + 2 * SSD_GROUPS * SSD_STATE
GDN_QK_HEADS, GDN_V_HEADS, GDN_HEAD_DIM = 4, 8, 128
GDN_QK_DIM = GDN_QK_HEADS * GDN_HEAD_DIM
GDN_V_DIM = GDN_V_HEADS * GDN_HEAD_DIM
GDN_CONV_DIM = 2 * GDN_QK_DIM + GDN_V_DIM
NSA_HEADS, NSA_KV_HEADS, NSA_HEAD_DIM = 16, 4, 64
NSA_GROUP = NSA_HEADS // NSA_KV_HEADS
NSA_KV_COLS = NSA_KV_HEADS * NSA_HEAD_DIM
CMP_STRIDE, CMP_LEN, CMP_HIDDEN = 16, 32, 64
SLC_BLOCK, SLC_TOPN, WINDOW = 64, 16, 512
PAGE_SIZE = 128
LANES = 128
VMEM_LIMIT = 56 * 1024 * 1024


def _cparams(sem):
    return pltpu.CompilerParams(dimension_semantics=sem, vmem_limit_bytes=VMEM_LIMIT)


def _dot(a, b):
    return jnp.dot(a, b, preferred_element_type=F32)


def _dot_nt(a, b):
    return lax.dot_general(a, b, (((1,), (1,)), ((), ())), preferred_element_type=F32)


def _dot_tn(a, b):
    return lax.dot_general(a, b, (((0,), (0,)), ((), ())), preferred_element_type=F32)


def _split2(x):
    hi = x.astype(BF16)
    lo = (x - hi.astype(F32)).astype(BF16)
    return hi, lo


def _split3(x):
    hi = x.astype(BF16)
    r = x - hi.astype(F32)
    mid = r.astype(BF16)
    lo = (r - mid.astype(F32)).astype(BF16)
    return hi, mid, lo


def _mm3(a, b):
    ah, al = _split2(a)
    bh, bl = _split2(b)
    return _dot(ah, bh) + (_dot(ah, bl) + _dot(al, bh))


def _rms_unit(x):
    return x * lax.rsqrt(jnp.mean(x * x, axis=-1, keepdims=True) + RMS_EPS)


def _silu(x):
    return x * jax.nn.sigmoid(x)


def _softplus(x):
    return jnp.maximum(x, 0.0) + jnp.log1p(jnp.exp(-jnp.abs(x)))


def _iota2(shape, axis):
    return lax.broadcasted_iota(jnp.int32, shape, axis)


def _norm_proj_body(x_ref, g_ref, *refs, n_out, n_t):
    n_w = n_out + n_t
    w_refs, o_refs = refs[:n_w], refs[n_w:]
    hn = (_rms_unit(x_ref[...]) * g_ref[...]).astype(BF16)
    for w_ref, o_ref in zip(w_refs[:n_out], o_refs[:n_out]):
        n = w_ref.shape[1]
        for c0 in range(0, n, 512):
            c1 = min(n, c0 + 512)
            o_ref[:, c0:c1] = _dot(hn, w_ref[:, c0:c1])
    for w_ref, o_ref in zip(w_refs[n_out:], o_refs[n_out:]):
        n = w_ref.shape[0]
        for c0 in range(0, n, 512):
            c1 = min(n, c0 + 512)
            o_ref[0, c0:c1, :] = _dot_nt(w_ref[c0:c1, :], hn)


def norm_proj(x, g, ws, tm, wts=()):
    t, d = x.shape
    assert t % tm == 0
    in_specs = [pl.BlockSpec((tm, d), lambda i: (i, 0)), pl.BlockSpec((1, d), lambda i: (0, 0))]
    in_specs += [pl.BlockSpec(w.shape, lambda i: (0, 0)) for w in (*ws, *wts)]
    out_specs = [pl.BlockSpec((tm, w.shape[1]), lambda i: (i, 0)) for w in ws]
    out_specs += [pl.BlockSpec((1, w.shape[0], tm), lambda i: (i, 0, 0)) for w in wts]
    out_shape = [jax.ShapeDtypeStruct((t, w.shape[1]), F32) for w in ws]
    out_shape += [jax.ShapeDtypeStruct((t // tm, w.shape[0], tm), F32) for w in wts]
    return pl.pallas_call(
        functools.partial(_norm_proj_body, n_out=len(ws), n_t=len(wts)),
        grid=(t // tm,), in_specs=in_specs, out_specs=out_specs, out_shape=out_shape,
        compiler_params=_cparams(("parallel",)), name="norm_proj",
    )(x, g.reshape(1, d), *ws, *wts)


def _mix_mlp_body(x_ref, *refs, group_sizes, transposed, final, hc):
    n_y, n_g = sum(group_sizes), len(group_sizes)
    y_refs = refs[:n_y]
    wo_refs = refs[n_y:n_y + n_g]
    g_ref, wu_ref, wd_ref = refs[n_y + n_g:n_y + n_g + 3]
    rest = refs[n_y + n_g + 3:]
    gf_ref = rest[0] if final else None
    o_ref = rest[-1]
    x1 = x_ref[...]
    i = 0
    for gs, wo_ref in zip(group_sizes, wo_refs):
        rd = (lambda r: r[0]) if transposed else (lambda r: r[...])
        y = rd(y_refs[i])
        for r in y_refs[i + 1:i + gs]:
            y = y + rd(r)
        i += gs
        x1 = x1 + (_dot_tn if transposed else _dot)(y.astype(BF16), wo_ref[...])
    hn = (_rms_unit(x1) * g_ref[...]).astype(BF16)
    acc = x1
    hidden = wu_ref.shape[1]
    for c0 in range(0, hidden, hc):
        h = jnp.maximum(_dot(hn, wu_ref[:, c0:c0 + hc]), 0.0)
        acc = acc + _dot((h * h).astype(BF16), wd_ref[c0:c0 + hc, :])
    if final:
        acc = _rms_unit(acc) * gf_ref[...]
    o_ref[...] = acc


def mix_mlp(x, y_groups, w_outs, g, w_up, w_down, g_final, tm, hc=None, transposed=False):
    t, d = x.shape
    hc = hc or w_up.shape[1]
    assert t % tm == 0
    final = g_final is not None
    const = lambda i: (0, 0)
    ys = [y for grp in y_groups for y in grp]
    in_specs = [pl.BlockSpec((tm, d), lambda i: (i, 0))]
    if transposed:
        assert all(y.shape[0] == t // tm and y.shape[2] == tm for y in ys)
        in_specs += [pl.BlockSpec((1, y.shape[1], tm), lambda i: (i, 0, 0)) for y in ys]
    else:
        in_specs += [pl.BlockSpec((tm, y.shape[1]), lambda i: (i, 0)) for y in ys]
    in_specs += [pl.BlockSpec(w.shape, const) for w in w_outs]
    in_specs += [pl.BlockSpec((1, d), const), pl.BlockSpec(w_up.shape, const), pl.BlockSpec(w_down.shape, const)]
    args = [x, *ys, *w_outs, g.reshape(1, d), w_up, w_down]
    if final:
        in_specs.append(pl.BlockSpec((1, d), const))
        args.append(g_final.reshape(1, d))
    return pl.pallas_call(
        functools.partial(_mix_mlp_body, group_sizes=tuple(len(grp) for grp in y_groups), transposed=transposed,
                          final=final, hc=hc),
        grid=(t // tm,), in_specs=in_specs, out_specs=pl.BlockSpec((tm, d), lambda i: (i, 0)),
        out_shape=jax.ShapeDtypeStruct((t, d), F32),
        compiler_params=_cparams(("parallel",)), name="mix_mlp",
    )(*args)


def _causal_conv(x, xp_ref, w_ref, first, conv0_ref):
    c = x.shape[0]

    @pl.when(first)
    def _():
        xp_ref[5:8, :] = conv0_ref[0]

    xp_ref[8:8 + c, :] = x
    y = xp_ref[5:5 + c, :] * w_ref[0:1, :]
    for k in range(1, CONV_W):
        y = y + xp_ref[5 + k:5 + k + c, :] * w_ref[k:k + 1, :]
    xp_ref[5:8, :] = x[c - 3:c, :]
    return y


def _cumsum_pair(col, row):
    c = col.shape[0]
    ri, ci = _iota2((c, c), 0), _iota2((c, c), 1)
    tril = jnp.where(ri >= ci, 1.0, 0.0).astype(BF16)
    triu = jnp.where(ri <= ci, 1.0, 0.0).astype(BF16)
    ch, cm, cl = _split3(col)
    rh, rm, rl = _split3(row)
    cs = _dot(tril, ch) + (_dot(tril, cm) + _dot(tril, cl))
    cst = _dot(rh, triu) + (_dot(rm, triu) + _dot(rl, triu))
    return cs, cst


def _ssd_body(xbc_ref, z_ref, sm_ref, smt_ref, st0_ref, cv0_ref, cw_ref, cb_ref, dtb_r_ref, dtb_c_ref,
              al_r_ref, al_c_ref, dfull_ref, nw_ref, y_ref, st_ref, xp_ref, ybuf_ref, *, l_valid):
    cidx = pl.program_id(1)
    c = CHUNK
    n, p = SSD_STATE, SSD_HEADDIM

    @pl.when(cidx == 0)
    def _():
        st_ref[0] = st0_ref[0]

    conv = _causal_conv(xbc_ref[0], xp_ref, cw_ref, cidx == 0, cv0_ref) + cb_ref[...]
    act = _silu(conv)
    xs = act[:, :SSD_INNER]
    bm = act[:, SSD_INNER:SSD_INNER + SSD_GROUPS * n]
    cm = act[:, SSD_INNER + SSD_GROUPS * n:]

    valid_c = (cidx * c + _iota2((c, 1), 0)) < l_valid
    valid_r = (cidx * c + _iota2((1, c), 1)) < l_valid
    dt = jnp.where(valid_c, _softplus(sm_ref[0][:, 0:SSD_HEADS] + dtb_r_ref[...]), 0.0)
    dtt = jnp.where(valid_r, _softplus(smt_ref[0, 0][0:SSD_HEADS, :] + dtb_c_ref[...]), 0.0)
    cs, cst = _cumsum_pair(dt * (-jnp.exp(al_r_ref[...])), dtt * (-jnp.exp(al_c_ref[...])))
    cs_last = cs[c - 1:c, :]
    ecs = jnp.exp(cs)
    to_end = jnp.exp(cs_last - cs) * dt
    edec = jnp.exp(cs_last)
    causal = _iota2((c, c), 0) >= _iota2((c, c), 1)

    heads_per_group = SSD_HEADS // SSD_GROUPS
    heads = range(SSD_HEADS)
    bm_l = [bm[:, g * n:(g + 1) * n].astype(BF16) for g in range(SSD_GROUPS)]
    cm_l = [cm[:, g * n:(g + 1) * n].astype(BF16) for g in range(SSD_GROUPS)]
    cb_l = [_dot_nt(c_g, b_g) for c_g, b_g in zip(cm_l, bm_l)]
    grp = lambda h: h // heads_per_group
    x_l = [xs[:, h * p:(h + 1) * p] for h in heads]
    s_l = [st_ref[0, h] for h in heads]
    sc_l = []
    for h in heads:
        seg = cs[:, h:h + 1] - cst[h:h + 1, :]
        decay = jnp.exp(jnp.where(causal, seg, NEG_INF))
        sc_l.append((cb_l[grp(h)] * decay * dtt[h:h + 1, :]).astype(BF16))
    yd_l = [_dot(sc_l[h], x_l[h].astype(BF16)) for h in heads]
    yo_l = [_dot_nt(cm_l[grp(h)], s_l[h].astype(BF16)) * ecs[:, h:h + 1] for h in heads]
    for h in heads:
        ybuf_ref[:, h * p:(h + 1) * p] = yd_l[h] + yo_l[h]
    for h in heads:
        xw = (x_l[h] * to_end[:, h:h + 1]).astype(BF16)
        st_ref[0, h] = s_l[h] * edec[:, h:h + 1] + _dot_tn(xw, bm_l[grp(h)])

    y = ybuf_ref[...] + dfull_ref[...] * xs
    yz = y * _silu(z_ref[0])
    gw = SSD_INNER // SSD_GROUPS
    for g in range(SSD_GROUPS):
        seg = yz[:, g * gw:(g + 1) * gw]
        y_ref[0, :, g * gw:(g + 1) * gw] = _rms_unit(seg) * nw_ref[:, g * gw:(g + 1) * gw]


def ssd_mixer(xbc, z, small, small_t, state0, conv0, conv_w, conv_b, dt_bias, a_log, d_skip, norm_w, l_valid):
    b, lp, _ = xbc.shape
    nc = lp // CHUNK
    h = SSD_HEADS
    d_full = jnp.repeat(d_skip, SSD_HEADDIM).reshape(1, SSD_INNER)
    const2 = lambda i, j: (0, 0)
    tok = lambda w: pl.BlockSpec((1, CHUNK, w), lambda i, j: (i, j, 0))
    in_specs = [
        tok(SSD_CONV_DIM), tok(SSD_INNER), tok(LANES),
        pl.BlockSpec((1, 1, 32, CHUNK), lambda i, j: (i, j, 0, 0)),
        pl.BlockSpec((1, h, SSD_HEADDIM, SSD_STATE), lambda i, j: (i, 0, 0, 0)),
        pl.BlockSpec((1, CONV_W - 1, SSD_CONV_DIM), lambda i, j: (i, 0, 0)),
        pl.BlockSpec((CONV_W, SSD_CONV_DIM), const2), pl.BlockSpec((1, SSD_CONV_DIM), const2),
        pl.BlockSpec((1, h), const2), pl.BlockSpec((h, 1), const2),
        pl.BlockSpec((1, h), const2), pl.BlockSpec((h, 1), const2),
        pl.BlockSpec((1, SSD_INNER), const2), pl.BlockSpec((1, SSD_INNER), const2),
    ]
    out_specs = [tok(SSD_INNER), pl.BlockSpec((1, h, SSD_HEADDIM, SSD_STATE), lambda i, j: (i, 0, 0, 0))]
    out_shape = [jax.ShapeDtypeStruct((b, lp, SSD_INNER), F32),
                 jax.ShapeDtypeStruct((b, h, SSD_HEADDIM, SSD_STATE), F32)]
    return pl.pallas_call(
        functools.partial(_ssd_body, l_valid=l_valid),
        grid=(b, nc), in_specs=in_specs, out_specs=out_specs, out_shape=out_shape,
        scratch_shapes=[pltpu.VMEM((CHUNK + 8, SSD_CONV_DIM), F32), pltpu.VMEM((CHUNK, SSD_INNER), F32)],
        compiler_params=_cparams(("parallel", "arbitrary")), name="ssd_mixer",
    )(xbc, z, small, small_t, state0, conv0, conv_w, conv_b.reshape(1, -1),
      dt_bias.reshape(1, h), dt_bias.reshape(h, 1), a_log.reshape(1, h), a_log.reshape(h, 1),
      d_full, norm_w.reshape(1, SSD_INNER))


def _unit_lower_inverses(mats):
    c = mats[0].shape[0]
    ri, ci = _iota2((c, c), 0), _iota2((c, c), 1)
    eye = jnp.where(ri == ci, 1.0, 0.0)
    same_blk = lax.shift_right_logical(ri, 4) == lax.shift_right_logical(ci, 4)
    d = [jnp.where(same_blk, a, 0.0) for a in mats]
    nn = [a - x for a, x in zip(mats, d)]
    d2 = [_mm3(x, x) for x in d]
    d4 = [_mm3(x, x) for x in d2]
    pinv = [eye - x for x in d]
    pinv = [x + _mm3(x, y) for x, y in zip(pinv, d2)]
    d8 = [_mm3(x, x) for x in d4]
    pinv = [x + _mm3(x, y) for x, y in zip(pinv, d4)]
    pinv = [x + _mm3(x, y) for x, y in zip(pinv, d8)]
    e = [_mm3(x, y) for x, y in zip(pinv, nn)]
    e2 = [_mm3(x, x) for x in e]
    q = [eye - x for x in e]
    q = [x + _mm3(x, y) for x, y in zip(q, e2)]
    return [_mm3(x, y) for x, y in zip(q, pinv)]


def _l2norm(x):
    return x * lax.rsqrt(jnp.sum(x * x, axis=-1, keepdims=True) + 1e-6)


def _gdn_body(qkv_ref, z_ref, sm_ref, smt_ref, st0_ref, cv0_ref, cw_ref, dtb_r_ref, dtb_c_ref,
              al_r_ref, al_c_ref, nw_ref, o_ref, st_ref, xp_ref, *, l_valid):
    cidx = pl.program_id(1)
    c = CHUNK
    hd = GDN_HEAD_DIM
    nh = GDN_V_HEADS

    @pl.when(cidx == 0)
    def _():
        st_ref[0] = st0_ref[0]

    act = _silu(_causal_conv(qkv_ref[0], xp_ref, cw_ref, cidx == 0, cv0_ref))
    valid_c = (cidx * c + _iota2((c, 1), 0)) < l_valid
    valid_r = (cidx * c + _iota2((1, c), 1)) < l_valid
    sm = sm_ref[0]
    smt = smt_ref[0, 0]
    beta = jnp.where(valid_c, jax.nn.sigmoid(sm[:, 16:16 + nh]), 0.0)
    g_c = jnp.where(valid_c, -jnp.exp(al_r_ref[...]) * _softplus(sm[:, 24:24 + nh] + dtb_r_ref[...]), 0.0)
    g_r = jnp.where(valid_r, -jnp.exp(al_c_ref[...]) * _softplus(smt[24:24 + nh, :] + dtb_c_ref[...]), 0.0)
    cs, cst = _cumsum_pair(g_c, g_r)
    cs_last = cs[c - 1:c, :]
    ecs = jnp.exp(cs)
    e_end = jnp.exp(cs_last - cs)
    etot = jnp.exp(cs_last)
    ri, ci = _iota2((c, c), 0), _iota2((c, c), 1)
    causal = ri >= ci
    strict = ri > ci
    rep = GDN_V_HEADS // GDN_QK_HEADS

    eye = jnp.where(ri == ci, 1.0, 0.0)
    q_l, k_l, qk_l, decay_l, a_l = [], [], [], [], []
    for hq in range(GDN_QK_HEADS):
        q_h = _l2norm(act[:, hq * hd:(hq + 1) * hd]) * (hd ** -0.5)
        k_h = _l2norm(act[:, GDN_QK_DIM + hq * hd:GDN_QK_DIM + (hq + 1) * hd])
        k16 = k_h.astype(BF16)
        kk = _dot_nt(k16, k16)
        qk0 = _dot_nt(q_h.astype(BF16), k16)
        q_l.append(q_h)
        k_l.append(k_h)
        for r in range(rep):
            h = hq * rep + r
            seg = cs[:, h:h + 1] - cst[h:h + 1, :]
            decay = jnp.exp(jnp.where(causal, seg, NEG_INF))
            decay_l.append(decay)
            qk_l.append((qk0 * decay).astype(BF16))
            a_l.append(jnp.where(strict, kk * beta[:, h:h + 1] * decay, 0.0))
    tinv_l = _unit_lower_inverses(a_l)
    heads = range(nh)
    rhs_l = []
    for h in heads:
        b_h = beta[:, h:h + 1]
        v_h = act[:, 2 * GDN_QK_DIM + h * hd:2 * GDN_QK_DIM + (h + 1) * hd]
        rhs_l.append(jnp.concatenate([v_h * b_h, k_l[h // rep] * (b_h * ecs[:, h:h + 1])], axis=1))
    sol_l = [rhs_l[h] + _dot((tinv_l[h] - eye).astype(BF16), rhs_l[h].astype(BF16)) for h in heads]
    s_l = [st_ref[0, h] for h in heads]
    s16_l = [s.astype(BF16) for s in s_l]
    o1_l = [_dot((q_l[h // rep] * ecs[:, h:h + 1]).astype(BF16), s16_l[h]) for h in heads]
    v16_l = [(sol_l[h][:, :hd] - _dot(sol_l[h][:, hd:].astype(BF16), s16_l[h])).astype(BF16) for h in heads]
    o_l = [o1_l[h] + _dot(qk_l[h], v16_l[h]) for h in heads]
    for h in heads:
        k_dec = (k_l[h // rep] * e_end[:, h:h + 1]).astype(BF16)
        st_ref[0, h] = s_l[h] * etot[:, h:h + 1] + _dot_tn(k_dec, v16_l[h])
    for h in heads:
        o_ref[0, :, h * hd:(h + 1) * hd] = ((_rms_unit(o_l[h]) * nw_ref[...])
                                            * _silu(z_ref[0, :, h * hd:(h + 1) * hd]))


def gdn_mixer(qkv, z, small, small_t, state0, conv0, conv_w, dt_bias, a_log, norm_w, l_valid):
    b, lp, _ = qkv.shape
    nc = lp // CHUNK
    h = GDN_V_HEADS
    const2 = lambda i, j: (0, 0)
    tok = lambda w: pl.BlockSpec((1, CHUNK, w), lambda i, j: (i, j, 0))
    in_specs = [
        tok(GDN_CONV_DIM), tok(GDN_V_DIM), tok(LANES),
        pl.BlockSpec((1, 1, 32, CHUNK), lambda i, j: (i, j, 0, 0)),
        pl.BlockSpec((1, h, GDN_HEAD_DIM, GDN_HEAD_DIM), lambda i, j: (i, 0, 0, 0)),
        pl.BlockSpec((1, CONV_W - 1, GDN_CONV_DIM), lambda i, j: (i, 0, 0)),
        pl.BlockSpec((CONV_W, GDN_CONV_DIM), const2),
        pl.BlockSpec((1, h), const2), pl.BlockSpec((h, 1), const2),
        pl.BlockSpec((1, h), const2), pl.BlockSpec((h, 1), const2),
        pl.BlockSpec((1, GDN_HEAD_DIM), const2),
    ]
    out_specs = [tok(GDN_V_DIM), pl.BlockSpec((1, h, GDN_HEAD_DIM, GDN_HEAD_DIM), lambda i, j: (i, 0, 0, 0))]
    out_shape = [jax.ShapeDtypeStruct((b, lp, GDN_V_DIM), F32),
                 jax.ShapeDtypeStruct((b, h, GDN_HEAD_DIM, GDN_HEAD_DIM), F32)]
    return pl.pallas_call(
        functools.partial(_gdn_body, l_valid=l_valid),
        grid=(b, nc), in_specs=in_specs, out_specs=out_specs, out_shape=out_shape,
        scratch_shapes=[pltpu.VMEM((CHUNK + 8, GDN_CONV_DIM), F32)],
        compiler_params=_cparams(("parallel", "arbitrary")), name="gdn_mixer",
    )(qkv, z, small, small_t, state0, conv0, conv_w,
      dt_bias.reshape(1, h), dt_bias.reshape(h, 1), a_log.reshape(1, h), a_log.reshape(h, 1),
      norm_w.reshape(1, GDN_HEAD_DIM))


def _prep_ab_weights(w_in_ab):
    o = [0, SSD_INNER]
    o.append(o[-1] + SSD_CONV_DIM)
    o.append(o[-1] + SSD_HEADS)
    o.append(o[-1] + GDN_CONV_DIM)
    o.append(o[-1] + GDN_V_DIM)
    o.append(o[-1] + 2 * GDN_V_HEADS)
    w = w_in_ab.astype(BF16)
    small = jnp.concatenate([w[:, o[2]:o[3]], w[:, o[5]:o[6]]], axis=1)
    small = jnp.pad(small, ((0, 0), (0, LANES - small.shape[1])))
    return [w[:, o[0]:o[1]], w[:, o[1]:o[2]], w[:, o[3]:o[4]], w[:, o[4]:o[5]], small]


def _pad_time(a, lp):
    return jnp.pad(a, [(0, 0), (0, lp - a.shape[1])] + [(0, 0)] * (a.ndim - 2))


def layer0_mixer(x, ssd_state0, ssd_conv0, gdn_state0, gdn_conv0, p, tm):
    b, l, d = x.shape
    lp = -(-l // CHUNK) * CHUNK
    xp = _pad_time(x, lp).reshape(b * lp, d)
    z_ssd, xbc, qkv, z_gdn, small = norm_proj(xp, p["norm_mix"][0], p["w_ab"], min(tm, b * lp))
    nc = lp // CHUNK
    small_t = small[:, :32].reshape(b, nc, CHUNK, 32).transpose(0, 1, 3, 2)
    r3 = lambda a: a.reshape(b, lp, a.shape[-1])
    xbc, qkv, small = r3(xbc), r3(qkv), r3(small)
    y_ssd, ssd_state = ssd_mixer(xbc, r3(z_ssd), small, small_t, ssd_state0, ssd_conv0, p["ssd_conv_w"],
                                 p["ssd_conv_b"], p["ssd_dt_bias"], p["ssd_a_log"], p["ssd_d"], p["ssd_norm_w"], l)
    y_gdn, gdn_state = gdn_mixer(qkv, r3(z_gdn), small, small_t, gdn_state0, gdn_conv0, p["gdn_conv_w"],
                                 p["gdn_dt_bias"], p["gdn_a_log"], p["gdn_norm_w"], l)
    ssd_conv = jnp.concatenate([ssd_conv0, xbc[:, :l]], axis=1)[:, l:]
    gdn_conv = jnp.concatenate([gdn_conv0, qkv[:, :l]], axis=1)[:, l:]
    y_ssd = y_ssd[:, :l].reshape(b * l, SSD_INNER)
    y_gdn = y_gdn[:, :l].reshape(b * l, GDN_V_DIM)
    return y_ssd, y_gdn, ssd_state, ssd_conv, gdn_state, gdn_conv


def _nsa_lt_body(*refs, n_in):
    x_refs = refs[:n_in]
    wk_ref, wv_ref, o_ref = refs[n_in:]
    w = 2 * NSA_KV_COLS
    acc_k = acc_v = None
    for j in range(CMP_STRIDE):
        xk = jnp.concatenate([r[0, :, j * w:j * w + NSA_KV_COLS] for r in x_refs], axis=0).astype(BF16)
        xv = jnp.concatenate([r[0, :, j * w + NSA_KV_COLS:(j + 1) * w] for r in x_refs], axis=0).astype(BF16)
        dk, dv = _dot(xk, wk_ref[j]), _dot(xv, wv_ref[j])
        acc_k = dk if acc_k is None else acc_k + dk
        acc_v = dv if acc_v is None else acc_v + dv
    o_ref[0, :, 0:2 * NSA_KV_COLS] = acc_k
    o_ref[0, :, 2 * NSA_KV_COLS:] = acc_v


def _lt_weights(w1):
    w1r = w1.reshape(2, CMP_STRIDE, NSA_HEAD_DIM, CMP_HIDDEN)
    eye = jnp.eye(NSA_KV_HEADS, dtype=w1.dtype)
    lead = jnp.einsum("ab,jdh->jadbh", eye, w1r[0]).reshape(CMP_STRIDE, NSA_KV_COLS, NSA_KV_COLS)
    tail = jnp.einsum("ab,jdh->jadbh", eye, w1r[1]).reshape(CMP_STRIDE, NSA_KV_COLS, NSA_KV_COLS)
    return jnp.concatenate([lead, tail], axis=-1).astype(BF16)


def nsa_lt_dense(kv, wk, wv, tc):
    b, l, w = kv.shape
    nch = l // CMP_STRIDE
    x = kv[:, :nch * CMP_STRIDE].reshape(b, nch, CMP_STRIDE * w)
    tc = min(tc, nch)
    assert nch % tc == 0
    const3 = lambda i, j: (0, 0, 0)
    return pl.pallas_call(
        functools.partial(_nsa_lt_body, n_in=1), grid=(b, nch // tc),
        in_specs=[pl.BlockSpec((1, tc, CMP_STRIDE * w), lambda i, j: (i, j, 0)),
                  pl.BlockSpec(wk.shape, const3), pl.BlockSpec(wv.shape, const3)],
        out_specs=pl.BlockSpec((1, tc, 4 * NSA_KV_COLS), lambda i, j: (i, j, 0)),
        out_shape=jax.ShapeDtypeStruct((b, nch, 4 * NSA_KV_COLS), F32),
        compiler_params=_cparams(("parallel", "parallel")), name="nsa_lt_dense",
    )(x, wk, wv)


def _nsa_lt_paged_body(pt_ref, *refs, n_pg):
    page_refs = refs[:n_pg]
    wk_ref, wv_ref, o_ref, xa_ref, xb_ref = refs[n_pg:]
    gi = pl.program_id(1)
    page = page_refs[0].shape[2]
    n_col = page_refs[0].shape[1] // LANES
    rows = n_pg * page // CMP_STRIDE
    half = n_col // 2

    @pl.when(gi == 0)
    def _():
        xb_ref[...] = jnp.zeros(xb_ref.shape, F32)

    chunks = page // CMP_STRIDE
    ti, ci = _iota2((page, page), 0), _iota2((page, page), 1)
    perm = jnp.where(ci == jnp.bitwise_and(ti, CMP_STRIDE - 1) * chunks + lax.shift_right_logical(ti, 4),
                     1.0, 0.0).astype(BF16)

    def work(fill_ref, read_ref):
        for i, r in enumerate(page_refs):
            xp = _dot(r[0].astype(BF16), perm)
            for c in range(n_col):
                fill_ref[c, i * page:(i + 1) * page, :] = xp[c * LANES:(c + 1) * LANES, :].T
        acc_k = acc_v = None
        for j in range(CMP_STRIDE):
            xj = [jnp.concatenate([read_ref[c, i * page + j * chunks:i * page + (j + 1) * chunks, :]
                                   for i in range(n_pg)], axis=0) for c in range(n_col)]
            dk = _dot(jnp.concatenate(xj[:half], axis=1).astype(BF16), wk_ref[j])
            dv = _dot(jnp.concatenate(xj[half:], axis=1).astype(BF16), wv_ref[j])
            acc_k = dk if acc_k is None else acc_k + dk
            acc_v = dv if acc_v is None else acc_v + dv
        o_ref[0, :, 0:2 * NSA_KV_COLS] = acc_k
        o_ref[0, :, 2 * NSA_KV_COLS:] = acc_v

    @pl.when(jnp.bitwise_and(gi, 1) == 0)
    def _():
        work(xa_ref, xb_ref)

    @pl.when(jnp.bitwise_and(gi, 1) == 1)
    def _():
        work(xb_ref, xa_ref)


def nsa_lt_paged(cache_t, page_table, wk, wv, n_pg):
    npool, w, page = cache_t.shape
    b, n_pages = page_table.shape
    rows = page // CMP_STRIDE
    n_grp = n_pages // n_pg
    assert n_pages % n_pg == 0
    page_spec = lambda i: pl.BlockSpec(
        (1, w, page), lambda bi, gi, pt: (pt[bi, jnp.minimum(gi, n_grp - 1) * n_pg + i], 0, 0))
    const3 = lambda bi, gi, pt: (0, 0, 0)
    buf = pltpu.VMEM((w // LANES, n_pg * page, LANES), F32)
    grid_spec = pltpu.PrefetchScalarGridSpec(
        num_scalar_prefetch=1, grid=(b, n_grp + 1),
        in_specs=[page_spec(i) for i in range(n_pg)] + [pl.BlockSpec(wk.shape, const3), pl.BlockSpec(wv.shape, const3)],
        out_specs=pl.BlockSpec((1, n_pg * rows, 4 * NSA_KV_COLS), lambda bi, gi, pt: (bi, jnp.maximum(gi - 1, 0), 0)),
        scratch_shapes=[buf, buf])
    return pl.pallas_call(
        functools.partial(_nsa_lt_paged_body, n_pg=n_pg), grid_spec=grid_spec,
        out_shape=jax.ShapeDtypeStruct((b, n_pages * rows, 4 * NSA_KV_COLS), F32),
        compiler_params=_cparams(("parallel", "arbitrary")), name="nsa_lt_paged",
    )(page_table, *([cache_t] * n_pg), wk, wv)


def _cmp_kv_body(lt_ref, pek_ref, w1k_ref, b1k_ref, w2k_ref, pev_ref, w1v_ref, b1v_ref, w2v_ref, w2vt_ref,
                 kc_ref, vc_ref, vct_ref, sh_ref):
    nch = lt_ref.shape[1]
    kvc = NSA_KV_COLS

    def branch(off, pe_ref, w1_ref, b1_ref, w2_ref, out_ref):
        pe = jnp.broadcast_to(pe_ref[...], (8, pe_ref.shape[1]))
        pe_term = _mm3(pe, w1_ref[...])[0:1, :] + b1_ref[...]
        pe4 = jnp.concatenate([pe_term] * NSA_KV_HEADS, axis=1)
        sh_ref[0:nch, :] = lt_ref[0, :, off + kvc:off + 2 * kvc]
        sh_ref[nch:nch + 8, :] = jnp.zeros((8, kvc), F32)
        hid = _silu(lt_ref[0, :, off:off + kvc] + sh_ref[1:nch + 1, :] + pe4).astype(BF16)
        out_ref[0] = _dot(hid, w2_ref[...])
        return hid

    branch(0, pek_ref, w1k_ref, b1k_ref, w2k_ref, kc_ref)
    hid_v = branch(2 * kvc, pev_ref, w1v_ref, b1v_ref, w2v_ref, vc_ref)
    vct_ref[0] = _dot_nt(w2vt_ref[...], hid_v)


def cmp_kv(lt, p):
    b, nch, _ = lt.shape
    kvc = NSA_KV_COLS
    const2 = lambda i: (0, 0)
    args, specs = [], []
    for s in ("k", "v"):
        pe = p["cmp_pe_" + s].reshape(1, -1)
        w1 = p["cmp_w1_" + s]
        b1 = p["cmp_b1_" + s].reshape(1, -1)
        w2 = jnp.kron(jnp.eye(NSA_KV_HEADS, dtype=F32), p["cmp_w2_" + s]).astype(BF16)
        for a in (pe, w1, b1, w2):
            args.append(a)
            specs.append(pl.BlockSpec(a.shape, const2))
    args.append(jnp.kron(jnp.eye(NSA_KV_HEADS, dtype=F32), p["cmp_w2_v"].T).astype(BF16))
    specs.append(pl.BlockSpec(args[-1].shape, const2))
    out_spec = pl.BlockSpec((1, nch, kvc), lambda i: (i, 0, 0))
    return pl.pallas_call(
        _cmp_kv_body, grid=(b,),
        in_specs=[pl.BlockSpec((1, nch, 4 * kvc), lambda i: (i, 0, 0))] + specs,
        out_specs=[out_spec, out_spec, pl.BlockSpec((1, kvc, nch), lambda i: (i, 0, 0))],
        out_shape=[jax.ShapeDtypeStruct((b, nch, kvc), F32)] * 2 + [jax.ShapeDtypeStruct((b, kvc, nch), F32)],
        scratch_shapes=[pltpu.VMEM((nch + 8, kvc), F32)],
        compiler_params=_cparams(("parallel",)), name="cmp_kv",
    )(lt, *args)


def _cmp_attn_body(q_ref, gate_ref, kc_ref, vc_ref, oc_ref, sel_ref, *, tq, pos0, n_slc, nbp):
    qi = pl.program_id(1)
    ncp = kc_ref.shape[1]
    hd = NSA_HEAD_DIM
    t_col = pos0 + qi * tq + _iota2((tq, 1), 0)
    cmask = (_iota2((1, ncp), 1) * CMP_STRIDE + (CMP_LEN - 1)) <= t_col
    kc = kc_ref[0].astype(BF16)
    vc = vc_ref[0].astype(BF16)
    gate = jax.nn.sigmoid(gate_ref[0])
    nn = _iota2((ncp, nbp), 0) * CMP_STRIDE
    ss = _iota2((ncp, nbp), 1) * SLC_BLOCK
    overlap = jnp.where((nn < ss + SLC_BLOCK) & (nn + CMP_LEN > ss), 1.0, 0.0).astype(BF16)
    blk = _iota2((1, nbp), 1)
    blk_f = blk.astype(F32)
    cur = lax.shift_right_logical(t_col, 6)
    forced = (blk == 0) | (blk == cur) | (blk == cur - 1)
    bonus = jnp.where(forced, FORCE_BONUS, 0.0)
    causal_blk = blk * SLC_BLOCK <= t_col
    n_pick = min(SLC_TOPN, n_slc)

    scores = []
    for k in range(NSA_KV_HEADS):
        kc_k = kc[:, k * hd:(k + 1) * hd]
        vc_k = vc[:, k * hd:(k + 1) * hd]
        hs = [k * NSA_GROUP + g for g in range(NSA_GROUP)]
        q_l = [(q_ref[0, :, h * hd:(h + 1) * hd] * (hd ** -0.5)).astype(BF16) for h in hs]
        s_l = [jnp.where(cmask, _dot_nt(q_h, kc_k), NEG_INF) for q_h in q_l]
        e_l = [jnp.where(cmask, jnp.exp(s - jnp.max(s, axis=-1, keepdims=True)), 0.0) for s in s_l]
        den_l = [jnp.sum(e, axis=-1, keepdims=True) for e in e_l]
        p_l = [e * (1.0 / jnp.where(den > 0.0, den, 1.0)) for e, den in zip(e_l, den_l)]
        o_l = [_dot(p.astype(BF16), vc_k) for p in p_l]
        for h, o in zip(hs, o_l):
            oc_ref[0, :, h * hd:(h + 1) * hd] = o * gate[:, h:h + 1]
        ph, plo = _split2((p_l[0] + p_l[1]) + (p_l[2] + p_l[3]))
        imp = _dot(ph, overlap) + _dot(plo, overlap)
        score = jnp.where(causal_blk, imp + bonus, NEG_INF)
        scores.append(jnp.where(blk < n_slc, score, -jnp.inf))

    def pick_one(_, carry):
        work, sel = carry
        m = jnp.max(work, axis=-1, keepdims=True)
        idx = jnp.min(jnp.where(work == m, blk_f, 1e9), axis=-1, keepdims=True)
        hit = blk_f == idx
        return jnp.where(hit, -jnp.inf, work), jnp.where(hit, 1.0, sel)

    work0 = jnp.concatenate(scores, axis=0)
    _, sel = lax.fori_loop(0, n_pick, pick_one, (work0, jnp.zeros(work0.shape, F32)))
    for k in range(NSA_KV_HEADS):
        sel_ref[0, :, k * nbp:(k + 1) * nbp] = sel[k * tq:(k + 1) * tq]


def _nbp(n_slc):
    return 64 if n_slc <= 64 else -(-n_slc // LANES) * LANES


def cmp_attn_select(q, gate, kc, vc, tq, pos0, n_slc):
    b, l, dq = q.shape
    ncp = kc.shape[1]
    nbp = _nbp(n_slc)
    assert l % tq == 0
    tok = lambda w: pl.BlockSpec((1, tq, w), lambda i, j: (i, j, 0))
    whole = pl.BlockSpec((1, ncp, NSA_KV_COLS), lambda i, j: (i, 0, 0))
    return pl.pallas_call(
        functools.partial(_cmp_attn_body, tq=tq, pos0=pos0, n_slc=n_slc, nbp=nbp),
        grid=(b, l // tq), in_specs=[tok(dq), tok(LANES), whole, whole],
        out_specs=[tok(dq), tok(NSA_KV_HEADS * nbp)],
        out_shape=[jax.ShapeDtypeStruct((b, l, dq), F32), jax.ShapeDtypeStruct((b, l, NSA_KV_HEADS * nbp), F32)],
        compiler_params=_cparams(("parallel", "parallel")), name="cmp_attn_select",
    )(q, gate, kc, vc)


def _nsa_attn_body(q_ref, gate_ref, kv_ref, *rest, mode, tq, tk, qpos0, kpos0, gate_off, nbp):
    if mode == "slc":
        sel_ref, o_ref, qs_ref, m_ref, l_ref, acc_ref = rest
    else:
        o_ref, qs_ref, m_ref, l_ref, acc_ref = rest
    qi = pl.program_id(1)
    hd = NSA_HEAD_DIM
    nk = kv_ref.shape[1] // tk
    qmin = qpos0 + qi * tq
    t_col = qmin + _iota2((tq, 1), 0)
    for k in range(NSA_KV_HEADS):
        qs_ref[k] = jnp.concatenate(
            [q_ref[0, :, (k * NSA_GROUP + g) * hd:(k * NSA_GROUP + g + 1) * hd] * (hd ** -0.5)
             for g in range(NSA_GROUP)], axis=0)
    m_ref[...] = jnp.full(m_ref.shape, NEG_INF, F32)
    l_ref[...] = jnp.zeros(l_ref.shape, F32)
    acc_ref[...] = jnp.zeros(acc_ref.shape, F32)
    hi = jnp.minimum(nk - 1, (qmin + tq - 1 - kpos0) // tk)
    if mode == "slc":
        lo = 0
    else:
        lo = jnp.maximum(qmin - (WINDOW - 1) - kpos0, 0) // tk

    def step(ki, carry):
        k0 = pl.multiple_of(ki * tk, tk)
        kv = kv_ref[0, pl.ds(k0, tk), :]
        kpos = kpos0 + k0 + _iota2((1, tk), 1)
        if mode == "slc":
            expand = jnp.where(_iota2((nbp, tk), 0) == lax.shift_right_logical(k0 + _iota2((nbp, tk), 1), 6),
                               1.0, 0.0).astype(BF16)
            base = kpos <= t_col
        else:
            dist = t_col - kpos
            base = (dist >= 0) & (dist < WINDOW)
        for k in range(NSA_KV_HEADS):
            kk = kv[:, k * hd:(k + 1) * hd].astype(BF16)
            vv = kv[:, NSA_KV_COLS + k * hd:NSA_KV_COLS + (k + 1) * hd].astype(BF16)
            s = _dot_nt(qs_ref[k].astype(BF16), kk)
            if mode == "slc":
                mask = base & (_dot(sel_ref[0, :, k * nbp:(k + 1) * nbp].astype(BF16), expand) > 0.5)
            else:
                mask = base
            for g in range(NSA_GROUP):
                r = k * NSA_GROUP + g
                s_g = jnp.where(mask, s[g * tq:(g + 1) * tq], NEG_INF)
                m_old = m_ref[r]
                m_new = jnp.maximum(m_old, jnp.max(s_g, axis=-1, keepdims=True))
                alpha = jnp.exp(m_old - m_new)
                p = jnp.where(mask, jnp.exp(s_g - m_new), 0.0)
                l_ref[r] = alpha * l_ref[r] + jnp.sum(p, axis=-1, keepdims=True)
                acc_ref[r] = alpha * acc_ref[r] + _dot(p.astype(BF16), vv)
                m_ref[r] = m_new
        return carry

    lax.fori_loop(lo, hi + 1, step, 0)
    gate = jax.nn.sigmoid(gate_ref[0])
    for r in range(NSA_HEADS):
        l = l_ref[r]
        o = acc_ref[r] / jnp.where(l > 0.0, l, 1.0)
        o_ref[0, :, r * hd:(r + 1) * hd] = o * gate[:, gate_off + r:gate_off + r + 1]


def nsa_attn(mode, q, gate, kv, sel, tq, tk, qpos0, kpos0):
    b, l, dq = q.shape
    nkeys = kv.shape[1]
    assert l % tq == 0 and nkeys % tk == 0
    tok = lambda w: pl.BlockSpec((1, tq, w), lambda i, j: (i, j, 0))
    in_specs = [tok(dq), tok(LANES), pl.BlockSpec((1, nkeys, kv.shape[2]), lambda i, j: (i, 0, 0))]
    args = [q, gate, kv]
    nbp = 0
    if mode == "slc":
        nbp = sel.shape[2] // NSA_KV_HEADS
        in_specs.append(tok(sel.shape[2]))
        args.append(sel)
    return pl.pallas_call(
        functools.partial(_nsa_attn_body, mode=mode, tq=tq, tk=tk, qpos0=qpos0, kpos0=kpos0,
                          gate_off=NSA_HEADS * (1 if mode == "slc" else 2), nbp=nbp),
        grid=(b, l // tq), in_specs=in_specs, out_specs=tok(dq),
        out_shape=jax.ShapeDtypeStruct((b, l, dq), F32),
        scratch_shapes=[pltpu.VMEM((NSA_KV_HEADS, NSA_GROUP * tq, NSA_HEAD_DIM), F32),
                        pltpu.VMEM((NSA_HEADS, tq, 1), F32), pltpu.VMEM((NSA_HEADS, tq, 1), F32),
                        pltpu.VMEM((NSA_HEADS, tq, NSA_HEAD_DIM), F32)],
        compiler_params=_cparams(("parallel", "parallel")), name="nsa_attn_" + mode,
    )(*args)


def _slc_paged_body(pt_ref, q_ref, gate_ref, selgrp_ref, selnew_ref, kvnew_ref, *rest, n_pg, tq, l_new):
    page_refs = rest[:n_pg]
    o_ref, qbd_ref, m_ref, l_ref, acc_ref = rest[n_pg:]
    gi = pl.program_id(1)
    hd = NSA_HEAD_DIM
    rows = NSA_HEADS * tq

    @pl.when(gi == 0)
    def _():
        qbd_ref[...] = jnp.zeros(qbd_ref.shape, F32)
        for h in range(NSA_HEADS):
            k = h // NSA_GROUP
            qbd_ref[h * tq:(h + 1) * tq, k * hd:(k + 1) * hd] = q_ref[0, :, h * hd:(h + 1) * hd] * (hd ** -0.5)
        m_ref[...] = jnp.full(m_ref.shape, NEG_INF, F32)
        l_ref[...] = jnp.zeros(l_ref.shape, F32)
        acc_ref[...] = jnp.zeros(acc_ref.shape, F32)

    qbd = qbd_ref[...].astype(BF16)

    def update(s, vv, v_transposed):
        m_old = m_ref[...]
        m_new = jnp.maximum(m_old, jnp.max(s, axis=-1, keepdims=True))
        alpha = jnp.exp(m_old - m_new)
        p = jnp.exp(s - m_new)
        l_ref[...] = alpha * l_ref[...] + jnp.sum(p, axis=-1, keepdims=True)
        acc_ref[...] = alpha * acc_ref[...] + (_dot_nt if v_transposed else _dot)(p.astype(BF16), vv)
        m_ref[...] = m_new

    page = page_refs[0].shape[2]
    wide = n_pg * page
    nblk = wide // SLC_BLOCK
    kt = jnp.concatenate([r[0, 0:NSA_KV_COLS, :] for r in page_refs], axis=1).astype(BF16)
    vt = jnp.concatenate([r[0, NSA_KV_COLS:, :] for r in page_refs], axis=1).astype(BF16)
    expand = jnp.where(_iota2((nblk, wide), 0) == lax.shift_right_logical(_iota2((nblk, wide), 1), 6),
                       1.0, 0.0).astype(BF16)
    selb = ((1.0 - selgrp_ref[0, 0]) * MASKED).astype(BF16)
    update(_dot(qbd, kt) + _dot(selb, expand), vt, True)

    @pl.when(gi == pl.num_programs(1) - 1)
    def _():
        kvn = kvnew_ref[0]
        nr = kvn.shape[0]
        j = _iota2((rows, nr), 1)
        iq = jnp.bitwise_and(_iota2((rows, nr), 0), tq - 1)
        mask = (selnew_ref[0] > 0.5) & (j <= iq) & (j < l_new)
        s_new = _dot_nt(qbd, kvn[:, :NSA_KV_COLS].astype(BF16)) + jnp.where(mask, 0.0, MASKED)
        update(s_new, kvn[:, NSA_KV_COLS:].astype(BF16), False)
        gate = jax.nn.sigmoid(gate_ref[0])
        l = l_ref[...]
        o = acc_ref[...] / jnp.where(l > 0.0, l, 1.0)
        for h in range(NSA_HEADS):
            k = h // NSA_GROUP
            o_ref[0, :, h * hd:(h + 1) * hd] = (o[h * tq:(h + 1) * tq, k * hd:(k + 1) * hd]
                                                * gate[:, NSA_HEADS + h:NSA_HEADS + h + 1])


def slc_paged(q, gate, selrows, kv_new, cache_t, page_table, n_pg, l_new, p0):
    b, tq, dq = q.shape
    npool, w, page = cache_t.shape
    n_pages = page_table.shape[1]
    rows = NSA_HEADS * tq
    n_grp = n_pages // n_pg
    nblk = n_pg * page // SLC_BLOCK
    assert n_pages % n_pg == 0 and p0 % SLC_BLOCK == 0 and l_new <= SLC_BLOCK and p0 == n_pages * page
    assert tq & (tq - 1) == 0
    blk_new = p0 // SLC_BLOCK
    sel_grp = selrows[:, :, :blk_new].reshape(b, rows, n_grp, nblk).transpose(0, 2, 1, 3)
    sel_new = selrows[:, :, blk_new:blk_new + 1]
    per_b = lambda shape: pl.BlockSpec((1,) + shape, lambda bi, gi, pt: (bi, 0, 0))
    page_spec = lambda i: pl.BlockSpec((1, w, page), lambda bi, gi, pt: (pt[bi, gi * n_pg + i], 0, 0))
    grid_spec = pltpu.PrefetchScalarGridSpec(
        num_scalar_prefetch=1, grid=(b, n_grp),
        in_specs=[per_b((tq, dq)), per_b((tq, LANES)),
                  pl.BlockSpec((1, 1, rows, nblk), lambda bi, gi, pt: (bi, gi, 0, 0)),
                  per_b((rows, 1)), per_b((kv_new.shape[1], w))]
        + [page_spec(i) for i in range(n_pg)],
        out_specs=per_b((tq, dq)),
        scratch_shapes=[pltpu.VMEM((rows, NSA_KV_COLS), F32), pltpu.VMEM((rows, 1), F32),
                        pltpu.VMEM((rows, 1), F32), pltpu.VMEM((rows, NSA_KV_COLS), F32)])
    return pl.pallas_call(
        functools.partial(_slc_paged_body, n_pg=n_pg, tq=tq, l_new=l_new),
        grid_spec=grid_spec, out_shape=jax.ShapeDtypeStruct((b, tq, dq), F32),
        compiler_params=_cparams(("parallel", "arbitrary")), name="slc_paged",
    )(page_table, q, gate, sel_grp, sel_new, kv_new, *([cache_t] * n_pg))


def _cmp_attn_t_body(qt_ref, gt_ref, kc_ref, vct_ref, oct_ref, selt_ref, *, tq, pos0, n_slc, nbp):
    qi = pl.program_id(1)
    ncp = kc_ref.shape[1]
    hd = NSA_HEAD_DIM
    t_row = pos0 + qi * tq + _iota2((1, tq), 1)
    cbias = jnp.where((_iota2((ncp, 1), 0) * CMP_STRIDE + (CMP_LEN - 1)) <= t_row, 0.0, MASKED)
    kc = kc_ref[0].astype(BF16)
    vct = vct_ref[0].astype(BF16)
    gate = jax.nn.sigmoid(gt_ref[0, 0:NSA_HEADS, :])
    ss = _iota2((nbp, ncp), 0) * SLC_BLOCK
    nn = _iota2((nbp, ncp), 1) * CMP_STRIDE
    overlap_t = jnp.where((nn < ss + SLC_BLOCK) & (nn + CMP_LEN > ss), 1.0, 0.0).astype(BF16)
    blk = _iota2((nbp, 1), 0)
    blk_f = blk.astype(F32)
    cur = lax.shift_right_logical(t_row, 6)
    forced = (blk == 0) | (blk == cur) | (blk == cur - 1)
    bonus = jnp.where(forced, FORCE_BONUS, 0.0)
    causal_blk = blk * SLC_BLOCK <= t_row
    n_pick = min(SLC_TOPN, n_slc)

    scores = []
    for k in range(NSA_KV_HEADS):
        kc_k = kc[:, k * hd:(k + 1) * hd]
        vct_k = vct[k * hd:(k + 1) * hd, :]
        hs = [k * NSA_GROUP + g for g in range(NSA_GROUP)]
        q_l = [(qt_ref[0, h * hd:(h + 1) * hd, :] * (hd ** -0.5 * LOG2E)).astype(BF16) for h in hs]
        s_l = [_dot(kc_k, q_h) + cbias for q_h in q_l]
        e_l = [jnp.exp2(s - jnp.maximum(jnp.max(s, axis=0, keepdims=True), NEG_INF)) for s in s_l]
        den_l = [jnp.sum(e, axis=0, keepdims=True) for e in e_l]
        p_l = [e * (1.0 / jnp.where(den > 0.0, den, 1.0)) for e, den in zip(e_l, den_l)]
        o_l = [_dot(vct_k, p.astype(BF16)) for p in p_l]
        for h, o in zip(hs, o_l):
            oct_ref[0, h * hd:(h + 1) * hd, :] = o * gate[h:h + 1, :]
        psum = (p_l[0] + p_l[1]) + (p_l[2] + p_l[3])
        ph, plo = _split2(psum)
        imp = _dot(overlap_t, ph) + _dot(overlap_t, plo)
        score = jnp.where(causal_blk, imp + bonus, NEG_INF)
        scores.append(jnp.where(blk < n_slc, score, -jnp.inf))

    def pick_one(_, carry):
        work, sel = carry
        m = jnp.max(work, axis=0, keepdims=True)
        idx = jnp.min(jnp.where(work == m, blk_f, 1e9), axis=0, keepdims=True)
        hit = blk_f == idx
        return jnp.where(hit, -jnp.inf, work), jnp.where(hit, 1.0, sel)

    work0 = jnp.concatenate(scores, axis=1)
    _, sel = lax.fori_loop(0, n_pick, pick_one, (work0, jnp.zeros(work0.shape, F32)))
    for k in range(NSA_KV_HEADS):
        selt_ref[0, k * nbp:(k + 1) * nbp, :] = sel[:, k * tq:(k + 1) * tq]


def cmp_attn_select_t(qt, gate_t, kc, vct, b, tq, pos0, n_slc):
    nt, dq, _ = qt.shape
    nq = nt // b
    ncp = kc.shape[1]
    nbp = _nbp(n_slc)
    tile = lambda rows: pl.BlockSpec((1, rows, tq), lambda i, j: (i * nq + j, 0, 0))
    return pl.pallas_call(
        functools.partial(_cmp_attn_t_body, tq=tq, pos0=pos0, n_slc=n_slc, nbp=nbp),
        grid=(b, nq),
        in_specs=[tile(dq), tile(LANES), pl.BlockSpec((1, ncp, NSA_KV_COLS), lambda i, j: (i, 0, 0)),
                  pl.BlockSpec((1, NSA_KV_COLS, ncp), lambda i, j: (i, 0, 0))],
        out_specs=[tile(dq), tile(NSA_KV_HEADS * nbp)],
        out_shape=[jax.ShapeDtypeStruct((nt, dq, tq), F32), jax.ShapeDtypeStruct((nt, NSA_KV_HEADS * nbp, tq), F32)],
        compiler_params=_cparams(("parallel", "parallel")), name="cmp_attn_select_t",
    )(qt, gate_t, kc, vct)


def _nsa_attn_t_body(qt_ref, gt_ref, kv_ref, vt_ref, *rest, mode, tq, tk, gate_off, nbp):
    if mode == "slc":
        selt_ref, o_ref, qs_ref, m_ref, l_ref, acc_ref = rest
    else:
        o_ref, qs_ref, m_ref, l_ref, acc_ref = rest
    qi = pl.program_id(1)
    hd = NSA_HEAD_DIM
    nk = kv_ref.shape[1] // tk
    qmin = qi * tq
    t_row = qmin + _iota2((1, tq), 1)
    for r in range(NSA_HEADS):
        qs_ref[r, 0:hd, :] = (qt_ref[0, r * hd:(r + 1) * hd, :] * (hd ** -0.5 * LOG2E)).astype(BF16)
        if mode == "slc":
            k = r // NSA_GROUP
            qs_ref[r, hd:hd + nbp, :] = ((1.0 - selt_ref[0, k * nbp:(k + 1) * nbp, :]) * MASKED).astype(BF16)
    m_ref[...] = jnp.full(m_ref.shape, NEG_INF, F32)
    l_ref[...] = jnp.zeros(l_ref.shape, F32)
    acc_ref[...] = jnp.zeros(acc_ref.shape, F32)
    hi = jnp.minimum(nk - 1, (qmin + tq - 1) // tk)

    heads_per_stage = 2 * NSA_GROUP

    def tile(ki, positional):
        k0 = pl.multiple_of(ki * tk, tk)
        krows = kv_ref[0, pl.ds(k0, tk), 0:NSA_KV_COLS].astype(BF16)
        vt = vt_ref[ki].astype(BF16)
        kpos = k0 + _iota2((tk, 1), 0)
        if mode == "slc":
            expand_t = jnp.where(_iota2((tk, nbp), 1) == lax.shift_right_logical(k0 + _iota2((tk, nbp), 0), 6),
                                 1.0, 0.0).astype(BF16)
            cbias = jnp.where(kpos <= t_row, 0.0, MASKED) if positional else None
        else:
            dist = t_row - kpos
            cbias = jnp.where((dist >= 0) & (dist < WINDOW), 0.0, MASKED)
        k_l, vt_l = [], []
        for k in range(NSA_KV_HEADS):
            k_k = krows[:, k * hd:(k + 1) * hd]
            k_l.append(jnp.concatenate([k_k, expand_t], axis=1) if mode == "slc" else k_k)
            vt_l.append(vt[k * hd:(k + 1) * hd, :])
        for r0 in range(0, NSA_HEADS, heads_per_stage):
            rs = list(range(r0, r0 + heads_per_stage))
            s_l = [_dot(k_l[r // NSA_GROUP], qs_ref[r]) for r in rs]
            if cbias is not None:
                s_l = [s + cbias for s in s_l]
            m_old = [m_ref[r] for r in rs]
            m_new = [jnp.maximum(mo, jnp.max(s, axis=0, keepdims=True)) for mo, s in zip(m_old, s_l)]
            p_l = [jnp.exp2(s - mn) for s, mn in zip(s_l, m_new)]
            pv_l = [_dot(vt_l[r // NSA_GROUP], p.astype(BF16)) for r, p in zip(rs, p_l)]
            for r, mo, mn, p, pv in zip(rs, m_old, m_new, p_l, pv_l):
                alpha = jnp.exp2(mo - mn)
                l_ref[r] = alpha * l_ref[r] + jnp.sum(p, axis=0, keepdims=True)
                acc_ref[r] = alpha * acc_ref[r] + pv
                m_ref[r] = mn

    def step(ki, carry):
        tile(ki, mode != "slc")
        return carry

    if mode == "slc":
        lax.fori_loop(0, hi, step, 0)
        tile(hi, True)
    else:
        lax.fori_loop(jnp.maximum(qmin - (WINDOW - 1), 0) // tk, hi + 1, step, 0)
    gate = jax.nn.sigmoid(gt_ref[0, gate_off:gate_off + NSA_HEADS, :])
    for r in range(NSA_HEADS):
        l = l_ref[r]
        o_ref[0, r * hd:(r + 1) * hd, :] = acc_ref[r] * (gate[r:r + 1, :] / jnp.where(l > 0.0, l, 1.0))


def nsa_attn_t(mode, qt, gate_t, kv, vt, selt, b, tq, tk):
    nt, dq, _ = qt.shape
    nq = nt // b
    l = kv.shape[1]
    nk = l // tk
    assert l % tk == 0 and vt.shape == (b * nk, NSA_KV_COLS, tk) and (mode != "slc" or tq == tk)
    tile = lambda rows: pl.BlockSpec((1, rows, tq), lambda i, j: (i * nq + j, 0, 0))
    in_specs = [tile(dq), tile(LANES), pl.BlockSpec((1, l, kv.shape[2]), lambda i, j: (i, 0, 0)),
                pl.BlockSpec((nk, NSA_KV_COLS, tk), lambda i, j: (i, 0, 0))]
    args = [qt, gate_t, kv, vt]
    nbp = 0
    if mode == "slc":
        nbp = selt.shape[1] // NSA_KV_HEADS
        in_specs.append(tile(selt.shape[1]))
        args.append(selt)
    return pl.pallas_call(
        functools.partial(_nsa_attn_t_body, mode=mode, tq=tq, tk=tk,
                          gate_off=NSA_HEADS * (1 if mode == "slc" else 2), nbp=nbp),
        grid=(b, nq), in_specs=in_specs, out_specs=tile(dq),
        out_shape=jax.ShapeDtypeStruct((nt, dq, tq), F32),
        scratch_shapes=[pltpu.VMEM((NSA_HEADS, NSA_HEAD_DIM + nbp, tq), BF16),
                        pltpu.VMEM((NSA_HEADS, 1, tq), F32), pltpu.VMEM((NSA_HEADS, 1, tq), F32),
                        pltpu.VMEM((NSA_HEADS, NSA_HEAD_DIM, tq), F32)],
        compiler_params=_cparams(("parallel", "parallel")), name="nsa_attn_t_" + mode,
    )(*args)


def _prep_nsa_weights(w_in_nsa):
    w = w_in_nsa.astype(BF16)
    dq = NSA_HEADS * NSA_HEAD_DIM
    kvw = 2 * NSA_KV_COLS
    gate = jnp.pad(w[:, dq + 3 * kvw:], ((0, 0), (0, LANES - 3 * NSA_HEADS)))
    return [w[:, :dq], w[:, dq:dq + kvw], w[:, dq + kvw:dq + 2 * kvw], w[:, dq + 2 * kvw:dq + 3 * kvw], gate]


def _kv5(a):
    return a.reshape(a.shape[0], a.shape[1], 2, NSA_KV_HEADS, NSA_HEAD_DIM)


def nsa_prompt(x, p, tm):
    b, l, d = x.shape
    assert l % tm == 0
    wq, wc, ws, ww, wg = p["w_nsa"]
    v_rows = slice(NSA_KV_COLS, 2 * NSA_KV_COLS)
    wts = [wq.T, ws.T[v_rows], ww.T[v_rows], wg.T]
    kvc, kvs, kvw, qt, vst, vwt, gt = norm_proj(x.reshape(b * l, d), p["norm_mix"][1], [wc, ws, ww], tm, wts)
    r3 = lambda a: a.reshape(b, l, a.shape[-1])
    kvc, kvs, kvw = r3(kvc), r3(kvs), r3(kvw)
    lt = nsa_lt_dense(kvc, p["lt_wk"], p["lt_wv"], 256)
    kc, _, vct = cmp_kv(lt, p)
    n_slc = -(-l // SLC_BLOCK)
    o_c, selt = cmp_attn_select_t(qt, gt, kc, vct, b, tm, 0, n_slc)
    o_s = nsa_attn_t("slc", qt, gt, kvs, vst, selt, b, tm, tm)
    o_w = nsa_attn_t("win", qt, gt, kvw, vwt, None, b, tm, tm)
    keep = min(WINDOW, l)
    return [o_c, o_s, o_w], _kv5(kvc), _kv5(kvs), _kv5(kvw[:, l - keep:])


def nsa_sample(x, cache_cmp, cache_slc, past_win, page_table, p, n_pg):
    b, lq, d = x.shape
    tq = 8
    assert lq <= tq
    npool, page = cache_cmp.shape[:2]
    w = 2 * NSA_KV_COLS
    p0 = page_table.shape[1] * page
    xp = _pad_time(x, tq).reshape(b * tq, d)
    q, kvc, kvs, kvw, gate = norm_proj(xp, p["norm_mix"][1], p["w_nsa"], b * tq)
    r3 = lambda a: a.reshape(b, tq, a.shape[-1])
    q, kvc, kvs, kvw, gate = r3(q), r3(kvc), r3(kvs), r3(kvw), r3(gate)
    pages_t = lambda c: c.reshape(npool, page, w).transpose(0, 2, 1)
    lt = nsa_lt_paged(pages_t(cache_cmp), page_table, p["lt_wk"], p["lt_wv"], n_pg)
    kc, vc, _ = cmp_kv(lt, p)
    n_slc = -(-(p0 + lq) // SLC_BLOCK)
    o_c, sel = cmp_attn_select(q, gate, kc, vc, tq, p0, n_slc)
    nbp = sel.shape[2] // NSA_KV_HEADS
    selrows = sel.reshape(b, tq, NSA_KV_HEADS, 1, nbp).transpose(0, 2, 3, 1, 4)
    selrows = jnp.broadcast_to(selrows, (b, NSA_KV_HEADS, NSA_GROUP, tq, nbp)).reshape(b, NSA_HEADS * tq, nbp)
    o_s = slc_paged(q, gate, selrows, kvs, pages_t(cache_slc), page_table, n_pg, lq, p0)
    n_past = past_win.shape[1]
    win_all = jnp.concatenate([past_win.reshape(b, n_past, w), kvw[:, :lq]], axis=1)
    nkeys = -(-(n_past + lq) // 8) * 8
    o_w = nsa_attn("win", q, gate, _pad_time(win_all, nkeys), None, tq, nkeys, p0, p0 - n_past)
    keep = min(WINDOW, n_past + lq)
    flat = lambda a: a[:, :lq].reshape(b * lq, a.shape[-1])
    return ([flat(o_c), flat(o_s), flat(o_w)], _kv5(kvc[:, :lq]), _kv5(kvs[:, :lq]),
            _kv5(win_all[:, n_past + lq - keep:]))


def _trunk(x, ssd_state0, ssd_conv0, gdn_state0, gdn_conv0, nsa_fn, p, tm):
    b, l, d = x.shape
    t = b * l
    tm = min(tm, t)
    y_ssd, y_gdn, ssd_state, ssd_conv, gdn_state, gdn_conv = layer0_mixer(
        x, ssd_state0, ssd_conv0, gdn_state0, gdn_conv0, p, tm)
    x1 = mix_mlp(x.reshape(t, d), [[y_ssd], [y_gdn]], p["w_out_ab"], p["norm_mlp"][0], p["w_up"][0],
                 p["w_down"][0], None, tm)
    branches, cmp_rows, slc_rows, win_rows = nsa_fn(x1.reshape(b, l, d))
    y = mix_mlp(x1, [branches], [p["w_out_nsa"]], p["norm_mlp"][1], p["w_up"][1], p["w_down"][1],
                p["norm_final"], tm, transposed=branches[0].ndim == 3)
    return y.reshape(b, l, d), ssd_state, ssd_conv, gdn_state, gdn_conv, cmp_rows, slc_rows, win_rows


def kernel(x_prompt, x_sample, state_ssd, state_ssd_conv, state_gdn, state_gdn_conv, cache_cmp_kv, cache_slc_kv,
           state_win_kv, page_table, norm_mix, norm_mlp, norm_final, w_in_ab, ssd_conv_w, ssd_conv_b, ssd_dt_bias,
           ssd_a_log, ssd_d, ssd_norm_w, gdn_conv_w, gdn_dt_bias, gdn_a_log, gdn_norm_w, w_out_ab, w_in_nsa,
           cmp_pe_k, cmp_w1_k, cmp_b1_k, cmp_w2_k, cmp_pe_v, cmp_w1_v, cmp_b1_v, cmp_w2_v, w_out_nsa, w_up, w_down):
    wo = w_out_ab.astype(BF16)
    p = dict(
        norm_mix=norm_mix, norm_mlp=norm_mlp, norm_final=norm_final, w_ab=_prep_ab_weights(w_in_ab),
        ssd_conv_w=ssd_conv_w, ssd_conv_b=ssd_conv_b, ssd_dt_bias=ssd_dt_bias, ssd_a_log=ssd_a_log, ssd_d=ssd_d,
        ssd_norm_w=ssd_norm_w, gdn_conv_w=gdn_conv_w, gdn_dt_bias=gdn_dt_bias, gdn_a_log=gdn_a_log,
        gdn_norm_w=gdn_norm_w, w_out_ab=[wo[:SSD_INNER], wo[SSD_INNER:]], w_nsa=_prep_nsa_weights(w_in_nsa),
        lt_wk=_lt_weights(cmp_w1_k), lt_wv=_lt_weights(cmp_w1_v),
        cmp_pe_k=cmp_pe_k, cmp_w1_k=cmp_w1_k, cmp_b1_k=cmp_b1_k, cmp_w2_k=cmp_w2_k,
        cmp_pe_v=cmp_pe_v, cmp_w1_v=cmp_w1_v, cmp_b1_v=cmp_b1_v, cmp_w2_v=cmp_w2_v,
        w_out_nsa=w_out_nsa.astype(BF16), w_up=w_up.astype(BF16), w_down=w_down.astype(BF16))
    bp = x_prompt.shape[0]
    zeros = lambda *s: jnp.zeros(s, F32)
    tm = 256
    out_p = _trunk(x_prompt, zeros(bp, SSD_HEADS, SSD_HEADDIM, SSD_STATE), zeros(bp, CONV_W - 1, SSD_CONV_DIM),
                   zeros(bp, GDN_V_HEADS, GDN_HEAD_DIM, GDN_HEAD_DIM), zeros(bp, CONV_W - 1, GDN_CONV_DIM),
                   lambda x1: nsa_prompt(x1, p, tm), p, tm)
    out_s = _trunk(x_sample, state_ssd, state_ssd_conv, state_gdn, state_gdn_conv,
                   lambda x1: nsa_sample(x1, cache_cmp_kv, cache_slc_kv, state_win_kv, page_table, p, 16), p, tm)
    return tuple(v for pair in zip(out_p, out_s) for v in pair)
```

```python
import functools

import jax
import jax.numpy as jnp
from jax import lax
from jax.experimental import pallas as pl
from jax.experimental.pallas import tpu as pltpu

F32 = jnp.float32
BF16 = jnp.bfloat16

RMS_EPS = 1e-6
NEG_INF = -1e30
MASKED = 2.0 * NEG_INF
FORCE_BONUS = 1e4
LOG2E = 1.4426950408889634
CONV_W = 4
CHUNK = 64
SSD_HEADS, SSD_HEADDIM, SSD_STATE, SSD_GROUPS = 16, 64, 128, 2
SSD_INNER = SSD_HEADS * SSD_HEADDIM
SSD_CONV_DIM = SSD_INNER + 2 * SSD_GROUPS * SSD_STATE
GDN_QK_HEADS, GDN_V_HEADS, GDN_HEAD_DIM = 4, 8, 128
GDN_QK_DIM = GDN_QK_HEADS * GDN_HEAD_DIM
GDN_V_DIM = GDN_V_HEADS * GDN_HEAD_DIM
GDN_CONV_DIM = 2 * GDN_QK_DIM + GDN_V_DIM
NSA_HEADS, NSA_KV_HEADS, NSA_HEAD_DIM = 16, 4, 64
NSA_GROUP = NSA_HEADS // NSA_KV_HEADS
NSA_KV_COLS = NSA_KV_HEADS * NSA_HEAD_DIM
CMP_STRIDE, CMP_LEN, CMP_HIDDEN = 16, 32, 64
SLC_BLOCK, SLC_TOPN, WINDOW = 64, 16, 512
PAGE_SIZE = 128
LANES = 128
VMEM_LIMIT = 56 * 1024 * 1024


def _cparams(sem):
    return pltpu.CompilerParams(dimension_semantics=sem, vmem_limit_bytes=VMEM_LIMIT)


def _dot(a, b):
    return jnp.dot(a, b, preferred_element_type=F32)


def _dot_nt(a, b):
    return lax.dot_general(a, b, (((1,), (1,)), ((), ())), preferred_element_type=F32)


def _dot_tn(a, b):
    return lax.dot_general(a, b, (((0,), (0,)), ((), ())), preferred_element_type=F32)


def _split2(x):
    hi = x.astype(BF16)
    lo = (x - hi.astype(F32)).astype(BF16)
    return hi, lo


def _split3(x):
    hi = x.astype(BF16)
    r = x - hi.astype(F32)
    mid = r.astype(BF16)
    lo = (r - mid.astype(F32)).astype(BF16)
    return hi, mid, lo


def _mm3(a, b):
    ah, al = _split2(a)
    bh, bl = _split2(b)
    return _dot(ah, bh) + (_dot(ah, bl) + _dot(al, bh))


def _rms_unit(x):
    return x * lax.rsqrt(jnp.mean(x * x, axis=-1, keepdims=True) + RMS_EPS)


def _silu(x):
    return x * jax.nn.sigmoid(x)


def _softplus(x):
    return jnp.maximum(x, 0.0) + jnp.log1p(jnp.exp(-jnp.abs(x)))


def _iota2(shape, axis):
    return lax.broadcasted_iota(jnp.int32, shape, axis)


def _norm_proj_body(x_ref, g_ref, *refs, n_out, n_t):
    n_w = n_out + n_t
    w_refs, o_refs = refs[:n_w], refs[n_w:]
    hn = (_rms_unit(x_ref[...]) * g_ref[...]).astype(BF16)
    for w_ref, o_ref in zip(w_refs[:n_out], o_refs[:n_out]):
        n = w_ref.shape[1]
        for c0 in range(0, n, 512):
            c1 = min(n, c0 + 512)
            o_ref[:, c0:c1] = _dot(hn, w_ref[:, c0:c1])
    for w_ref, o_ref in zip(w_refs[n_out:], o_refs[n_out:]):
        n = w_ref.shape[0]
        for c0 in range(0, n, 512):
            c1 = min(n, c0 + 512)
            o_ref[0, c0:c1, :] = _dot_nt(w_ref[c0:c1, :], hn)


def norm_proj(x, g, ws, tm, wts=()):
    t, d = x.shape
    assert t % tm == 0
    in_specs = [pl.BlockSpec((tm, d), lambda i: (i, 0)), pl.BlockSpec((1, d), lambda i: (0, 0))]
    in_specs += [pl.BlockSpec(w.shape, lambda i: (0, 0)) for w in (*ws, *wts)]
    out_specs = [pl.BlockSpec((tm, w.shape[1]), lambda i: (i, 0)) for w in ws]
    out_specs += [pl.BlockSpec((1, w.shape[0], tm), lambda i: (i, 0, 0)) for w in wts]
    out_shape = [jax.ShapeDtypeStruct((t, w.shape[1]), F32) for w in ws]
    out_shape += [jax.ShapeDtypeStruct((t // tm, w.shape[0], tm), F32) for w in wts]
    return pl.pallas_call(
        functools.partial(_norm_proj_body, n_out=len(ws), n_t=len(wts)),
        grid=(t // tm,), in_specs=in_specs, out_specs=out_specs, out_shape=out_shape,
        compiler_params=_cparams(("parallel",)), name="norm_proj",
    )(x, g.reshape(1, d), *ws, *wts)


def _mix_mlp_body(x_ref, *refs, group_sizes, transposed, final, hc):
    n_y, n_g = sum(group_sizes), len(group_sizes)
    y_refs = refs[:n_y]
    wo_refs = refs[n_y:n_y + n_g]
    g_ref, wu_ref, wd_ref = refs[n_y + n_g:n_y + n_g + 3]
    rest = refs[n_y + n_g + 3:]
    gf_ref = rest[0] if final else None
    o_ref = rest[-1]
    x1 = x_ref[...]
    i = 0
    for gs, wo_ref in zip(group_sizes, wo_refs):
        rd = (lambda r: r[0]) if transposed else (lambda r: r[...])
        y = rd(y_refs[i])
        for r in y_refs[i + 1:i + gs]:
            y = y + rd(r)
        i += gs
        x1 = x1 + (_dot_tn if transposed else _dot)(y.astype(BF16), wo_ref[...])
    hn = (_rms_unit(x1) * g_ref[...]).astype(BF16)
    acc = x1
    hidden = wu_ref.shape[1]
    for c0 in range(0, hidden, hc):
        h = jnp.maximum(_dot(hn, wu_ref[:, c0:c0 + hc]), 0.0)
        acc = acc + _dot((h * h).astype(BF16), wd_ref[c0:c0 + hc, :])
    if final:
        acc = _rms_unit(acc) * gf_ref[...]
    o_ref[...] = acc


def mix_mlp(x, y_groups, w_outs, g, w_up, w_down, g_final, tm, hc=None, transposed=False):
    t, d = x.shape
    hc = hc or w_up.shape[1]
    assert t % tm == 0
    final = g_final is not None
    const = lambda i: (0, 0)
    ys = [y for grp in y_groups for y in grp]
    in_specs = [pl.BlockSpec((tm, d), lambda i: (i, 0))]
    if transposed:
        assert all(y.shape[0] == t // tm and y.shape[2] == tm for y in ys)
        in_specs += [pl.BlockSpec((1, y.shape[1], tm), lambda i: (i, 0, 0)) for y in ys]
    else:
        in_specs += [pl.BlockSpec((tm, y.shape[1]), lambda i: (i, 0)) for y in ys]
    in_specs += [pl.BlockSpec(w.shape, const) for w in w_outs]
    in_specs += [pl.BlockSpec((1, d), const), pl.BlockSpec(w_up.shape, const), pl.BlockSpec(w_down.shape, const)]
    args = [x, *ys, *w_outs, g.reshape(1, d), w_up, w_down]
    if final:
        in_specs.append(pl.BlockSpec((1, d), const))
        args.append(g_final.reshape(1, d))
    return pl.pallas_call(
        functools.partial(_mix_mlp_body, group_sizes=tuple(len(grp) for grp in y_groups), transposed=transposed,
                          final=final, hc=hc),
        grid=(t // tm,), in_specs=in_specs, out_specs=pl.BlockSpec((tm, d), lambda i: (i, 0)),
        out_shape=jax.ShapeDtypeStruct((t, d), F32),
        compiler_params=_cparams(("parallel",)), name="mix_mlp",
    )(*args)


def _causal_conv(x, xp_ref, w_ref, first, conv0_ref):
    c = x.shape[0]

    @pl.when(first)
    def _():
        xp_ref[5:8, :] = conv0_ref[0]

    xp_ref[8:8 + c, :] = x
    y = xp_ref[5:5 + c, :] * w_ref[0:1, :]
    for k in range(1, CONV_W):
        y = y + xp_ref[5 + k:5 + k + c, :] * w_ref[k:k + 1, :]
    xp_ref[5:8, :] = x[c - 3:c, :]
    return y


def _cumsum_pair(col, row):
    c = col.shape[0]
    ri, ci = _iota2((c, c), 0), _iota2((c, c), 1)
    tril = jnp.where(ri >= ci, 1.0, 0.0).astype(BF16)
    triu = jnp.where(ri <= ci, 1.0, 0.0).astype(BF16)
    ch, cm, cl = _split3(col)
    rh, rm, rl = _split3(row)
    cs = _dot(tril, ch) + (_dot(tril, cm) + _dot(tril, cl))
    cst = _dot(rh, triu) + (_dot(rm, triu) + _dot(rl, triu))
    return cs, cst


def _ssd_body(xbc_ref, z_ref, sm_ref, smt_ref, st0_ref, cv0_ref, cw_ref, cb_ref, dtb_r_ref, dtb_c_ref,
              al_r_ref, al_c_ref, dfull_ref, nw_ref, y_ref, st_ref, xp_ref, ybuf_ref, *, l_valid, nsub):
    cidx = pl.program_id(1)
    c = CHUNK
    tb = nsub * c
    n, p = SSD_STATE, SSD_HEADDIM
    heads_per_group = SSD_HEADS // SSD_GROUPS
    heads = range(SSD_HEADS)
    grp = lambda h: h // heads_per_group

    @pl.when(cidx == 0)
    def _():
        st_ref[0] = st0_ref[0]

    conv = _causal_conv(xbc_ref[0], xp_ref, cw_ref, cidx == 0, cv0_ref) + cb_ref[...]
    act = _silu(conv)
    xs = act[:, :SSD_INNER]
    valid_c = (cidx * tb + _iota2((tb, 1), 0)) < l_valid
    dt_all = jnp.where(valid_c, _softplus(sm_ref[0][:, 0:SSD_HEADS] + dtb_r_ref[...]), 0.0)
    a_r, a_c = -jnp.exp(al_r_ref[...]), -jnp.exp(al_c_ref[...])
    causal = _iota2((c, c), 0) >= _iota2((c, c), 1)

    subs = []
    for s in range(nsub):
        rows = slice(s * c, (s + 1) * c)
        valid_r = (cidx * tb + s * c + _iota2((1, c), 1)) < l_valid
        dt = dt_all[rows]
        dtt = jnp.where(valid_r, _softplus(smt_ref[0, s][0:SSD_HEADS, :] + dtb_c_ref[...]), 0.0)
        cs, cst = _cumsum_pair(dt * a_r, dtt * a_c)
        cs_last = cs[c - 1:c, :]
        bm_l = [act[rows, SSD_INNER + g * n:SSD_INNER + (g + 1) * n].astype(BF16) for g in range(SSD_GROUPS)]
        c0 = SSD_INNER + SSD_GROUPS * n
        cm_l = [act[rows, c0 + g * n:c0 + (g + 1) * n].astype(BF16) for g in range(SSD_GROUPS)]
        subs.append(dict(rows=rows, dtt=dtt, cs=cs, cst=cst, ecs=jnp.exp(cs), to_end=jnp.exp(cs_last - cs) * dt,
                         edec=jnp.exp(cs_last), bm=bm_l, cm=cm_l, x=[xs[rows, h * p:(h + 1) * p] for h in heads]))
    for d in subs:
        d["cb"] = [_dot_nt(c_g, b_g) for c_g, b_g in zip(d["cm"], d["bm"])]
    for d in subs:
        d["sc"] = []
        for h in heads:
            seg = d["cs"][:, h:h + 1] - d["cst"][h:h + 1, :]
            decay = jnp.exp(jnp.where(causal, seg, NEG_INF))
            d["sc"].append((d["cb"][grp(h)] * decay * d["dtt"][h:h + 1, :]).astype(BF16))
    for d in subs:
        d["yd"] = [_dot(d["sc"][h], d["x"][h].astype(BF16)) for h in heads]

    s_l = [st_ref[0, h] for h in heads]
    for d in subs:
        yo_l = [_dot_nt(d["cm"][grp(h)], s_l[h].astype(BF16)) * d["ecs"][:, h:h + 1] for h in heads]
        for h in heads:
            ybuf_ref[d["rows"], h * p:(h + 1) * p] = d["yd"][h] + yo_l[h]
        s_l = [s_l[h] * d["edec"][:, h:h + 1]
               + _dot_tn((d["x"][h] * d["to_end"][:, h:h + 1]).astype(BF16), d["bm"][grp(h)]) for h in heads]
    for h in heads:
        st_ref[0, h] = s_l[h]

    y = ybuf_ref[...] + dfull_ref[...] * xs
    yz = y * _silu(z_ref[0])
    gw = SSD_INNER // SSD_GROUPS
    for g in range(SSD_GROUPS):
        seg = yz[:, g * gw:(g + 1) * gw]
        y_ref[0, :, g * gw:(g + 1) * gw] = _rms_unit(seg) * nw_ref[:, g * gw:(g + 1) * gw]


def ssd_mixer(xbc, z, small, small_t, state0, conv0, conv_w, conv_b, dt_bias, a_log, d_skip, norm_w, l_valid):
    b, lp, _ = xbc.shape
    nc = lp // CHUNK
    nsub = 1
    h = SSD_HEADS
    d_full = jnp.repeat(d_skip, SSD_HEADDIM).reshape(1, SSD_INNER)
    const2 = lambda i, j: (0, 0)
    tok = lambda w: pl.BlockSpec((1, nsub * CHUNK, w), lambda i, j: (i, j, 0))
    in_specs = [
        tok(SSD_CONV_DIM), tok(SSD_INNER), tok(LANES),
        pl.BlockSpec((1, nsub, 32, CHUNK), lambda i, j: (i, j, 0, 0)),
        pl.BlockSpec((1, h, SSD_HEADDIM, SSD_STATE), lambda i, j: (i, 0, 0, 0)),
        pl.BlockSpec((1, CONV_W - 1, SSD_CONV_DIM), lambda i, j: (i, 0, 0)),
        pl.BlockSpec((CONV_W, SSD_CONV_DIM), const2), pl.BlockSpec((1, SSD_CONV_DIM), const2),
        pl.BlockSpec((1, h), const2), pl.BlockSpec((h, 1), const2),
        pl.BlockSpec((1, h), const2), pl.BlockSpec((h, 1), const2),
        pl.BlockSpec((1, SSD_INNER), const2), pl.BlockSpec((1, SSD_INNER), const2),
    ]
    out_specs = [tok(SSD_INNER), pl.BlockSpec((1, h, SSD_HEADDIM, SSD_STATE), lambda i, j: (i, 0, 0, 0))]
    out_shape = [jax.ShapeDtypeStruct((b, lp, SSD_INNER), F32),
                 jax.ShapeDtypeStruct((b, h, SSD_HEADDIM, SSD_STATE), F32)]
    return pl.pallas_call(
        functools.partial(_ssd_body, l_valid=l_valid, nsub=nsub),
        grid=(b, nc // nsub), in_specs=in_specs, out_specs=out_specs, out_shape=out_shape,
        scratch_shapes=[pltpu.VMEM((nsub * CHUNK + 8, SSD_CONV_DIM), F32),
                        pltpu.VMEM((nsub * CHUNK, SSD_INNER), F32)],
        compiler_params=_cparams(("parallel", "arbitrary")), name="ssd_mixer",
    )(xbc, z, small, small_t, state0, conv0, conv_w, conv_b.reshape(1, -1),
      dt_bias.reshape(1, h), dt_bias.reshape(h, 1), a_log.reshape(1, h), a_log.reshape(h, 1),
      d_full, norm_w.reshape(1, SSD_INNER))


def _unit_lower_inverses(mats):
    c = mats[0].shape[0]
    ri, ci = _iota2((c, c), 0), _iota2((c, c), 1)
    eye = jnp.where(ri == ci, 1.0, 0.0)
    same_blk = lax.shift_right_logical(ri, 4) == lax.shift_right_logical(ci, 4)
    d = [jnp.where(same_blk, a, 0.0) for a in mats]
    nn = [a - x for a, x in zip(mats, d)]
    d2 = [_mm3(x, x) for x in d]
    d4 = [_mm3(x, x) for x in d2]
    pinv = [eye - x for x in d]
    pinv = [x + _mm3(x, y) for x, y in zip(pinv, d2)]
    d8 = [_mm3(x, x) for x in d4]
    pinv = [x + _mm3(x, y) for x, y in zip(pinv, d4)]
    pinv = [x + _mm3(x, y) for x, y in zip(pinv, d8)]
    e = [_mm3(x, y) for x, y in zip(pinv, nn)]
    e2 = [_mm3(x, x) for x in e]
    q = [eye - x for x in e]
    q = [x + _mm3(x, y) for x, y in zip(q, e2)]
    return [_mm3(x, y) for x, y in zip(q, pinv)]


def _l2norm(x):
    return x * lax.rsqrt(jnp.sum(x * x, axis=-1, keepdims=True) + 1e-6)


def _gdn_body(qkv_ref, z_ref, sm_ref, smt_ref, st0_ref, cv0_ref, cw_ref, dtb_r_ref, dtb_c_ref,
              al_r_ref, al_c_ref, nw_ref, o_ref, st_ref, xp_ref, *, l_valid, nsub):
    cidx = pl.program_id(1)
    c = CHUNK
    tb = nsub * c
    hd = GDN_HEAD_DIM
    nh = GDN_V_HEADS
    heads = range(nh)
    rep = GDN_V_HEADS // GDN_QK_HEADS

    @pl.when(cidx == 0)
    def _():
        st_ref[0] = st0_ref[0]

    act = _silu(_causal_conv(qkv_ref[0], xp_ref, cw_ref, cidx == 0, cv0_ref))
    valid_c = (cidx * tb + _iota2((tb, 1), 0)) < l_valid
    sm = sm_ref[0]
    beta_all = jnp.where(valid_c, jax.nn.sigmoid(sm[:, 16:16 + nh]), 0.0)
    g_all = jnp.where(valid_c, -jnp.exp(al_r_ref[...]) * _softplus(sm[:, 24:24 + nh] + dtb_r_ref[...]), 0.0)
    q_all = [_l2norm(act[:, hq * hd:(hq + 1) * hd]) * (hd ** -0.5) for hq in range(GDN_QK_HEADS)]
    k_all = [_l2norm(act[:, GDN_QK_DIM + hq * hd:GDN_QK_DIM + (hq + 1) * hd]) for hq in range(GDN_QK_HEADS)]
    ri, ci = _iota2((c, c), 0), _iota2((c, c), 1)
    causal = ri >= ci
    strict = ri > ci
    eye = jnp.where(ri == ci, 1.0, 0.0)

    subs = []
    for s in range(nsub):
        rows = slice(s * c, (s + 1) * c)
        valid_r = (cidx * tb + s * c + _iota2((1, c), 1)) < l_valid
        g_r = jnp.where(valid_r, -jnp.exp(al_c_ref[...]) * _softplus(smt_ref[0, s][24:24 + nh, :] + dtb_c_ref[...]),
                        0.0)
        cs, cst = _cumsum_pair(g_all[rows], g_r)
        cs_last = cs[c - 1:c, :]
        d = dict(rows=rows, beta=beta_all[rows], ecs=jnp.exp(cs), e_end=jnp.exp(cs_last - cs),
                 etot=jnp.exp(cs_last), q=[x[rows] for x in q_all], k=[x[rows] for x in k_all], qk=[], a=[])
        for hq in range(GDN_QK_HEADS):
            k16 = d["k"][hq].astype(BF16)
            kk = _dot_nt(k16, k16)
            qk0 = _dot_nt(d["q"][hq].astype(BF16), k16)
            for r in range(rep):
                h = hq * rep + r
                seg = cs[:, h:h + 1] - cst[h:h + 1, :]
                decay = jnp.exp(jnp.where(causal, seg, NEG_INF))
                d["qk"].append((qk0 * decay).astype(BF16))
                d["a"].append(jnp.where(strict, kk * d["beta"][:, h:h + 1] * decay, 0.0))
        subs.append(d)
    tinv_all = _unit_lower_inverses([a for d in subs for a in d["a"]])
    for s, d in enumerate(subs):
        rhs_l = []
        for h in heads:
            b_h = d["beta"][:, h:h + 1]
            v_h = act[d["rows"], 2 * GDN_QK_DIM + h * hd:2 * GDN_QK_DIM + (h + 1) * hd]
            rhs_l.append(jnp.concatenate([v_h * b_h, d["k"][h // rep] * (b_h * d["ecs"][:, h:h + 1])], axis=1))
        d["sol"] = [rhs_l[h] + _dot((tinv_all[s * nh + h] - eye).astype(BF16), rhs_l[h].astype(BF16))
                    for h in heads]

    s_l = [st_ref[0, h] for h in heads]
    for d in subs:
        ecs, rows = d["ecs"], d["rows"]
        s16_l = [x.astype(BF16) for x in s_l]
        o1_l = [_dot((d["q"][h // rep] * ecs[:, h:h + 1]).astype(BF16), s16_l[h]) for h in heads]
        v16_l = [(d["sol"][h][:, :hd] - _dot(d["sol"][h][:, hd:].astype(BF16), s16_l[h])).astype(BF16)
                 for h in heads]
        o_l = [o1_l[h] + _dot(d["qk"][h], v16_l[h]) for h in heads]
        s_l = [s_l[h] * d["etot"][:, h:h + 1]
               + _dot_tn((d["k"][h // rep] * d["e_end"][:, h:h + 1]).astype(BF16), v16_l[h]) for h in heads]
        for h in heads:
            o_ref[0, rows, h * hd:(h + 1) * hd] = ((_rms_unit(o_l[h]) * nw_ref[...])
                                                   * _silu(z_ref[0, rows, h * hd:(h + 1) * hd]))
    for h in heads:
        st_ref[0, h] = s_l[h]


def _chunks_per_step(nc):
    return 2 if nc % 2 == 0 else 1


def gdn_mixer(qkv, z, small, small_t, state0, conv0, conv_w, dt_bias, a_log, norm_w, l_valid):
    b, lp, _ = qkv.shape
    nc = lp // CHUNK
    nsub = _chunks_per_step(nc)
    h = GDN_V_HEADS
    const2 = lambda i, j: (0, 0)
    tok = lambda w: pl.BlockSpec((1, nsub * CHUNK, w), lambda i, j: (i, j, 0))
    in_specs = [
        tok(GDN_CONV_DIM), tok(GDN_V_DIM), tok(LANES),
        pl.BlockSpec((1, nsub, 32, CHUNK), lambda i, j: (i, j, 0, 0)),
        pl.BlockSpec((1, h, GDN_HEAD_DIM, GDN_HEAD_DIM), lambda i, j: (i, 0, 0, 0)),
        pl.BlockSpec((1, CONV_W - 1, GDN_CONV_DIM), lambda i, j: (i, 0, 0)),
        pl.BlockSpec((CONV_W, GDN_CONV_DIM), const2),
        pl.BlockSpec((1, h), const2), pl.BlockSpec((h, 1), const2),
        pl.BlockSpec((1, h), const2), pl.BlockSpec((h, 1), const2),
        pl.BlockSpec((1, GDN_HEAD_DIM), const2),
    ]
    out_specs = [tok(GDN_V_DIM), pl.BlockSpec((1, h, GDN_HEAD_DIM, GDN_HEAD_DIM), lambda i, j: (i, 0, 0, 0))]
    out_shape = [jax.ShapeDtypeStruct((b, lp, GDN_V_DIM), F32),
                 jax.ShapeDtypeStruct((b, h, GDN_HEAD_DIM, GDN_HEAD_DIM), F32)]
    return pl.pallas_call(
        functools.partial(_gdn_body, l_valid=l_valid, nsub=nsub),
        grid=(b, nc // nsub), in_specs=in_specs, out_specs=out_specs, out_shape=out_shape,
        scratch_shapes=[pltpu.VMEM((nsub * CHUNK + 8, GDN_CONV_DIM), F32)],
        compiler_params=_cparams(("parallel", "arbitrary")), name="gdn_mixer",
    )(qkv, z, small, small_t, state0, conv0, conv_w,
      dt_bias.reshape(1, h), dt_bias.reshape(h, 1), a_log.reshape(1, h), a_log.reshape(h, 1),
      norm_w.reshape(1, GDN_HEAD_DIM))


def _prep_ab_weights(w_in_ab):
    o = [0, SSD_INNER]
    o.append(o[-1] + SSD_CONV_DIM)
    o.append(o[-1] + SSD_HEADS)
    o.append(o[-1] + GDN_CONV_DIM)
    o.append(o[-1] + GDN_V_DIM)
    o.append(o[-1] + 2 * GDN_V_HEADS)
    w = w_in_ab.astype(BF16)
    small = jnp.concatenate([w[:, o[2]:o[3]], w[:, o[5]:o[6]]], axis=1)
    small = jnp.pad(small, ((0, 0), (0, LANES - small.shape[1])))
    return [w[:, o[0]:o[1]], w[:, o[1]:o[2]], w[:, o[3]:o[4]], w[:, o[4]:o[5]], small]


def _pad_time(a, lp):
    return jnp.pad(a, [(0, 0), (0, lp - a.shape[1])] + [(0, 0)] * (a.ndim - 2))


def layer0_mixer(x, ssd_state0, ssd_conv0, gdn_state0, gdn_conv0, p, tm):
    b, l, d = x.shape
    lp = -(-l // CHUNK) * CHUNK
    xp = _pad_time(x, lp).reshape(b * lp, d)
    z_ssd, xbc, qkv, z_gdn, small = norm_proj(xp, p["norm_mix"][0], p["w_ab"], min(tm, b * lp))
    nc = lp // CHUNK
    small_t = small[:, :32].reshape(b, nc, CHUNK, 32).transpose(0, 1, 3, 2)
    r3 = lambda a: a.reshape(b, lp, a.shape[-1])
    xbc, qkv, small = r3(xbc), r3(qkv), r3(small)
    y_ssd, ssd_state = ssd_mixer(xbc, r3(z_ssd), small, small_t, ssd_state0, ssd_conv0, p["ssd_conv_w"],
                                 p["ssd_conv_b"], p["ssd_dt_bias"], p["ssd_a_log"], p["ssd_d"], p["ssd_norm_w"], l)
    y_gdn, gdn_state = gdn_mixer(qkv, r3(z_gdn), small, small_t, gdn_state0, gdn_conv0, p["gdn_conv_w"],
                                 p["gdn_dt_bias"], p["gdn_a_log"], p["gdn_norm_w"], l)
    ssd_conv = jnp.concatenate([ssd_conv0, xbc[:, :l]], axis=1)[:, l:]
    gdn_conv = jnp.concatenate([gdn_conv0, qkv[:, :l]], axis=1)[:, l:]
    y_ssd = y_ssd[:, :l].reshape(b * l, SSD_INNER)
    y_gdn = y_gdn[:, :l].reshape(b * l, GDN_V_DIM)
    return y_ssd, y_gdn, ssd_state, ssd_conv, gdn_state, gdn_conv


def _nsa_lt_body(*refs, n_in):
    x_refs = refs[:n_in]
    wk_ref, wv_ref, o_ref = refs[n_in:]
    w = 2 * NSA_KV_COLS
    acc_k = acc_v = None
    for j in range(CMP_STRIDE):
        xk = jnp.concatenate([r[0, :, j * w:j * w + NSA_KV_COLS] for r in x_refs], axis=0).astype(BF16)
        xv = jnp.concatenate([r[0, :, j * w + NSA_KV_COLS:(j + 1) * w] for r in x_refs], axis=0).astype(BF16)
        dk, dv = _dot(xk, wk_ref[j]), _dot(xv, wv_ref[j])
        acc_k = dk if acc_k is None else acc_k + dk
        acc_v = dv if acc_v is None else acc_v + dv
    o_ref[0, :, 0:2 * NSA_KV_COLS] = acc_k
    o_ref[0, :, 2 * NSA_KV_COLS:] = acc_v


def _lt_weights(w1):
    w1r = w1.reshape(2, CMP_STRIDE, NSA_HEAD_DIM, CMP_HIDDEN)
    eye = jnp.eye(NSA_KV_HEADS, dtype=w1.dtype)
    lead = jnp.einsum("ab,jdh->jadbh", eye, w1r[0]).reshape(CMP_STRIDE, NSA_KV_COLS, NSA_KV_COLS)
    tail = jnp.einsum("ab,jdh->jadbh", eye, w1r[1]).reshape(CMP_STRIDE, NSA_KV_COLS, NSA_KV_COLS)
    return jnp.concatenate([lead, tail], axis=-1).astype(BF16)


def nsa_lt_dense(kv, wk, wv, tc):
    b, l, w = kv.shape
    nch = l // CMP_STRIDE
    x = kv[:, :nch * CMP_STRIDE].reshape(b, nch, CMP_STRIDE * w)
    tc = min(tc, nch)
    assert nch % tc == 0
    const3 = lambda i, j: (0, 0, 0)
    return pl.pallas_call(
        functools.partial(_nsa_lt_body, n_in=1), grid=(b, nch // tc),
        in_specs=[pl.BlockSpec((1, tc, CMP_STRIDE * w), lambda i, j: (i, j, 0)),
                  pl.BlockSpec(wk.shape, const3), pl.BlockSpec(wv.shape, const3)],
        out_specs=pl.BlockSpec((1, tc, 4 * NSA_KV_COLS), lambda i, j: (i, j, 0)),
        out_shape=jax.ShapeDtypeStruct((b, nch, 4 * NSA_KV_COLS), F32),
        compiler_params=_cparams(("parallel", "parallel")), name="nsa_lt_dense",
    )(x, wk, wv)


def _nsa_lt_paged_body(pt_ref, *refs, n_pg):
    page_refs = refs[:n_pg]
    wk_ref, wv_ref, o_ref, xa_ref, xb_ref = refs[n_pg:]
    gi = pl.program_id(1)
    page = page_refs[0].shape[2]
    n_col = page_refs[0].shape[1] // LANES
    rows = n_pg * page // CMP_STRIDE
    half = n_col // 2

    @pl.when(gi == 0)
    def _():
        xb_ref[...] = jnp.zeros(xb_ref.shape, F32)

    chunks = page // CMP_STRIDE
    ti, ci = _iota2((page, page), 0), _iota2((page, page), 1)
    perm = jnp.where(ci == jnp.bitwise_and(ti, CMP_STRIDE - 1) * chunks + lax.shift_right_logical(ti, 4),
                     1.0, 0.0).astype(BF16)

    def work(fill_ref, read_ref):
        for i, r in enumerate(page_refs):
            xp = _dot(r[0].astype(BF16), perm)
            for c in range(n_col):
                fill_ref[c, i * page:(i + 1) * page, :] = xp[c * LANES:(c + 1) * LANES, :].T
        acc_k = acc_v = None
        for j in range(CMP_STRIDE):
            xj = [jnp.concatenate([read_ref[c, i * page + j * chunks:i * page + (j + 1) * chunks, :]
                                   for i in range(n_pg)], axis=0) for c in range(n_col)]
            dk = _dot(jnp.concatenate(xj[:half], axis=1).astype(BF16), wk_ref[j])
            dv = _dot(jnp.concatenate(xj[half:], axis=1).astype(BF16), wv_ref[j])
            acc_k = dk if acc_k is None else acc_k + dk
            acc_v = dv if acc_v is None else acc_v + dv
        o_ref[0, :, 0:2 * NSA_KV_COLS] = acc_k
        o_ref[0, :, 2 * NSA_KV_COLS:] = acc_v

    @pl.when(jnp.bitwise_and(gi, 1) == 0)
    def _():
        work(xa_ref, xb_ref)

    @pl.when(jnp.bitwise_and(gi, 1) == 1)
    def _():
        work(xb_ref, xa_ref)


def nsa_lt_paged(cache_t, page_table, wk, wv, n_pg):
    npool, w, page = cache_t.shape
    b, n_pages = page_table.shape
    rows = page // CMP_STRIDE
    n_grp = n_pages // n_pg
    assert n_pages % n_pg == 0
    page_spec = lambda i: pl.BlockSpec(
        (1, w, page), lambda bi, gi, pt: (pt[bi, jnp.minimum(gi, n_grp - 1) * n_pg + i], 0, 0))
    const3 = lambda bi, gi, pt: (0, 0, 0)
    buf = pltpu.VMEM((w // LANES, n_pg * page, LANES), F32)
    grid_spec = pltpu.PrefetchScalarGridSpec(
        num_scalar_prefetch=1, grid=(b, n_grp + 1),
        in_specs=[page_spec(i) for i in range(n_pg)] + [pl.BlockSpec(wk.shape, const3), pl.BlockSpec(wv.shape, const3)],
        out_specs=pl.BlockSpec((1, n_pg * rows, 4 * NSA_KV_COLS), lambda bi, gi, pt: (bi, jnp.maximum(gi - 1, 0), 0)),
        scratch_shapes=[buf, buf])
    return pl.pallas_call(
        functools.partial(_nsa_lt_paged_body, n_pg=n_pg), grid_spec=grid_spec,
        out_shape=jax.ShapeDtypeStruct((b, n_pages * rows, 4 * NSA_KV_COLS), F32),
        compiler_params=_cparams(("parallel", "arbitrary")), name="nsa_lt_paged",
    )(page_table, *([cache_t] * n_pg), wk, wv)


def _cmp_kv_body(lt_ref, pek_ref, w1k_ref, b1k_ref, w2k_ref, pev_ref, w1v_ref, b1v_ref, w2v_ref, w2vt_ref,
                 kc_ref, vc_ref, vct_ref, sh_ref):
    nch = lt_ref.shape[1]
    kvc = NSA_KV_COLS

    def branch(off, pe_ref, w1_ref, b1_ref, w2_ref, out_ref):
        pe = jnp.broadcast_to(pe_ref[...], (8, pe_ref.shape[1]))
        pe_term = _mm3(pe, w1_ref[...])[0:1, :] + b1_ref[...]
        pe4 = jnp.concatenate([pe_term] * NSA_KV_HEADS, axis=1)
        sh_ref[0:nch, :] = lt_ref[0, :, off + kvc:off + 2 * kvc]
        sh_ref[nch:nch + 8, :] = jnp.zeros((8, kvc), F32)
        hid = _silu(lt_ref[0, :, off:off + kvc] + sh_ref[1:nch + 1, :] + pe4).astype(BF16)
        out_ref[0] = _dot(hid, w2_ref[...])
        return hid

    branch(0, pek_ref, w1k_ref, b1k_ref, w2k_ref, kc_ref)
    hid_v = branch(2 * kvc, pev_ref, w1v_ref, b1v_ref, w2v_ref, vc_ref)
    vct_ref[0] = _dot_nt(w2vt_ref[...], hid_v)


def cmp_kv(lt, p):
    b, nch, _ = lt.shape
    kvc = NSA_KV_COLS
    const2 = lambda i: (0, 0)
    args, specs = [], []
    for s in ("k", "v"):
        pe = p["cmp_pe_" + s].reshape(1, -1)
        w1 = p["cmp_w1_" + s]
        b1 = p["cmp_b1_" + s].reshape(1, -1)
        w2 = jnp.kron(jnp.eye(NSA_KV_HEADS, dtype=F32), p["cmp_w2_" + s]).astype(BF16)
        for a in (pe, w1, b1, w2):
            args.append(a)
            specs.append(pl.BlockSpec(a.shape, const2))
    args.append(jnp.kron(jnp.eye(NSA_KV_HEADS, dtype=F32), p["cmp_w2_v"].T).astype(BF16))
    specs.append(pl.BlockSpec(args[-1].shape, const2))
    out_spec = pl.BlockSpec((1, nch, kvc), lambda i: (i, 0, 0))
    return pl.pallas_call(
        _cmp_kv_body, grid=(b,),
        in_specs=[pl.BlockSpec((1, nch, 4 * kvc), lambda i: (i, 0, 0))] + specs,
        out_specs=[out_spec, out_spec, pl.BlockSpec((1, kvc, nch), lambda i: (i, 0, 0))],
        out_shape=[jax.ShapeDtypeStruct((b, nch, kvc), F32)] * 2 + [jax.ShapeDtypeStruct((b, kvc, nch), F32)],
        scratch_shapes=[pltpu.VMEM((nch + 8, kvc), F32)],
        compiler_params=_cparams(("parallel",)), name="cmp_kv",
    )(lt, *args)


def _cmp_attn_body(q_ref, gate_ref, kc_ref, vc_ref, oc_ref, sel_ref, *, tq, pos0, n_slc, nbp):
    qi = pl.program_id(1)
    ncp = kc_ref.shape[1]
    hd = NSA_HEAD_DIM
    t_col = pos0 + qi * tq + _iota2((tq, 1), 0)
    cmask = (_iota2((1, ncp), 1) * CMP_STRIDE + (CMP_LEN - 1)) <= t_col
    kc = kc_ref[0].astype(BF16)
    vc = vc_ref[0].astype(BF16)
    gate = jax.nn.sigmoid(gate_ref[0])
    nn = _iota2((ncp, nbp), 0) * CMP_STRIDE
    ss = _iota2((ncp, nbp), 1) * SLC_BLOCK
    overlap = jnp.where((nn < ss + SLC_BLOCK) & (nn + CMP_LEN > ss), 1.0, 0.0).astype(BF16)
    blk = _iota2((1, nbp), 1)
    blk_f = blk.astype(F32)
    cur = lax.shift_right_logical(t_col, 6)
    forced = (blk == 0) | (blk == cur) | (blk == cur - 1)
    bonus = jnp.where(forced, FORCE_BONUS, 0.0)
    causal_blk = blk * SLC_BLOCK <= t_col
    n_pick = min(SLC_TOPN, n_slc)

    scores = []
    for k in range(NSA_KV_HEADS):
        kc_k = kc[:, k * hd:(k + 1) * hd]
        vc_k = vc[:, k * hd:(k + 1) * hd]
        hs = [k * NSA_GROUP + g for g in range(NSA_GROUP)]
        q_l = [(q_ref[0, :, h * hd:(h + 1) * hd] * (hd ** -0.5)).astype(BF16) for h in hs]
        s_l = [jnp.where(cmask, _dot_nt(q_h, kc_k), NEG_INF) for q_h in q_l]
        e_l = [jnp.where(cmask, jnp.exp(s - jnp.max(s, axis=-1, keepdims=True)), 0.0) for s in s_l]
        den_l = [jnp.sum(e, axis=-1, keepdims=True) for e in e_l]
        p_l = [e * (1.0 / jnp.where(den > 0.0, den, 1.0)) for e, den in zip(e_l, den_l)]
        o_l = [_dot(p.astype(BF16), vc_k) for p in p_l]
        for h, o in zip(hs, o_l):
            oc_ref[0, :, h * hd:(h + 1) * hd] = o * gate[:, h:h + 1]
        ph, plo = _split2((p_l[0] + p_l[1]) + (p_l[2] + p_l[3]))
        imp = _dot(ph, overlap) + _dot(plo, overlap)
        score = jnp.where(causal_blk, imp + bonus, NEG_INF)
        scores.append(jnp.where(blk < n_slc, score, -jnp.inf))

    def pick_one(_, carry):
        work, sel = carry
        m = jnp.max(work, axis=-1, keepdims=True)
        idx = jnp.min(jnp.where(work == m, blk_f, 1e9), axis=-1, keepdims=True)
        hit = blk_f == idx
        return jnp.where(hit, -jnp.inf, work), jnp.where(hit, 1.0, sel)

    work0 = jnp.concatenate(scores, axis=0)
    _, sel = lax.fori_loop(0, n_pick, pick_one, (work0, jnp.zeros(work0.shape, F32)))
    for k in range(NSA_KV_HEADS):
        sel_ref[0, :, k * nbp:(k + 1) * nbp] = sel[k * tq:(k + 1) * tq]


def _nbp(n_slc):
    return 64 if n_slc <= 64 else -(-n_slc // LANES) * LANES


def cmp_attn_select(q, gate, kc, vc, tq, pos0, n_slc):
    b, l, dq = q.shape
    ncp = kc.shape[1]
    nbp = _nbp(n_slc)
    assert l % tq == 0
    tok = lambda w: pl.BlockSpec((1, tq, w), lambda i, j: (i, j, 0))
    whole = pl.BlockSpec((1, ncp, NSA_KV_COLS), lambda i, j: (i, 0, 0))
    return pl.pallas_call(
        functools.partial(_cmp_attn_body, tq=tq, pos0=pos0, n_slc=n_slc, nbp=nbp),
        grid=(b, l // tq), in_specs=[tok(dq), tok(LANES), whole, whole],
        out_specs=[tok(dq), tok(NSA_KV_HEADS * nbp)],
        out_shape=[jax.ShapeDtypeStruct((b, l, dq), F32), jax.ShapeDtypeStruct((b, l, NSA_KV_HEADS * nbp), F32)],
        compiler_params=_cparams(("parallel", "parallel")), name="cmp_attn_select",
    )(q, gate, kc, vc)


def _nsa_attn_body(q_ref, gate_ref, kv_ref, *rest, mode, tq, tk, qpos0, kpos0, gate_off, nbp):
    if mode == "slc":
        sel_ref, o_ref, qs_ref, m_ref, l_ref, acc_ref = rest
    else:
        o_ref, qs_ref, m_ref, l_ref, acc_ref = rest
    qi = pl.program_id(1)
    hd = NSA_HEAD_DIM
    nk = kv_ref.shape[1] // tk
    qmin = qpos0 + qi * tq
    t_col = qmin + _iota2((tq, 1), 0)
    for k in range(NSA_KV_HEADS):
        qs_ref[k] = jnp.concatenate(
            [q_ref[0, :, (k * NSA_GROUP + g) * hd:(k * NSA_GROUP + g + 1) * hd] * (hd ** -0.5)
             for g in range(NSA_GROUP)], axis=0)
    m_ref[...] = jnp.full(m_ref.shape, NEG_INF, F32)
    l_ref[...] = jnp.zeros(l_ref.shape, F32)
    acc_ref[...] = jnp.zeros(acc_ref.shape, F32)
    hi = jnp.minimum(nk - 1, (qmin + tq - 1 - kpos0) // tk)
    if mode == "slc":
        lo = 0
    else:
        lo = jnp.maximum(qmin - (WINDOW - 1) - kpos0, 0) // tk

    def step(ki, carry):
        k0 = pl.multiple_of(ki * tk, tk)
        kv = kv_ref[0, pl.ds(k0, tk), :]
        kpos = kpos0 + k0 + _iota2((1, tk), 1)
        if mode == "slc":
            expand = jnp.where(_iota2((nbp, tk), 0) == lax.shift_right_logical(k0 + _iota2((nbp, tk), 1), 6),
                               1.0, 0.0).astype(BF16)
            base = kpos <= t_col
        else:
            dist = t_col - kpos
            base = (dist >= 0) & (dist < WINDOW)
        for k in range(NSA_KV_HEADS):
            kk = kv[:, k * hd:(k + 1) * hd].astype(BF16)
            vv = kv[:, NSA_KV_COLS + k * hd:NSA_KV_COLS + (k + 1) * hd].astype(BF16)
            s = _dot_nt(qs_ref[k].astype(BF16), kk)
            if mode == "slc":
                mask = base & (_dot(sel_ref[0, :, k * nbp:(k + 1) * nbp].astype(BF16), expand) > 0.5)
            else:
                mask = base
            for g in range(NSA_GROUP):
                r = k * NSA_GROUP + g
                s_g = jnp.where(mask, s[g * tq:(g + 1) * tq], NEG_INF)
                m_old = m_ref[r]
                m_new = jnp.maximum(m_old, jnp.max(s_g, axis=-1, keepdims=True))
                alpha = jnp.exp(m_old - m_new)
                p = jnp.where(mask, jnp.exp(s_g - m_new), 0.0)
                l_ref[r] = alpha * l_ref[r] + jnp.sum(p, axis=-1, keepdims=True)
                acc_ref[r] = alpha * acc_ref[r] + _dot(p.astype(BF16), vv)
                m_ref[r] = m_new
        return carry

    lax.fori_loop(lo, hi + 1, step, 0)
    gate = jax.nn.sigmoid(gate_ref[0])
    for r in range(NSA_HEADS):
        l = l_ref[r]
        o = acc_ref[r] / jnp.where(l > 0.0, l, 1.0)
        o_ref[0, :, r * hd:(r + 1) * hd] = o * gate[:, gate_off + r:gate_off + r + 1]


def nsa_attn(mode, q, gate, kv, sel, tq, tk, qpos0, kpos0):
    b, l, dq = q.shape
    nkeys = kv.shape[1]
    assert l % tq == 0 and nkeys % tk == 0
    tok = lambda w: pl.BlockSpec((1, tq, w), lambda i, j: (i, j, 0))
    in_specs = [tok(dq), tok(LANES), pl.BlockSpec((1, nkeys, kv.shape[2]), lambda i, j: (i, 0, 0))]
    args = [q, gate, kv]
    nbp = 0
    if mode == "slc":
        nbp = sel.shape[2] // NSA_KV_HEADS
        in_specs.append(tok(sel.shape[2]))
        args.append(sel)
    return pl.pallas_call(
        functools.partial(_nsa_attn_body, mode=mode, tq=tq, tk=tk, qpos0=qpos0, kpos0=kpos0,
                          gate_off=NSA_HEADS * (1 if mode == "slc" else 2), nbp=nbp),
        grid=(b, l // tq), in_specs=in_specs, out_specs=tok(dq),
        out_shape=jax.ShapeDtypeStruct((b, l, dq), F32),
        scratch_shapes=[pltpu.VMEM((NSA_KV_HEADS, NSA_GROUP * tq, NSA_HEAD_DIM), F32),
                        pltpu.VMEM((NSA_HEADS, tq, 1), F32), pltpu.VMEM((NSA_HEADS, tq, 1), F32),
                        pltpu.VMEM((NSA_HEADS, tq, NSA_HEAD_DIM), F32)],
        compiler_params=_cparams(("parallel", "parallel")), name="nsa_attn_" + mode,
    )(*args)


def _slc_paged_body(pt_ref, q_ref, gate_ref, selgrp_ref, selnew_ref, kvnew_ref, *rest, n_pg, tq, l_new):
    page_refs = rest[:n_pg]
    o_ref, qbd_ref, m_ref, l_ref, acc_ref = rest[n_pg:]
    gi = pl.program_id(1)
    hd = NSA_HEAD_DIM
    rows = NSA_HEADS * tq

    @pl.when(gi == 0)
    def _():
        qbd_ref[...] = jnp.zeros(qbd_ref.shape, F32)
        for h in range(NSA_HEADS):
            k = h // NSA_GROUP
            qbd_ref[h * tq:(h + 1) * tq, k * hd:(k + 1) * hd] = q_ref[0, :, h * hd:(h + 1) * hd] * (hd ** -0.5)
        m_ref[...] = jnp.full(m_ref.shape, NEG_INF, F32)
        l_ref[...] = jnp.zeros(l_ref.shape, F32)
        acc_ref[...] = jnp.zeros(acc_ref.shape, F32)

    qbd = qbd_ref[...].astype(BF16)

    def update(s, vv, v_transposed):
        m_old = m_ref[...]
        m_new = jnp.maximum(m_old, jnp.max(s, axis=-1, keepdims=True))
        alpha = jnp.exp(m_old - m_new)
        p = jnp.exp(s - m_new)
        l_ref[...] = alpha * l_ref[...] + jnp.sum(p, axis=-1, keepdims=True)
        acc_ref[...] = alpha * acc_ref[...] + (_dot_nt if v_transposed else _dot)(p.astype(BF16), vv)
        m_ref[...] = m_new

    page = page_refs[0].shape[2]
    wide = n_pg * page
    nblk = wide // SLC_BLOCK
    kt = jnp.concatenate([r[0, 0:NSA_KV_COLS, :] for r in page_refs], axis=1).astype(BF16)
    vt = jnp.concatenate([r[0, NSA_KV_COLS:, :] for r in page_refs], axis=1).astype(BF16)
    expand = jnp.where(_iota2((nblk, wide), 0) == lax.shift_right_logical(_iota2((nblk, wide), 1), 6),
                       1.0, 0.0).astype(BF16)
    selb = ((1.0 - selgrp_ref[0, 0]) * MASKED).astype(BF16)
    update(_dot(qbd, kt) + _dot(selb, expand), vt, True)

    @pl.when(gi == pl.num_programs(1) - 1)
    def _():
        kvn = kvnew_ref[0]
        nr = kvn.shape[0]
        j = _iota2((rows, nr), 1)
        iq = jnp.bitwise_and(_iota2((rows, nr), 0), tq - 1)
        mask = (selnew_ref[0] > 0.5) & (j <= iq) & (j < l_new)
        s_new = _dot_nt(qbd, kvn[:, :NSA_KV_COLS].astype(BF16)) + jnp.where(mask, 0.0, MASKED)
        update(s_new, kvn[:, NSA_KV_COLS:].astype(BF16), False)
        gate = jax.nn.sigmoid(gate_ref[0])
        l = l_ref[...]
        o = acc_ref[...] / jnp.where(l > 0.0, l, 1.0)
        for h in range(NSA_HEADS):
            k = h // NSA_GROUP
            o_ref[0, :, h * hd:(h + 1) * hd] = (o[h * tq:(h + 1) * tq, k * hd:(k + 1) * hd]
                                                * gate[:, NSA_HEADS + h:NSA_HEADS + h + 1])


def slc_paged(q, gate, selrows, kv_new, cache_t, page_table, n_pg, l_new, p0):
    b, tq, dq = q.shape
    npool, w, page = cache_t.shape
    n_pages = page_table.shape[1]
    rows = NSA_HEADS * tq
    n_grp = n_pages // n_pg
    nblk = n_pg * page // SLC_BLOCK
    assert n_pages % n_pg == 0 and p0 % SLC_BLOCK == 0 and l_new <= SLC_BLOCK and p0 == n_pages * page
    assert tq & (tq - 1) == 0
    blk_new = p0 // SLC_BLOCK
    sel_grp = selrows[:, :, :blk_new].reshape(b, rows, n_grp, nblk).transpose(0, 2, 1, 3)
    sel_new = selrows[:, :, blk_new:blk_new + 1]
    per_b = lambda shape: pl.BlockSpec((1,) + shape, lambda bi, gi, pt: (bi, 0, 0))
    page_spec = lambda i: pl.BlockSpec((1, w, page), lambda bi, gi, pt: (pt[bi, gi * n_pg + i], 0, 0))
    grid_spec = pltpu.PrefetchScalarGridSpec(
        num_scalar_prefetch=1, grid=(b, n_grp),
        in_specs=[per_b((tq, dq)), per_b((tq, LANES)),
                  pl.BlockSpec((1, 1, rows, nblk), lambda bi, gi, pt: (bi, gi, 0, 0)),
                  per_b((rows, 1)), per_b((kv_new.shape[1], w))]
        + [page_spec(i) for i in range(n_pg)],
        out_specs=per_b((tq, dq)),
        scratch_shapes=[pltpu.VMEM((rows, NSA_KV_COLS), F32), pltpu.VMEM((rows, 1), F32),
                        pltpu.VMEM((rows, 1), F32), pltpu.VMEM((rows, NSA_KV_COLS), F32)])
    return pl.pallas_call(
        functools.partial(_slc_paged_body, n_pg=n_pg, tq=tq, l_new=l_new),
        grid_spec=grid_spec, out_shape=jax.ShapeDtypeStruct((b, tq, dq), F32),
        compiler_params=_cparams(("parallel", "arbitrary")), name="slc_paged",
    )(page_table, q, gate, sel_grp, sel_new, kv_new, *([cache_t] * n_pg))


def _cmp_attn_t_body(qt_ref, gt_ref, kc_ref, vct_ref, oct_ref, selt_ref, *, tq, pos0, n_slc, nbp):
    qi = pl.program_id(1)
    ncp = kc_ref.shape[1]
    hd = NSA_HEAD_DIM
    t_row = pos0 + qi * tq + _iota2((1, tq), 1)
    cbias = jnp.where((_iota2((ncp, 1), 0) * CMP_STRIDE + (CMP_LEN - 1)) <= t_row, 0.0, MASKED)
    kc = kc_ref[0].astype(BF16)
    vct = vct_ref[0].astype(BF16)
    gate = jax.nn.sigmoid(gt_ref[0, 0:NSA_HEADS, :])
    ss = _iota2((nbp, ncp), 0) * SLC_BLOCK
    nn = _iota2((nbp, ncp), 1) * CMP_STRIDE
    overlap_t = jnp.where((nn < ss + SLC_BLOCK) & (nn + CMP_LEN > ss), 1.0, 0.0).astype(BF16)
    blk = _iota2((nbp, 1), 0)
    blk_f = blk.astype(F32)
    cur = lax.shift_right_logical(t_row, 6)
    forced = (blk == 0) | (blk == cur) | (blk == cur - 1)
    bonus = jnp.where(forced, FORCE_BONUS, 0.0)
    causal_blk = blk * SLC_BLOCK <= t_row
    n_pick = min(SLC_TOPN, n_slc)

    scores = []
    for k in range(NSA_KV_HEADS):
        kc_k = kc[:, k * hd:(k + 1) * hd]
        vct_k = vct[k * hd:(k + 1) * hd, :]
        hs = [k * NSA_GROUP + g for g in range(NSA_GROUP)]
        q_l = [(qt_ref[0, h * hd:(h + 1) * hd, :] * (hd ** -0.5 * LOG2E)).astype(BF16) for h in hs]
        s_l = [_dot(kc_k, q_h) + cbias for q_h in q_l]
        e_l = [jnp.exp2(s - jnp.maximum(jnp.max(s, axis=0, keepdims=True), NEG_INF)) for s in s_l]
        den_l = [jnp.sum(e, axis=0, keepdims=True) for e in e_l]
        p_l = [e * (1.0 / jnp.where(den > 0.0, den, 1.0)) for e, den in zip(e_l, den_l)]
        o_l = [_dot(vct_k, p.astype(BF16)) for p in p_l]
        for h, o in zip(hs, o_l):
            oct_ref[0, h * hd:(h + 1) * hd, :] = o * gate[h:h + 1, :]
        psum = (p_l[0] + p_l[1]) + (p_l[2] + p_l[3])
        ph, plo = _split2(psum)
        imp = _dot(overlap_t, ph) + _dot(overlap_t, plo)
        score = jnp.where(causal_blk, imp + bonus, NEG_INF)
        scores.append(jnp.where(blk < n_slc, score, -jnp.inf))

    def pick_one(_, work):
        m = jnp.max(work, axis=0, keepdims=True)
        idx = jnp.min(jnp.where(work == m, blk_f, 1e9), axis=0, keepdims=True)
        return jnp.where(blk_f == idx, -jnp.inf, work)

    work = lax.fori_loop(0, n_pick, pick_one, jnp.concatenate(scores, axis=1))
    sel = jnp.where((work == -jnp.inf) & (blk < n_slc), 1.0, 0.0)
    for k in range(NSA_KV_HEADS):
        selt_ref[0, k * nbp:(k + 1) * nbp, :] = sel[:, k * tq:(k + 1) * tq]


def cmp_attn_select_t(qt, gate_t, kc, vct, b, tq, pos0, n_slc):
    nt, dq, _ = qt.shape
    nq = nt // b
    ncp = kc.shape[1]
    nbp = _nbp(n_slc)
    tile = lambda rows: pl.BlockSpec((1, rows, tq), lambda i, j: (i * nq + j, 0, 0))
    return pl.pallas_call(
        functools.partial(_cmp_attn_t_body, tq=tq, pos0=pos0, n_slc=n_slc, nbp=nbp),
        grid=(b, nq),
        in_specs=[tile(dq), tile(LANES), pl.BlockSpec((1, ncp, NSA_KV_COLS), lambda i, j: (i, 0, 0)),
                  pl.BlockSpec((1, NSA_KV_COLS, ncp), lambda i, j: (i, 0, 0))],
        out_specs=[tile(dq), tile(NSA_KV_HEADS * nbp)],
        out_shape=[jax.ShapeDtypeStruct((nt, dq, tq), F32), jax.ShapeDtypeStruct((nt, NSA_KV_HEADS * nbp, tq), F32)],
        compiler_params=_cparams(("parallel", "parallel")), name="cmp_attn_select_t",
    )(qt, gate_t, kc, vct)


def _nsa_attn_t_body(qt_ref, gt_ref, kv_ref, vt_ref, *rest, mode, tq, tk, gate_off, nbp):
    if mode == "slc":
        selt_ref, o_ref, qs_ref, m_ref, l_ref, acc_ref = rest
    else:
        o_ref, qs_ref, m_ref, l_ref, acc_ref = rest
    qi = pl.program_id(1)
    hd = NSA_HEAD_DIM
    nk = kv_ref.shape[1] // tk
    qmin = qi * tq
    t_row = qmin + _iota2((1, tq), 1)
    for r in range(NSA_HEADS):
        qs_ref[r, 0:hd, :] = (qt_ref[0, r * hd:(r + 1) * hd, :] * (hd ** -0.5 * LOG2E)).astype(BF16)
        if mode == "slc":
            k = r // NSA_GROUP
            qs_ref[r, hd:hd + nbp, :] = ((1.0 - selt_ref[0, k * nbp:(k + 1) * nbp, :]) * MASKED).astype(BF16)
    m_ref[...] = jnp.full(m_ref.shape, NEG_INF, F32)
    l_ref[...] = jnp.zeros(l_ref.shape, F32)
    acc_ref[...] = jnp.zeros(acc_ref.shape, F32)
    hi = jnp.minimum(nk - 1, (qmin + tq - 1) // tk)

    heads_per_stage = NSA_HEADS

    def tile(ki, positional):
        k0 = pl.multiple_of(ki * tk, tk)
        krows = kv_ref[0, pl.ds(k0, tk), 0:NSA_KV_COLS].astype(BF16)
        vt = vt_ref[ki].astype(BF16)
        kpos = k0 + _iota2((tk, 1), 0)
        if mode == "slc":
            expand_t = jnp.where(_iota2((tk, nbp), 1) == lax.shift_right_logical(k0 + _iota2((tk, nbp), 0), 6),
                                 1.0, 0.0).astype(BF16)
            cbias = jnp.where(kpos <= t_row, 0.0, MASKED) if positional else None
        else:
            dist = t_row - kpos
            cbias = jnp.where((dist >= 0) & (dist < WINDOW), 0.0, MASKED)
        k_l, vt_l = [], []
        for k in range(NSA_KV_HEADS):
            k_k = krows[:, k * hd:(k + 1) * hd]
            k_l.append(jnp.concatenate([k_k, expand_t], axis=1) if mode == "slc" else k_k)
            vt_l.append(vt[k * hd:(k + 1) * hd, :])
        for r0 in range(0, NSA_HEADS, heads_per_stage):
            rs = list(range(r0, r0 + heads_per_stage))
            s_l = [_dot(k_l[r // NSA_GROUP], qs_ref[r]) for r in rs]
            if cbias is not None:
                s_l = [s + cbias for s in s_l]
            m_old = [m_ref[r] for r in rs]
            m_new = [jnp.maximum(mo, jnp.max(s, axis=0, keepdims=True)) for mo, s in zip(m_old, s_l)]
            p_l = [jnp.exp2(s - mn) for s, mn in zip(s_l, m_new)]
            pv_l = [_dot(vt_l[r // NSA_GROUP], p.astype(BF16)) for r, p in zip(rs, p_l)]
            for r, mo, mn, p, pv in zip(rs, m_old, m_new, p_l, pv_l):
                alpha = jnp.exp2(mo - mn)
                l_ref[r] = alpha * l_ref[r] + jnp.sum(p, axis=0, keepdims=True)
                acc_ref[r] = alpha * acc_ref[r] + pv
                m_ref[r] = mn

    def step(ki, carry):
        tile(ki, mode != "slc")
        return carry

    if mode == "slc":
        lax.fori_loop(0, hi, step, 0)
        tile(hi, True)
    else:
        lax.fori_loop(jnp.maximum(qmin - (WINDOW - 1), 0) // tk, hi + 1, step, 0)
    gate = jax.nn.sigmoid(gt_ref[0, gate_off:gate_off + NSA_HEADS, :])
    for r in range(NSA_HEADS):
        l = l_ref[r]
        o_ref[0, r * hd:(r + 1) * hd, :] = acc_ref[r] * (gate[r:r + 1, :] / jnp.where(l > 0.0, l, 1.0))


def nsa_attn_t(mode, qt, gate_t, kv, vt, selt, b, tq, tk):
    nt, dq, _ = qt.shape
    nq = nt // b
    l = kv.shape[1]
    nk = l // tk
    assert l % tk == 0 and vt.shape == (b * nk, NSA_KV_COLS, tk) and (mode != "slc" or tq == tk)
    tile = lambda rows: pl.BlockSpec((1, rows, tq), lambda i, j: (i * nq + j, 0, 0))
    in_specs = [tile(dq), tile(LANES), pl.BlockSpec((1, l, kv.shape[2]), lambda i, j: (i, 0, 0)),
                pl.BlockSpec((nk, NSA_KV_COLS, tk), lambda i, j: (i, 0, 0))]
    args = [qt, gate_t, kv, vt]
    nbp = 0
    if mode == "slc":
        nbp = selt.shape[1] // NSA_KV_HEADS
        in_specs.append(tile(selt.shape[1]))
        args.append(selt)
    return pl.pallas_call(
        functools.partial(_nsa_attn_t_body, mode=mode, tq=tq, tk=tk,
                          gate_off=NSA_HEADS * (1 if mode == "slc" else 2), nbp=nbp),
        grid=(b, nq), in_specs=in_specs, out_specs=tile(dq),
        out_shape=jax.ShapeDtypeStruct((nt, dq, tq), F32),
        scratch_shapes=[pltpu.VMEM((NSA_HEADS, NSA_HEAD_DIM + nbp, tq), BF16),
                        pltpu.VMEM((NSA_HEADS, 1, tq), F32), pltpu.VMEM((NSA_HEADS, 1, tq), F32),
                        pltpu.VMEM((NSA_HEADS, NSA_HEAD_DIM, tq), F32)],
        compiler_params=_cparams(("parallel", "parallel")), name="nsa_attn_t_" + mode,
    )(*args)


def _prep_nsa_weights(w_in_nsa):
    w = w_in_nsa.astype(BF16)
    dq = NSA_HEADS * NSA_HEAD_DIM
    kvw = 2 * NSA_KV_COLS
    gate = jnp.pad(w[:, dq + 3 * kvw:], ((0, 0), (0, LANES - 3 * NSA_HEADS)))
    return [w[:, :dq], w[:, dq:dq + kvw], w[:, dq + kvw:dq + 2 * kvw], w[:, dq + 2 * kvw:dq + 3 * kvw], gate]


def _kv5(a):
    return a.reshape(a.shape[0], a.shape[1], 2, NSA_KV_HEADS, NSA_HEAD_DIM)


def nsa_prompt(x, p, tm):
    b, l, d = x.shape
    assert l % tm == 0
    wq, wc, ws, ww, wg = p["w_nsa"]
    v_rows = slice(NSA_KV_COLS, 2 * NSA_KV_COLS)
    wts = [wq.T, ws.T[v_rows], ww.T[v_rows], wg.T]
    kvc, kvs, kvw, qt, vst, vwt, gt = norm_proj(x.reshape(b * l, d), p["norm_mix"][1], [wc, ws, ww], tm, wts)
    r3 = lambda a: a.reshape(b, l, a.shape[-1])
    kvc, kvs, kvw = r3(kvc), r3(kvs), r3(kvw)
    lt = nsa_lt_dense(kvc, p["lt_wk"], p["lt_wv"], 256)
    kc, _, vct = cmp_kv(lt, p)
    n_slc = -(-l // SLC_BLOCK)
    o_c, selt = cmp_attn_select_t(qt, gt, kc, vct, b, tm, 0, n_slc)
    o_s = nsa_attn_t("slc", qt, gt, kvs, vst, selt, b, tm, tm)
    o_w = nsa_attn_t("win", qt, gt, kvw, vwt, None, b, tm, tm)
    keep = min(WINDOW, l)
    return [o_c, o_s, o_w], _kv5(kvc), _kv5(kvs), _kv5(kvw[:, l - keep:])


def nsa_sample(x, cache_cmp, cache_slc, past_win, page_table, p, n_pg):
    b, lq, d = x.shape
    tq = 8
    assert lq <= tq
    npool, page = cache_cmp.shape[:2]
    w = 2 * NSA_KV_COLS
    p0 = page_table.shape[1] * page
    xp = _pad_time(x, tq).reshape(b * tq, d)
    q, kvc, kvs, kvw, gate = norm_proj(xp, p["norm_mix"][1], p["w_nsa"], b * tq)
    r3 = lambda a: a.reshape(b, tq, a.shape[-1])
    q, kvc, kvs, kvw, gate = r3(q), r3(kvc), r3(kvs), r3(kvw), r3(gate)
    pages_t = lambda c: c.reshape(npool, page, w).transpose(0, 2, 1)
    lt = nsa_lt_paged(pages_t(cache_cmp), page_table, p["lt_wk"], p["lt_wv"], n_pg)
    kc, vc, _ = cmp_kv(lt, p)
    n_slc = -(-(p0 + lq) // SLC_BLOCK)
    o_c, sel = cmp_attn_select(q, gate, kc, vc, tq, p0, n_slc)
    nbp = sel.shape[2] // NSA_KV_HEADS
    selrows = sel.reshape(b, tq, NSA_KV_HEADS, 1, nbp).transpose(0, 2, 3, 1, 4)
    selrows = jnp.broadcast_to(selrows, (b, NSA_KV_HEADS, NSA_GROUP, tq, nbp)).reshape(b, NSA_HEADS * tq, nbp)
    o_s = slc_paged(q, gate, selrows, kvs, pages_t(cache_slc), page_table, n_pg, lq, p0)
    n_past = past_win.shape[1]
    win_all = jnp.concatenate([past_win.reshape(b, n_past, w), kvw[:, :lq]], axis=1)
    nkeys = -(-(n_past + lq) // 8) * 8
    o_w = nsa_attn("win", q, gate, _pad_time(win_all, nkeys), None, tq, nkeys, p0, p0 - n_past)
    keep = min(WINDOW, n_past + lq)
    flat = lambda a: a[:, :lq].reshape(b * lq, a.shape[-1])
    return ([flat(o_c), flat(o_s), flat(o_w)], _kv5(kvc[:, :lq]), _kv5(kvs[:, :lq]),
            _kv5(win_all[:, n_past + lq - keep:]))


def _trunk(x, ssd_state0, ssd_conv0, gdn_state0, gdn_conv0, nsa_fn, p, tm):
    b, l, d = x.shape
    t = b * l
    tm = min(tm, t)
    y_ssd, y_gdn, ssd_state, ssd_conv, gdn_state, gdn_conv = layer0_mixer(
        x, ssd_state0, ssd_conv0, gdn_state0, gdn_conv0, p, tm)
    x1 = mix_mlp(x.reshape(t, d), [[y_ssd], [y_gdn]], p["w_out_ab"], p["norm_mlp"][0], p["w_up"][0],
                 p["w_down"][0], None, tm)
    branches, cmp_rows, slc_rows, win_rows = nsa_fn(x1.reshape(b, l, d))
    y = mix_mlp(x1, [branches], [p["w_out_nsa"]], p["norm_mlp"][1], p["w_up"][1], p["w_down"][1],
                p["norm_final"], tm, transposed=branches[0].ndim == 3)
    return y.reshape(b, l, d), ssd_state, ssd_conv, gdn_state, gdn_conv, cmp_rows, slc_rows, win_rows


def kernel(x_prompt, x_sample, state_ssd, state_ssd_conv, state_gdn, state_gdn_conv, cache_cmp_kv, cache_slc_kv,
           state_win_kv, page_table, norm_mix, norm_mlp, norm_final, w_in_ab, ssd_conv_w, ssd_conv_b, ssd_dt_bias,
           ssd_a_log, ssd_d, ssd_norm_w, gdn_conv_w, gdn_dt_bias, gdn_a_log, gdn_norm_w, w_out_ab, w_in_nsa,
           cmp_pe_k, cmp_w1_k, cmp_b1_k, cmp_w2_k, cmp_pe_v, cmp_w1_v, cmp_b1_v, cmp_w2_v, w_out_nsa, w_up, w_down):
    wo = w_out_ab.astype(BF16)
    p = dict(
        norm_mix=norm_mix, norm_mlp=norm_mlp, norm_final=norm_final, w_ab=_prep_ab_weights(w_in_ab),
        ssd_conv_w=ssd_conv_w, ssd_conv_b=ssd_conv_b, ssd_dt_bias=ssd_dt_bias, ssd_a_log=ssd_a_log, ssd_d=ssd_d,
        ssd_norm_w=ssd_norm_w, gdn_conv_w=gdn_conv_w, gdn_dt_bias=gdn_dt_bias, gdn_a_log=gdn_a_log,
        gdn_norm_w=gdn_norm_w, w_out_ab=[wo[:SSD_INNER], wo[SSD_INNER:]], w_nsa=_prep_nsa_weights(w_in_nsa),
        lt_wk=_lt_weights(cmp_w1_k), lt_wv=_lt_weights(cmp_w1_v),
        cmp_pe_k=cmp_pe_k, cmp_w1_k=cmp_w1_k, cmp_b1_k=cmp_b1_k, cmp_w2_k=cmp_w2_k,
        cmp_pe_v=cmp_pe_v, cmp_w1_v=cmp_w1_v, cmp_b1_v=cmp_b1_v, cmp_w2_v=cmp_w2_v,
        w_out_nsa=w_out_nsa.astype(BF16), w_up=w_up.astype(BF16), w_down=w_down.astype(BF16))
    bp = x_prompt.shape[0]
    zeros = lambda *s: jnp.zeros(s, F32)
    tm = 256
    out_p = _trunk(x_prompt, zeros(bp, SSD_HEADS, SSD_HEADDIM, SSD_STATE), zeros(bp, CONV_W - 1, SSD_CONV_DIM),
                   zeros(bp, GDN_V_HEADS, GDN_HEAD_DIM, GDN_HEAD_DIM), zeros(bp, CONV_W - 1, GDN_CONV_DIM),
                   lambda x1: nsa_prompt(x1, p, tm), p, tm)
    out_s = _trunk(x_sample, state_ssd, state_ssd_conv, state_gdn, state_gdn_conv,
                   lambda x1: nsa_sample(x1, cache_cmp_kv, cache_slc_kv, state_win_kv, page_table, p, 16), p, tm)
    return tuple(v for pair in zip(out_p, out_s) for v in pair)
```

```python
import functools

import jax
import jax.numpy as jnp
from jax import lax
from jax.experimental import pallas as pl
from jax.experimental.pallas import tpu as pltpu

F32 = jnp.float32
BF16 = jnp.bfloat16

RMS_EPS = 1e-6
NEG_INF = -1e30
MASKED = 2.0 * NEG_INF
FORCE_BONUS = 1e4
LOG2E = 1.4426950408889634
CONV_W = 4
CHUNK = 64
SSD_HEADS, SSD_HEADDIM, SSD_STATE, SSD_GROUPS = 16, 64, 128, 2
SSD_INNER = SSD_HEADS * SSD_HEADDIM
SSD_CONV_DIM = SSD_INNER + 2 * SSD_GROUPS * SSD_STATE
GDN_QK_HEADS, GDN_V_HEADS, GDN_HEAD_DIM = 4, 8, 128
GDN_QK_DIM = GDN_QK_HEADS * GDN_HEAD_DIM
GDN_V_DIM = GDN_V_HEADS * GDN_HEAD_DIM
GDN_CONV_DIM = 2 * GDN_QK_DIM + GDN_V_DIM
NSA_HEADS, NSA_KV_HEADS, NSA_HEAD_DIM = 16, 4, 64
NSA_GROUP = NSA_HEADS // NSA_KV_HEADS
NSA_KV_COLS = NSA_KV_HEADS * NSA_HEAD_DIM
CMP_STRIDE, CMP_LEN, CMP_HIDDEN = 16, 32, 64
SLC_BLOCK, SLC_TOPN, WINDOW = 64, 16, 512
PAGE_SIZE = 128
LANES = 128
VMEM_LIMIT = 56 * 1024 * 1024


def _cparams(sem):
    return pltpu.CompilerParams(dimension_semantics=sem, vmem_limit_bytes=VMEM_LIMIT)


def _dot(a, b):
    return jnp.dot(a, b, preferred_element_type=F32)


def _dot_nt(a, b):
    return lax.dot_general(a, b, (((1,), (1,)), ((), ())), preferred_element_type=F32)


def _dot_tn(a, b):
    return lax.dot_general(a, b, (((0,), (0,)), ((), ())), preferred_element_type=F32)


def _split2(x):
    hi = x.astype(BF16)
    lo = (x - hi.astype(F32)).astype(BF16)
    return hi, lo


def _split3(x):
    hi = x.astype(BF16)
    r = x - hi.astype(F32)
    mid = r.astype(BF16)
    lo = (r - mid.astype(F32)).astype(BF16)
    return hi, mid, lo


def _mm3(a, b):
    ah, al = _split2(a)
    bh, bl = _split2(b)
    return _dot(ah, bh) + (_dot(ah, bl) + _dot(al, bh))


def _rms_unit(x):
    return x * lax.rsqrt(jnp.mean(x * x, axis=-1, keepdims=True) + RMS_EPS)


def _silu(x):
    return x * jax.nn.sigmoid(x)


def _softplus(x):
    return jnp.maximum(x, 0.0) + jnp.log1p(jnp.exp(-jnp.abs(x)))


def _iota2(shape, axis):
    return lax.broadcasted_iota(jnp.int32, shape, axis)


def _norm_proj_body(x_ref, g_ref, *refs, n_out, n_t):
    n_w = n_out + n_t
    w_refs, o_refs = refs[:n_w], refs[n_w:]
    hn = (_rms_unit(x_ref[...]) * g_ref[...]).astype(BF16)
    for w_ref, o_ref in zip(w_refs[:n_out], o_refs[:n_out]):
        n = w_ref.shape[1]
        for c0 in range(0, n, 512):
            c1 = min(n, c0 + 512)
            o_ref[:, c0:c1] = _dot(hn, w_ref[:, c0:c1])
    for w_ref, o_ref in zip(w_refs[n_out:], o_refs[n_out:]):
        n = w_ref.shape[0]
        for c0 in range(0, n, 512):
            c1 = min(n, c0 + 512)
            o_ref[0, c0:c1, :] = _dot_nt(w_ref[c0:c1, :], hn)


def norm_proj(x, g, ws, tm, wts=()):
    t, d = x.shape
    assert t % tm == 0
    in_specs = [pl.BlockSpec((tm, d), lambda i: (i, 0)), pl.BlockSpec((1, d), lambda i: (0, 0))]
    in_specs += [pl.BlockSpec(w.shape, lambda i: (0, 0), pipeline_mode=pl.Buffered(1)) for w in (*ws, *wts)]
    out_specs = [pl.BlockSpec((tm, w.shape[1]), lambda i: (i, 0)) for w in ws]
    out_specs += [pl.BlockSpec((1, w.shape[0], tm), lambda i: (i, 0, 0)) for w in wts]
    out_shape = [jax.ShapeDtypeStruct((t, w.shape[1]), F32) for w in ws]
    out_shape += [jax.ShapeDtypeStruct((t // tm, w.shape[0], tm), F32) for w in wts]
    return pl.pallas_call(
        functools.partial(_norm_proj_body, n_out=len(ws), n_t=len(wts)),
        grid=(t // tm,), in_specs=in_specs, out_specs=out_specs, out_shape=out_shape,
        compiler_params=_cparams(("parallel",)), name="norm_proj",
    )(x, g.reshape(1, d), *ws, *wts)


def _mix_mlp_body(x_ref, *refs, group_sizes, transposed, final, hc):
    n_y, n_g = sum(group_sizes), len(group_sizes)
    y_refs = refs[:n_y]
    wo_refs = refs[n_y:n_y + n_g]
    g_ref, wu_ref, wd_ref = refs[n_y + n_g:n_y + n_g + 3]
    rest = refs[n_y + n_g + 3:]
    gf_ref = rest[0] if final else None
    o_ref = rest[-1]
    x1 = x_ref[...]
    i = 0
    for gs, wo_ref in zip(group_sizes, wo_refs):
        rd = (lambda r: r[0]) if transposed else (lambda r: r[...])
        y = rd(y_refs[i])
        for r in y_refs[i + 1:i + gs]:
            y = y + rd(r)
        i += gs
        x1 = x1 + (_dot_tn if transposed else _dot)(y.astype(BF16), wo_ref[...])
    hn = (_rms_unit(x1) * g_ref[...]).astype(BF16)
    acc = x1
    hidden = wu_ref.shape[1]
    for c0 in range(0, hidden, hc):
        h = jnp.maximum(_dot(hn, wu_ref[:, c0:c0 + hc]), 0.0)
        acc = acc + _dot((h * h).astype(BF16), wd_ref[c0:c0 + hc, :])
    if final:
        acc = _rms_unit(acc) * gf_ref[...]
    o_ref[...] = acc


def mix_mlp(x, y_groups, w_outs, g, w_up, w_down, g_final, tm, hc=None, transposed=False):
    t, d = x.shape
    hc = hc or w_up.shape[1]
    assert t % tm == 0
    final = g_final is not None
    const = lambda i: (0, 0)
    ys = [y for grp in y_groups for y in grp]
    in_specs = [pl.BlockSpec((tm, d), lambda i: (i, 0))]
    if transposed:
        assert all(y.shape[0] == t // tm and y.shape[2] == tm for y in ys)
        in_specs += [pl.BlockSpec((1, y.shape[1], tm), lambda i: (i, 0, 0)) for y in ys]
    else:
        in_specs += [pl.BlockSpec((tm, y.shape[1]), lambda i: (i, 0)) for y in ys]
    resident = lambda w: pl.BlockSpec(w.shape, const, pipeline_mode=pl.Buffered(1))
    in_specs += [resident(w) for w in w_outs]
    in_specs += [pl.BlockSpec((1, d), const), resident(w_up), resident(w_down)]
    args = [x, *ys, *w_outs, g.reshape(1, d), w_up, w_down]
    if final:
        in_specs.append(pl.BlockSpec((1, d), const))
        args.append(g_final.reshape(1, d))
    return pl.pallas_call(
        functools.partial(_mix_mlp_body, group_sizes=tuple(len(grp) for grp in y_groups), transposed=transposed,
                          final=final, hc=hc),
        grid=(t // tm,), in_specs=in_specs, out_specs=pl.BlockSpec((tm, d), lambda i: (i, 0)),
        out_shape=jax.ShapeDtypeStruct((t, d), F32),
        compiler_params=_cparams(("parallel",)), name="mix_mlp",
    )(*args)


def _causal_conv(x, xp_ref, w_ref, first, conv0_ref):
    c = x.shape[0]

    @pl.when(first)
    def _():
        xp_ref[5:8, :] = conv0_ref[0]

    xp_ref[8:8 + c, :] = x
    y = xp_ref[5:5 + c, :] * w_ref[0:1, :]
    for k in range(1, CONV_W):
        y = y + xp_ref[5 + k:5 + k + c, :] * w_ref[k:k + 1, :]
    xp_ref[5:8, :] = x[c - 3:c, :]
    return y


def _cumsum_pair(col, row):
    c = col.shape[0]
    ri, ci = _iota2((c, c), 0), _iota2((c, c), 1)
    tril = jnp.where(ri >= ci, 1.0, 0.0).astype(BF16)
    triu = jnp.where(ri <= ci, 1.0, 0.0).astype(BF16)
    ch, cm, cl = _split3(col)
    rh, rm, rl = _split3(row)
    cs = _dot(tril, ch) + (_dot(tril, cm) + _dot(tril, cl))
    cst = _dot(rh, triu) + (_dot(rm, triu) + _dot(rl, triu))
    return cs, cst


def _ssd_body(xbc_ref, z_ref, sm_ref, smt_ref, st0_ref, cv0_ref, cw_ref, cb_ref, dtb_r_ref, dtb_c_ref,
              al_r_ref, al_c_ref, dfull_ref, nw_ref, y_ref, st_ref, xp_ref, ybuf_ref, *, l_valid, nsub):
    cidx = pl.program_id(1)
    c = CHUNK
    tb = nsub * c
    n, p = SSD_STATE, SSD_HEADDIM
    heads_per_group = SSD_HEADS // SSD_GROUPS
    heads = range(SSD_HEADS)
    grp = lambda h: h // heads_per_group

    @pl.when(cidx == 0)
    def _():
        st_ref[0] = st0_ref[0]

    conv = _causal_conv(xbc_ref[0], xp_ref, cw_ref, cidx == 0, cv0_ref) + cb_ref[...]
    act = _silu(conv)
    xs = act[:, :SSD_INNER]
    valid_c = (cidx * tb + _iota2((tb, 1), 0)) < l_valid
    dt_all = jnp.where(valid_c, _softplus(sm_ref[0][:, 0:SSD_HEADS] + dtb_r_ref[...]), 0.0)
    a_r, a_c = -jnp.exp(al_r_ref[...]), -jnp.exp(al_c_ref[...])
    causal = _iota2((c, c), 0) >= _iota2((c, c), 1)

    subs = []
    for s in range(nsub):
        rows = slice(s * c, (s + 1) * c)
        valid_r = (cidx * tb + s * c + _iota2((1, c), 1)) < l_valid
        dt = dt_all[rows]
        dtt = jnp.where(valid_r, _softplus(smt_ref[0, s][0:SSD_HEADS, :] + dtb_c_ref[...]), 0.0)
        cs, cst = _cumsum_pair(dt * a_r, dtt * a_c)
        cs_last = cs[c - 1:c, :]
        bm_l = [act[rows, SSD_INNER + g * n:SSD_INNER + (g + 1) * n].astype(BF16) for g in range(SSD_GROUPS)]
        c0 = SSD_INNER + SSD_GROUPS * n
        cm_l = [act[rows, c0 + g * n:c0 + (g + 1) * n].astype(BF16) for g in range(SSD_GROUPS)]
        subs.append(dict(rows=rows, dtt=dtt, cs=cs, cst=cst, ecs=jnp.exp(cs), to_end=jnp.exp(cs_last - cs) * dt,
                         edec=jnp.exp(cs_last), bm=bm_l, cm=cm_l, x=[xs[rows, h * p:(h + 1) * p] for h in heads]))
    for d in subs:
        d["cb"] = [_dot_nt(c_g, b_g) for c_g, b_g in zip(d["cm"], d["bm"])]
    for d in subs:
        d["sc"] = []
        for h in heads:
            seg = d["cs"][:, h:h + 1] - d["cst"][h:h + 1, :]
            decay = jnp.exp(jnp.where(causal, seg, NEG_INF))
            d["sc"].append((d["cb"][grp(h)] * decay * d["dtt"][h:h + 1, :]).astype(BF16))
    for d in subs:
        d["yd"] = [_dot(d["sc"][h], d["x"][h].astype(BF16)) for h in heads]

    s_l = [st_ref[0, h] for h in heads]
    for d in subs:
        yo_l = [_dot_nt(d["cm"][grp(h)], s_l[h].astype(BF16)) * d["ecs"][:, h:h + 1] for h in heads]
        for h in heads:
            ybuf_ref[d["rows"], h * p:(h + 1) * p] = d["yd"][h] + yo_l[h]
        s_l = [s_l[h] * d["edec"][:, h:h + 1]
               + _dot_tn((d["x"][h] * d["to_end"][:, h:h + 1]).astype(BF16), d["bm"][grp(h)]) for h in heads]
    for h in heads:
        st_ref[0, h] = s_l[h]

    y = ybuf_ref[...] + dfull_ref[...] * xs
    yz = y * _silu(z_ref[0])
    gw = SSD_INNER // SSD_GROUPS
    for g in range(SSD_GROUPS):
        seg = yz[:, g * gw:(g + 1) * gw]
        y_ref[0, :, g * gw:(g + 1) * gw] = _rms_unit(seg) * nw_ref[:, g * gw:(g + 1) * gw]


def ssd_mixer(xbc, z, small, small_t, state0, conv0, conv_w, conv_b, dt_bias, a_log, d_skip, norm_w, l_valid):
    b, lp, _ = xbc.shape
    nc = lp // CHUNK
    nsub = 1
    h = SSD_HEADS
    d_full = jnp.repeat(d_skip, SSD_HEADDIM).reshape(1, SSD_INNER)
    const2 = lambda i, j: (0, 0)
    tok = lambda w: pl.BlockSpec((1, nsub * CHUNK, w), lambda i, j: (i, j, 0))
    in_specs = [
        tok(SSD_CONV_DIM), tok(SSD_INNER), tok(LANES),
        pl.BlockSpec((1, nsub, 32, CHUNK), lambda i, j: (i, j, 0, 0)),
        pl.BlockSpec((1, h, SSD_HEADDIM, SSD_STATE), lambda i, j: (i, 0, 0, 0)),
        pl.BlockSpec((1, CONV_W - 1, SSD_CONV_DIM), lambda i, j: (i, 0, 0)),
        pl.BlockSpec((CONV_W, SSD_CONV_DIM), const2), pl.BlockSpec((1, SSD_CONV_DIM), const2),
        pl.BlockSpec((1, h), const2), pl.BlockSpec((h, 1), const2),
        pl.BlockSpec((1, h), const2), pl.BlockSpec((h, 1), const2),
        pl.BlockSpec((1, SSD_INNER), const2), pl.BlockSpec((1, SSD_INNER), const2),
    ]
    out_specs = [tok(SSD_INNER), pl.BlockSpec((1, h, SSD_HEADDIM, SSD_STATE), lambda i, j: (i, 0, 0, 0))]
    out_shape = [jax.ShapeDtypeStruct((b, lp, SSD_INNER), F32),
                 jax.ShapeDtypeStruct((b, h, SSD_HEADDIM, SSD_STATE), F32)]
    return pl.pallas_call(
        functools.partial(_ssd_body, l_valid=l_valid, nsub=nsub),
        grid=(b, nc // nsub), in_specs=in_specs, out_specs=out_specs, out_shape=out_shape,
        scratch_shapes=[pltpu.VMEM((nsub * CHUNK + 8, SSD_CONV_DIM), F32),
                        pltpu.VMEM((nsub * CHUNK, SSD_INNER), F32)],
        compiler_params=_cparams(("parallel", "arbitrary")), name="ssd_mixer",
    )(xbc, z, small, small_t, state0, conv0, conv_w, conv_b.reshape(1, -1),
      dt_bias.reshape(1, h), dt_bias.reshape(h, 1), a_log.reshape(1, h), a_log.reshape(h, 1),
      d_full, norm_w.reshape(1, SSD_INNER))


def _unit_lower_inverses(mats):
    c = mats[0].shape[0]
    ri, ci = _iota2((c, c), 0), _iota2((c, c), 1)
    eye = jnp.where(ri == ci, 1.0, 0.0)
    same_blk = lax.shift_right_logical(ri, 4) == lax.shift_right_logical(ci, 4)
    d = [jnp.where(same_blk, a, 0.0) for a in mats]
    nn = [a - x for a, x in zip(mats, d)]
    d2 = [_mm3(x, x) for x in d]
    d4 = [_mm3(x, x) for x in d2]
    pinv = [eye - x for x in d]
    pinv = [x + _mm3(x, y) for x, y in zip(pinv, d2)]
    d8 = [_mm3(x, x) for x in d4]
    pinv = [x + _mm3(x, y) for x, y in zip(pinv, d4)]
    pinv = [x + _mm3(x, y) for x, y in zip(pinv, d8)]
    e = [_mm3(x, y) for x, y in zip(pinv, nn)]
    e2 = [_mm3(x, x) for x in e]
    q = [eye - x for x in e]
    q = [x + _mm3(x, y) for x, y in zip(q, e2)]
    return [_mm3(x, y) for x, y in zip(q, pinv)]


def _l2norm(x):
    return x * lax.rsqrt(jnp.sum(x * x, axis=-1, keepdims=True) + 1e-6)


def _gdn_body(qkv_ref, z_ref, sm_ref, smt_ref, st0_ref, cv0_ref, cw_ref, dtb_r_ref, dtb_c_ref,
              al_r_ref, al_c_ref, nw_ref, o_ref, st_ref, xp_ref, *, l_valid, nsub):
    cidx = pl.program_id(1)
    c = CHUNK
    tb = nsub * c
    hd = GDN_HEAD_DIM
    nh = GDN_V_HEADS
    heads = range(nh)
    rep = GDN_V_HEADS // GDN_QK_HEADS

    @pl.when(cidx == 0)
    def _():
        st_ref[0] = st0_ref[0]

    act = _silu(_causal_conv(qkv_ref[0], xp_ref, cw_ref, cidx == 0, cv0_ref))
    valid_c = (cidx * tb + _iota2((tb, 1), 0)) < l_valid
    sm = sm_ref[0]
    beta_all = jnp.where(valid_c, jax.nn.sigmoid(sm[:, 16:16 + nh]), 0.0)
    g_all = jnp.where(valid_c, -jnp.exp(al_r_ref[...]) * _softplus(sm[:, 24:24 + nh] + dtb_r_ref[...]), 0.0)
    q_all = [_l2norm(act[:, hq * hd:(hq + 1) * hd]) * (hd ** -0.5) for hq in range(GDN_QK_HEADS)]
    k_all = [_l2norm(act[:, GDN_QK_DIM + hq * hd:GDN_QK_DIM + (hq + 1) * hd]) for hq in range(GDN_QK_HEADS)]
    ri, ci = _iota2((c, c), 0), _iota2((c, c), 1)
    causal = ri >= ci
    strict = ri > ci
    eye = jnp.where(ri == ci, 1.0, 0.0)

    subs = []
    for s in range(nsub):
        rows = slice(s * c, (s + 1) * c)
        valid_r = (cidx * tb + s * c + _iota2((1, c), 1)) < l_valid
        g_r = jnp.where(valid_r, -jnp.exp(al_c_ref[...]) * _softplus(smt_ref[0, s][24:24 + nh, :] + dtb_c_ref[...]),
                        0.0)
        cs, cst = _cumsum_pair(g_all[rows], g_r)
        cs_last = cs[c - 1:c, :]
        d = dict(rows=rows, beta=beta_all[rows], ecs=jnp.exp(cs), e_end=jnp.exp(cs_last - cs),
                 etot=jnp.exp(cs_last), q=[x[rows] for x in q_all], k=[x[rows] for x in k_all], qk=[], a=[])
        for hq in range(GDN_QK_HEADS):
            k16 = d["k"][hq].astype(BF16)
            kk = _dot_nt(k16, k16)
            qk0 = _dot_nt(d["q"][hq].astype(BF16), k16)
            for r in range(rep):
                h = hq * rep + r
                seg = cs[:, h:h + 1] - cst[h:h + 1, :]
                decay = jnp.exp(jnp.where(causal, seg, NEG_INF))
                d["qk"].append((qk0 * decay).astype(BF16))
                d["a"].append(jnp.where(strict, kk * d["beta"][:, h:h + 1] * decay, 0.0))
        subs.append(d)
    tinv_all = _unit_lower_inverses([a for d in subs for a in d["a"]])
    for s, d in enumerate(subs):
        rhs_l = []
        for h in heads:
            b_h = d["beta"][:, h:h + 1]
            v_h = act[d["rows"], 2 * GDN_QK_DIM + h * hd:2 * GDN_QK_DIM + (h + 1) * hd]
            rhs_l.append(jnp.concatenate([v_h * b_h, d["k"][h // rep] * (b_h * d["ecs"][:, h:h + 1])], axis=1))
        d["sol"] = [rhs_l[h] + _dot((tinv_all[s * nh + h] - eye).astype(BF16), rhs_l[h].astype(BF16))
                    for h in heads]

    s_l = [st_ref[0, h] for h in heads]
    for d in subs:
        ecs, rows = d["ecs"], d["rows"]
        s16_l = [x.astype(BF16) for x in s_l]
        o1_l = [_dot((d["q"][h // rep] * ecs[:, h:h + 1]).astype(BF16), s16_l[h]) for h in heads]
        v16_l = [(d["sol"][h][:, :hd] - _dot(d["sol"][h][:, hd:].astype(BF16), s16_l[h])).astype(BF16)
                 for h in heads]
        o_l = [o1_l[h] + _dot(d["qk"][h], v16_l[h]) for h in heads]
        s_l = [s_l[h] * d["etot"][:, h:h + 1]
               + _dot_tn((d["k"][h // rep] * d["e_end"][:, h:h + 1]).astype(BF16), v16_l[h]) for h in heads]
        for h in heads:
            o_ref[0, rows, h * hd:(h + 1) * hd] = ((_rms_unit(o_l[h]) * nw_ref[...])
                                                   * _silu(z_ref[0, rows, h * hd:(h + 1) * hd]))
    for h in heads:
        st_ref[0, h] = s_l[h]


def _chunks_per_step(nc):
    return 2 if nc % 2 == 0 else 1


def gdn_mixer(qkv, z, small, small_t, state0, conv0, conv_w, dt_bias, a_log, norm_w, l_valid):
    b, lp, _ = qkv.shape
    nc = lp // CHUNK
    nsub = _chunks_per_step(nc)
    h = GDN_V_HEADS
    const2 = lambda i, j: (0, 0)
    tok = lambda w: pl.BlockSpec((1, nsub * CHUNK, w), lambda i, j: (i, j, 0))
    in_specs = [
        tok(GDN_CONV_DIM), tok(GDN_V_DIM), tok(LANES),
        pl.BlockSpec((1, nsub, 32, CHUNK), lambda i, j: (i, j, 0, 0)),
        pl.BlockSpec((1, h, GDN_HEAD_DIM, GDN_HEAD_DIM), lambda i, j: (i, 0, 0, 0)),
        pl.BlockSpec((1, CONV_W - 1, GDN_CONV_DIM), lambda i, j: (i, 0, 0)),
        pl.BlockSpec((CONV_W, GDN_CONV_DIM), const2),
        pl.BlockSpec((1, h), const2), pl.BlockSpec((h, 1), const2),
        pl.BlockSpec((1, h), const2), pl.BlockSpec((h, 1), const2),
        pl.BlockSpec((1, GDN_HEAD_DIM), const2),
    ]
    out_specs = [tok(GDN_V_DIM), pl.BlockSpec((1, h, GDN_HEAD_DIM, GDN_HEAD_DIM), lambda i, j: (i, 0, 0, 0))]
    out_shape = [jax.ShapeDtypeStruct((b, lp, GDN_V_DIM), F32),
                 jax.ShapeDtypeStruct((b, h, GDN_HEAD_DIM, GDN_HEAD_DIM), F32)]
    return pl.pallas_call(
        functools.partial(_gdn_body, l_valid=l_valid, nsub=nsub),
        grid=(b, nc // nsub), in_specs=in_specs, out_specs=out_specs, out_shape=out_shape,
        scratch_shapes=[pltpu.VMEM((nsub * CHUNK + 8, GDN_CONV_DIM), F32)],
        compiler_params=_cparams(("parallel", "arbitrary")), name="gdn_mixer",
    )(qkv, z, small, small_t, state0, conv0, conv_w,
      dt_bias.reshape(1, h), dt_bias.reshape(h, 1), a_log.reshape(1, h), a_log.reshape(h, 1),
      norm_w.reshape(1, GDN_HEAD_DIM))


def _prep_ab_weights(w_in_ab):
    o = [0, SSD_INNER]
    o.append(o[-1] + SSD_CONV_DIM)
    o.append(o[-1] + SSD_HEADS)
    o.append(o[-1] + GDN_CONV_DIM)
    o.append(o[-1] + GDN_V_DIM)
    o.append(o[-1] + 2 * GDN_V_HEADS)
    w = w_in_ab.astype(BF16)
    small = jnp.concatenate([w[:, o[2]:o[3]], w[:, o[5]:o[6]]], axis=1)
    small = jnp.pad(small, ((0, 0), (0, LANES - small.shape[1])))
    return [w[:, o[0]:o[1]], w[:, o[1]:o[2]], w[:, o[3]:o[4]], w[:, o[4]:o[5]], small]


def _pad_time(a, lp):
    return jnp.pad(a, [(0, 0), (0, lp - a.shape[1])] + [(0, 0)] * (a.ndim - 2))


def layer0_mixer(x, ssd_state0, ssd_conv0, gdn_state0, gdn_conv0, p, tm):
    b, l, d = x.shape
    lp = -(-l // CHUNK) * CHUNK
    xp = _pad_time(x, lp).reshape(b * lp, d)
    tm_proj = 2 * tm if (b * lp) % (2 * tm) == 0 else min(tm, b * lp)
    z_ssd, xbc, qkv, z_gdn, small = norm_proj(xp, p["norm_mix"][0], p["w_ab"], tm_proj)
    nc = lp // CHUNK
    small_t = small[:, :32].reshape(b, nc, CHUNK, 32).transpose(0, 1, 3, 2)
    r3 = lambda a: a.reshape(b, lp, a.shape[-1])
    xbc, qkv, small = r3(xbc), r3(qkv), r3(small)
    y_ssd, ssd_state = ssd_mixer(xbc, r3(z_ssd), small, small_t, ssd_state0, ssd_conv0, p["ssd_conv_w"],
                                 p["ssd_conv_b"], p["ssd_dt_bias"], p["ssd_a_log"], p["ssd_d"], p["ssd_norm_w"], l)
    y_gdn, gdn_state = gdn_mixer(qkv, r3(z_gdn), small, small_t, gdn_state0, gdn_conv0, p["gdn_conv_w"],
                                 p["gdn_dt_bias"], p["gdn_a_log"], p["gdn_norm_w"], l)
    ssd_conv = jnp.concatenate([ssd_conv0, xbc[:, :l]], axis=1)[:, l:]
    gdn_conv = jnp.concatenate([gdn_conv0, qkv[:, :l]], axis=1)[:, l:]
    y_ssd = y_ssd[:, :l].reshape(b * l, SSD_INNER)
    y_gdn = y_gdn[:, :l].reshape(b * l, GDN_V_DIM)
    return y_ssd, y_gdn, ssd_state, ssd_conv, gdn_state, gdn_conv


def _nsa_lt_body(*refs, n_in):
    x_refs = refs[:n_in]
    wk_ref, wv_ref, o_ref = refs[n_in:]
    w = 2 * NSA_KV_COLS
    acc_k = acc_v = None
    for j in range(CMP_STRIDE):
        xk = jnp.concatenate([r[0, :, j * w:j * w + NSA_KV_COLS] for r in x_refs], axis=0).astype(BF16)
        xv = jnp.concatenate([r[0, :, j * w + NSA_KV_COLS:(j + 1) * w] for r in x_refs], axis=0).astype(BF16)
        dk, dv = _dot(xk, wk_ref[j]), _dot(xv, wv_ref[j])
        acc_k = dk if acc_k is None else acc_k + dk
        acc_v = dv if acc_v is None else acc_v + dv
    o_ref[0, :, 0:2 * NSA_KV_COLS] = acc_k
    o_ref[0, :, 2 * NSA_KV_COLS:] = acc_v


def _lt_weights(w1):
    w1r = w1.reshape(2, CMP_STRIDE, NSA_HEAD_DIM, CMP_HIDDEN)
    eye = jnp.eye(NSA_KV_HEADS, dtype=w1.dtype)
    lead = jnp.einsum("ab,jdh->jadbh", eye, w1r[0]).reshape(CMP_STRIDE, NSA_KV_COLS, NSA_KV_COLS)
    tail = jnp.einsum("ab,jdh->jadbh", eye, w1r[1]).reshape(CMP_STRIDE, NSA_KV_COLS, NSA_KV_COLS)
    return jnp.concatenate([lead, tail], axis=-1).astype(BF16)


def nsa_lt_dense(kv, wk, wv, tc):
    b, l, w = kv.shape
    nch = l // CMP_STRIDE
    x = kv[:, :nch * CMP_STRIDE].reshape(b, nch, CMP_STRIDE * w)
    tc = min(tc, nch)
    assert nch % tc == 0
    const3 = lambda i, j: (0, 0, 0)
    return pl.pallas_call(
        functools.partial(_nsa_lt_body, n_in=1), grid=(b, nch // tc),
        in_specs=[pl.BlockSpec((1, tc, CMP_STRIDE * w), lambda i, j: (i, j, 0)),
                  pl.BlockSpec(wk.shape, const3), pl.BlockSpec(wv.shape, const3)],
        out_specs=pl.BlockSpec((1, tc, 4 * NSA_KV_COLS), lambda i, j: (i, j, 0)),
        out_shape=jax.ShapeDtypeStruct((b, nch, 4 * NSA_KV_COLS), F32),
        compiler_params=_cparams(("parallel", "parallel")), name="nsa_lt_dense",
    )(x, wk, wv)


def _nsa_lt_paged_body(pt_ref, *refs, n_pg):
    page_refs = refs[:n_pg]
    wk_ref, wv_ref, o_ref, xa_ref, xb_ref = refs[n_pg:]
    gi = pl.program_id(1)
    page = page_refs[0].shape[2]
    n_col = page_refs[0].shape[1] // LANES
    rows = n_pg * page // CMP_STRIDE
    half = n_col // 2

    @pl.when(gi == 0)
    def _():
        xb_ref[...] = jnp.zeros(xb_ref.shape, F32)

    chunks = page // CMP_STRIDE
    ti, ci = _iota2((page, page), 0), _iota2((page, page), 1)
    perm = jnp.where(ci == jnp.bitwise_and(ti, CMP_STRIDE - 1) * chunks + lax.shift_right_logical(ti, 4),
                     1.0, 0.0).astype(BF16)

    def work(fill_ref, read_ref):
        for i, r in enumerate(page_refs):
            xp = _dot(r[0].astype(BF16), perm)
            for c in range(n_col):
                fill_ref[c, i * page:(i + 1) * page, :] = xp[c * LANES:(c + 1) * LANES, :].T
        acc_k = acc_v = None
        for j in range(CMP_STRIDE):
            xj = [jnp.concatenate([read_ref[c, i * page + j * chunks:i * page + (j + 1) * chunks, :]
                                   for i in range(n_pg)], axis=0) for c in range(n_col)]
            dk = _dot(jnp.concatenate(xj[:half], axis=1).astype(BF16), wk_ref[j])
            dv = _dot(jnp.concatenate(xj[half:], axis=1).astype(BF16), wv_ref[j])
            acc_k = dk if acc_k is None else acc_k + dk
            acc_v = dv if acc_v is None else acc_v + dv
        o_ref[0, :, 0:2 * NSA_KV_COLS] = acc_k
        o_ref[0, :, 2 * NSA_KV_COLS:] = acc_v

    @pl.when(jnp.bitwise_and(gi, 1) == 0)
    def _():
        work(xa_ref, xb_ref)

    @pl.when(jnp.bitwise_and(gi, 1) == 1)
    def _():
        work(xb_ref, xa_ref)


def nsa_lt_paged(cache_t, page_table, wk, wv, n_pg):
    npool, w, page = cache_t.shape
    b, n_pages = page_table.shape
    rows = page // CMP_STRIDE
    n_grp = n_pages // n_pg
    assert n_pages % n_pg == 0
    page_spec = lambda i: pl.BlockSpec(
        (1, w, page), lambda bi, gi, pt: (pt[bi, jnp.minimum(gi, n_grp - 1) * n_pg + i], 0, 0))
    const3 = lambda bi, gi, pt: (0, 0, 0)
    buf = pltpu.VMEM((w // LANES, n_pg * page, LANES), F32)
    grid_spec = pltpu.PrefetchScalarGridSpec(
        num_scalar_prefetch=1, grid=(b, n_grp + 1),
        in_specs=[page_spec(i) for i in range(n_pg)] + [pl.BlockSpec(wk.shape, const3), pl.BlockSpec(wv.shape, const3)],
        out_specs=pl.BlockSpec((1, n_pg * rows, 4 * NSA_KV_COLS), lambda bi, gi, pt: (bi, jnp.maximum(gi - 1, 0), 0)),
        scratch_shapes=[buf, buf])
    return pl.pallas_call(
        functools.partial(_nsa_lt_paged_body, n_pg=n_pg), grid_spec=grid_spec,
        out_shape=jax.ShapeDtypeStruct((b, n_pages * rows, 4 * NSA_KV_COLS), F32),
        compiler_params=_cparams(("parallel", "arbitrary")), name="nsa_lt_paged",
    )(page_table, *([cache_t] * n_pg), wk, wv)


def _cmp_kv_body(lt_ref, pek_ref, w1k_ref, b1k_ref, w2k_ref, pev_ref, w1v_ref, b1v_ref, w2v_ref, w2vt_ref,
                 kc_ref, vc_ref, vct_ref, sh_ref):
    nch = lt_ref.shape[1]
    kvc = NSA_KV_COLS

    def branch(off, pe_ref, w1_ref, b1_ref, w2_ref, out_ref):
        pe = jnp.broadcast_to(pe_ref[...], (8, pe_ref.shape[1]))
        pe_term = _mm3(pe, w1_ref[...])[0:1, :] + b1_ref[...]
        pe4 = jnp.concatenate([pe_term] * NSA_KV_HEADS, axis=1)
        sh_ref[0:nch, :] = lt_ref[0, :, off + kvc:off + 2 * kvc]
        sh_ref[nch:nch + 8, :] = jnp.zeros((8, kvc), F32)
        hid = _silu(lt_ref[0, :, off:off + kvc] + sh_ref[1:nch + 1, :] + pe4).astype(BF16)
        out_ref[0] = _dot(hid, w2_ref[...])
        return hid

    branch(0, pek_ref, w1k_ref, b1k_ref, w2k_ref, kc_ref)
    hid_v = branch(2 * kvc, pev_ref, w1v_ref, b1v_ref, w2v_ref, vc_ref)
    vct_ref[0] = _dot_nt(w2vt_ref[...], hid_v)


def cmp_kv(lt, p):
    b, nch, _ = lt.shape
    kvc = NSA_KV_COLS
    const2 = lambda i: (0, 0)
    args, specs = [], []
    for s in ("k", "v"):
        pe = p["cmp_pe_" + s].reshape(1, -1)
        w1 = p["cmp_w1_" + s]
        b1 = p["cmp_b1_" + s].reshape(1, -1)
        w2 = jnp.kron(jnp.eye(NSA_KV_HEADS, dtype=F32), p["cmp_w2_" + s]).astype(BF16)
        for a in (pe, w1, b1, w2):
            args.append(a)
            specs.append(pl.BlockSpec(a.shape, const2))
    args.append(jnp.kron(jnp.eye(NSA_KV_HEADS, dtype=F32), p["cmp_w2_v"].T).astype(BF16))
    specs.append(pl.BlockSpec(args[-1].shape, const2))
    out_spec = pl.BlockSpec((1, nch, kvc), lambda i: (i, 0, 0))
    return pl.pallas_call(
        _cmp_kv_body, grid=(b,),
        in_specs=[pl.BlockSpec((1, nch, 4 * kvc), lambda i: (i, 0, 0))] + specs,
        out_specs=[out_spec, out_spec, pl.BlockSpec((1, kvc, nch), lambda i: (i, 0, 0))],
        out_shape=[jax.ShapeDtypeStruct((b, nch, kvc), F32)] * 2 + [jax.ShapeDtypeStruct((b, kvc, nch), F32)],
        scratch_shapes=[pltpu.VMEM((nch + 8, kvc), F32)],
        compiler_params=_cparams(("parallel",)), name="cmp_kv",
    )(lt, *args)


def _cmp_attn_body(q_ref, gate_ref, kc_ref, vc_ref, oc_ref, sel_ref, *, tq, pos0, n_slc, nbp):
    qi = pl.program_id(1)
    ncp = kc_ref.shape[1]
    hd = NSA_HEAD_DIM
    t_col = pos0 + qi * tq + _iota2((tq, 1), 0)
    cmask = (_iota2((1, ncp), 1) * CMP_STRIDE + (CMP_LEN - 1)) <= t_col
    kc = kc_ref[0].astype(BF16)
    vc = vc_ref[0].astype(BF16)
    gate = jax.nn.sigmoid(gate_ref[0])
    nn = _iota2((ncp, nbp), 0) * CMP_STRIDE
    ss = _iota2((ncp, nbp), 1) * SLC_BLOCK
    overlap = jnp.where((nn < ss + SLC_BLOCK) & (nn + CMP_LEN > ss), 1.0, 0.0).astype(BF16)
    blk = _iota2((1, nbp), 1)
    blk_f = blk.astype(F32)
    cur = lax.shift_right_logical(t_col, 6)
    forced = (blk == 0) | (blk == cur) | (blk == cur - 1)
    bonus = jnp.where(forced, FORCE_BONUS, 0.0)
    causal_blk = blk * SLC_BLOCK <= t_col
    n_pick = min(SLC_TOPN, n_slc)

    scores = []
    for k in range(NSA_KV_HEADS):
        kc_k = kc[:, k * hd:(k + 1) * hd]
        vc_k = vc[:, k * hd:(k + 1) * hd]
        hs = [k * NSA_GROUP + g for g in range(NSA_GROUP)]
        q_l = [(q_ref[0, :, h * hd:(h + 1) * hd] * (hd ** -0.5)).astype(BF16) for h in hs]
        s_l = [jnp.where(cmask, _dot_nt(q_h, kc_k), NEG_INF) for q_h in q_l]
        e_l = [jnp.where(cmask, jnp.exp(s - jnp.max(s, axis=-1, keepdims=True)), 0.0) for s in s_l]
        den_l = [jnp.sum(e, axis=-1, keepdims=True) for e in e_l]
        p_l = [e * (1.0 / jnp.where(den > 0.0, den, 1.0)) for e, den in zip(e_l, den_l)]
        o_l = [_dot(p.astype(BF16), vc_k) for p in p_l]
        for h, o in zip(hs, o_l):
            oc_ref[0, :, h * hd:(h + 1) * hd] = o * gate[:, h:h + 1]
        ph, plo = _split2((p_l[0] + p_l[1]) + (p_l[2] + p_l[3]))
        imp = _dot(ph, overlap) + _dot(plo, overlap)
        score = jnp.where(causal_blk, imp + bonus, NEG_INF)
        scores.append(jnp.where(blk < n_slc, score, -jnp.inf))

    def pick_one(_, carry):
        work, sel = carry
        m = jnp.max(work, axis=-1, keepdims=True)
        idx = jnp.min(jnp.where(work == m, blk_f, 1e9), axis=-1, keepdims=True)
        hit = blk_f == idx
        return jnp.where(hit, -jnp.inf, work), jnp.where(hit, 1.0, sel)

    work0 = jnp.concatenate(scores, axis=0)
    _, sel = lax.fori_loop(0, n_pick, pick_one, (work0, jnp.zeros(work0.shape, F32)))
    for k in range(NSA_KV_HEADS):
        sel_ref[0, :, k * nbp:(k + 1) * nbp] = sel[k * tq:(k + 1) * tq]


def _nbp(n_slc):
    return 64 if n_slc <= 64 else -(-n_slc // LANES) * LANES


def cmp_attn_select(q, gate, kc, vc, tq, pos0, n_slc):
    b, l, dq = q.shape
    ncp = kc.shape[1]
    nbp = _nbp(n_slc)
    assert l % tq == 0
    tok = lambda w: pl.BlockSpec((1, tq, w), lambda i, j: (i, j, 0))
    whole = pl.BlockSpec((1, ncp, NSA_KV_COLS), lambda i, j: (i, 0, 0))
    return pl.pallas_call(
        functools.partial(_cmp_attn_body, tq=tq, pos0=pos0, n_slc=n_slc, nbp=nbp),
        grid=(b, l // tq), in_specs=[tok(dq), tok(LANES), whole, whole],
        out_specs=[tok(dq), tok(NSA_KV_HEADS * nbp)],
        out_shape=[jax.ShapeDtypeStruct((b, l, dq), F32), jax.ShapeDtypeStruct((b, l, NSA_KV_HEADS * nbp), F32)],
        compiler_params=_cparams(("parallel", "parallel")), name="cmp_attn_select",
    )(q, gate, kc, vc)


def _nsa_attn_body(q_ref, gate_ref, kv_ref, *rest, mode, tq, tk, qpos0, kpos0, gate_off, nbp):
    if mode == "slc":
        sel_ref, o_ref, qs_ref, m_ref, l_ref, acc_ref = rest
    else:
        o_ref, qs_ref, m_ref, l_ref, acc_ref = rest
    qi = pl.program_id(1)
    hd = NSA_HEAD_DIM
    nk = kv_ref.shape[1] // tk
    qmin = qpos0 + qi * tq
    t_col = qmin + _iota2((tq, 1), 0)
    for k in range(NSA_KV_HEADS):
        qs_ref[k] = jnp.concatenate(
            [q_ref[0, :, (k * NSA_GROUP + g) * hd:(k * NSA_GROUP + g + 1) * hd] * (hd ** -0.5)
             for g in range(NSA_GROUP)], axis=0)
    m_ref[...] = jnp.full(m_ref.shape, NEG_INF, F32)
    l_ref[...] = jnp.zeros(l_ref.shape, F32)
    acc_ref[...] = jnp.zeros(acc_ref.shape, F32)
    hi = jnp.minimum(nk - 1, (qmin + tq - 1 - kpos0) // tk)
    if mode == "slc":
        lo = 0
    else:
        lo = jnp.maximum(qmin - (WINDOW - 1) - kpos0, 0) // tk

    def step(ki, carry):
        k0 = pl.multiple_of(ki * tk, tk)
        kv = kv_ref[0, pl.ds(k0, tk), :]
        kpos = kpos0 + k0 + _iota2((1, tk), 1)
        if mode == "slc":
            expand = jnp.where(_iota2((nbp, tk), 0) == lax.shift_right_logical(k0 + _iota2((nbp, tk), 1), 6),
                               1.0, 0.0).astype(BF16)
            base = kpos <= t_col
        else:
            dist = t_col - kpos
            base = (dist >= 0) & (dist < WINDOW)
        for k in range(NSA_KV_HEADS):
            kk = kv[:, k * hd:(k + 1) * hd].astype(BF16)
            vv = kv[:, NSA_KV_COLS + k * hd:NSA_KV_COLS + (k + 1) * hd].astype(BF16)
            s = _dot_nt(qs_ref[k].astype(BF16), kk)
            if mode == "slc":
                mask = base & (_dot(sel_ref[0, :, k * nbp:(k + 1) * nbp].astype(BF16), expand) > 0.5)
            else:
                mask = base
            for g in range(NSA_GROUP):
                r = k * NSA_GROUP + g
                s_g = jnp.where(mask, s[g * tq:(g + 1) * tq], NEG_INF)
                m_old = m_ref[r]
                m_new = jnp.maximum(m_old, jnp.max(s_g, axis=-1, keepdims=True))
                alpha = jnp.exp(m_old - m_new)
                p = jnp.where(mask, jnp.exp(s_g - m_new), 0.0)
                l_ref[r] = alpha * l_ref[r] + jnp.sum(p, axis=-1, keepdims=True)
                acc_ref[r] = alpha * acc_ref[r] + _dot(p.astype(BF16), vv)
                m_ref[r] = m_new
        return carry

    lax.fori_loop(lo, hi + 1, step, 0)
    gate = jax.nn.sigmoid(gate_ref[0])
    for r in range(NSA_HEADS):
        l = l_ref[r]
        o = acc_ref[r] / jnp.where(l > 0.0, l, 1.0)
        o_ref[0, :, r * hd:(r + 1) * hd] = o * gate[:, gate_off + r:gate_off + r + 1]


def nsa_attn(mode, q, gate, kv, sel, tq, tk, qpos0, kpos0):
    b, l, dq = q.shape
    nkeys = kv.shape[1]
    assert l % tq == 0 and nkeys % tk == 0
    tok = lambda w: pl.BlockSpec((1, tq, w), lambda i, j: (i, j, 0))
    in_specs = [tok(dq), tok(LANES), pl.BlockSpec((1, nkeys, kv.shape[2]), lambda i, j: (i, 0, 0))]
    args = [q, gate, kv]
    nbp = 0
    if mode == "slc":
        nbp = sel.shape[2] // NSA_KV_HEADS
        in_specs.append(tok(sel.shape[2]))
        args.append(sel)
    return pl.pallas_call(
        functools.partial(_nsa_attn_body, mode=mode, tq=tq, tk=tk, qpos0=qpos0, kpos0=kpos0,
                          gate_off=NSA_HEADS * (1 if mode == "slc" else 2), nbp=nbp),
        grid=(b, l // tq), in_specs=in_specs, out_specs=tok(dq),
        out_shape=jax.ShapeDtypeStruct((b, l, dq), F32),
        scratch_shapes=[pltpu.VMEM((NSA_KV_HEADS, NSA_GROUP * tq, NSA_HEAD_DIM), F32),
                        pltpu.VMEM((NSA_HEADS, tq, 1), F32), pltpu.VMEM((NSA_HEADS, tq, 1), F32),
                        pltpu.VMEM((NSA_HEADS, tq, NSA_HEAD_DIM), F32)],
        compiler_params=_cparams(("parallel", "parallel")), name="nsa_attn_" + mode,
    )(*args)


def _slc_paged_body(pt_ref, q_ref, gate_ref, selgrp_ref, selnew_ref, kvnew_ref, *rest, n_pg, tq, l_new):
    page_refs = rest[:n_pg]
    o_ref, qbd_ref, m_ref, l_ref, acc_ref = rest[n_pg:]
    gi = pl.program_id(1)
    hd = NSA_HEAD_DIM
    rows = NSA_HEADS * tq

    @pl.when(gi == 0)
    def _():
        qbd_ref[...] = jnp.zeros(qbd_ref.shape, F32)
        for h in range(NSA_HEADS):
            k = h // NSA_GROUP
            qbd_ref[h * tq:(h + 1) * tq, k * hd:(k + 1) * hd] = q_ref[0, :, h * hd:(h + 1) * hd] * (hd ** -0.5)
        m_ref[...] = jnp.full(m_ref.shape, NEG_INF, F32)
        l_ref[...] = jnp.zeros(l_ref.shape, F32)
        acc_ref[...] = jnp.zeros(acc_ref.shape, F32)

    qbd = qbd_ref[...].astype(BF16)

    def update(s, vv, v_transposed):
        m_old = m_ref[...]
        m_new = jnp.maximum(m_old, jnp.max(s, axis=-1, keepdims=True))
        alpha = jnp.exp(m_old - m_new)
        p = jnp.exp(s - m_new)
        l_ref[...] = alpha * l_ref[...] + jnp.sum(p, axis=-1, keepdims=True)
        acc_ref[...] = alpha * acc_ref[...] + (_dot_nt if v_transposed else _dot)(p.astype(BF16), vv)
        m_ref[...] = m_new

    page = page_refs[0].shape[2]
    wide = n_pg * page
    nblk = wide // SLC_BLOCK
    kt = jnp.concatenate([r[0, 0:NSA_KV_COLS, :] for r in page_refs], axis=1).astype(BF16)
    vt = jnp.concatenate([r[0, NSA_KV_COLS:, :] for r in page_refs], axis=1).astype(BF16)
    expand = jnp.where(_iota2((nblk, wide), 0) == lax.shift_right_logical(_iota2((nblk, wide), 1), 6),
                       1.0, 0.0).astype(BF16)
    selb = ((1.0 - selgrp_ref[0, 0]) * MASKED).astype(BF16)
    update(_dot(qbd, kt) + _dot(selb, expand), vt, True)

    @pl.when(gi == pl.num_programs(1) - 1)
    def _():
        kvn = kvnew_ref[0]
        nr = kvn.shape[0]
        j = _iota2((rows, nr), 1)
        iq = jnp.bitwise_and(_iota2((rows, nr), 0), tq - 1)
        mask = (selnew_ref[0] > 0.5) & (j <= iq) & (j < l_new)
        s_new = _dot_nt(qbd, kvn[:, :NSA_KV_COLS].astype(BF16)) + jnp.where(mask, 0.0, MASKED)
        update(s_new, kvn[:, NSA_KV_COLS:].astype(BF16), False)
        gate = jax.nn.sigmoid(gate_ref[0])
        l = l_ref[...]
        o = acc_ref[...] / jnp.where(l > 0.0, l, 1.0)
        for h in range(NSA_HEADS):
            k = h // NSA_GROUP
            o_ref[0, :, h * hd:(h + 1) * hd] = (o[h * tq:(h + 1) * tq, k * hd:(k + 1) * hd]
                                                * gate[:, NSA_HEADS + h:NSA_HEADS + h + 1])


def slc_paged(q, gate, selrows, kv_new, cache_t, page_table, n_pg, l_new, p0):
    b, tq, dq = q.shape
    npool, w, page = cache_t.shape
    n_pages = page_table.shape[1]
    rows = NSA_HEADS * tq
    n_grp = n_pages // n_pg
    nblk = n_pg * page // SLC_BLOCK
    assert n_pages % n_pg == 0 and p0 % SLC_BLOCK == 0 and l_new <= SLC_BLOCK and p0 == n_pages * page
    assert tq & (tq - 1) == 0
    blk_new = p0 // SLC_BLOCK
    sel_grp = selrows[:, :, :blk_new].reshape(b, rows, n_grp, nblk).transpose(0, 2, 1, 3)
    sel_new = selrows[:, :, blk_new:blk_new + 1]
    per_b = lambda shape: pl.BlockSpec((1,) + shape, lambda bi, gi, pt: (bi, 0, 0))
    page_spec = lambda i: pl.BlockSpec((1, w, page), lambda bi, gi, pt: (pt[bi, gi * n_pg + i], 0, 0))
    grid_spec = pltpu.PrefetchScalarGridSpec(
        num_scalar_prefetch=1, grid=(b, n_grp),
        in_specs=[per_b((tq, dq)), per_b((tq, LANES)),
                  pl.BlockSpec((1, 1, rows, nblk), lambda bi, gi, pt: (bi, gi, 0, 0)),
                  per_b((rows, 1)), per_b((kv_new.shape[1], w))]
        + [page_spec(i) for i in range(n_pg)],
        out_specs=per_b((tq, dq)),
        scratch_shapes=[pltpu.VMEM((rows, NSA_KV_COLS), F32), pltpu.VMEM((rows, 1), F32),
                        pltpu.VMEM((rows, 1), F32), pltpu.VMEM((rows, NSA_KV_COLS), F32)])
    return pl.pallas_call(
        functools.partial(_slc_paged_body, n_pg=n_pg, tq=tq, l_new=l_new),
        grid_spec=grid_spec, out_shape=jax.ShapeDtypeStruct((b, tq, dq), F32),
        compiler_params=_cparams(("parallel", "arbitrary")), name="slc_paged",
    )(page_table, q, gate, sel_grp, sel_new, kv_new, *([cache_t] * n_pg))


def _cmp_attn_t_body(qt_ref, gt_ref, kc_ref, vct_ref, oct_ref, selt_ref, *, tq, pos0, n_slc, nbp):
    qi = pl.program_id(1)
    ncp = kc_ref.shape[1]
    hd = NSA_HEAD_DIM
    t_row = pos0 + qi * tq + _iota2((1, tq), 1)
    cbias = jnp.where((_iota2((ncp, 1), 0) * CMP_STRIDE + (CMP_LEN - 1)) <= t_row, 0.0, MASKED)
    kc = kc_ref[0].astype(BF16)
    vct = vct_ref[0].astype(BF16)
    gate = jax.nn.sigmoid(gt_ref[0, 0:NSA_HEADS, :])
    ss = _iota2((nbp, ncp), 0) * SLC_BLOCK
    nn = _iota2((nbp, ncp), 1) * CMP_STRIDE
    overlap_t = jnp.where((nn < ss + SLC_BLOCK) & (nn + CMP_LEN > ss), 1.0, 0.0).astype(BF16)
    blk = _iota2((nbp, 1), 0)
    blk_f = blk.astype(F32)
    cur = lax.shift_right_logical(t_row, 6)
    forced = (blk == 0) | (blk == cur) | (blk == cur - 1)
    bonus = jnp.where(forced, FORCE_BONUS, 0.0)
    causal_blk = blk * SLC_BLOCK <= t_row
    n_pick = min(SLC_TOPN, n_slc)

    scores = []
    for k in range(NSA_KV_HEADS):
        kc_k = kc[:, k * hd:(k + 1) * hd]
        vct_k = vct[k * hd:(k + 1) * hd, :]
        hs = [k * NSA_GROUP + g for g in range(NSA_GROUP)]
        q_l = [(qt_ref[0, h * hd:(h + 1) * hd, :] * (hd ** -0.5 * LOG2E)).astype(BF16) for h in hs]
        s_l = [_dot(kc_k, q_h) + cbias for q_h in q_l]
        e_l = [jnp.exp2(s - jnp.maximum(jnp.max(s, axis=0, keepdims=True), NEG_INF)) for s in s_l]
        den_l = [jnp.sum(e, axis=0, keepdims=True) for e in e_l]
        p_l = [e * (1.0 / jnp.where(den > 0.0, den, 1.0)) for e, den in zip(e_l, den_l)]
        o_l = [_dot(vct_k, p.astype(BF16)) for p in p_l]
        for h, o in zip(hs, o_l):
            oct_ref[0, h * hd:(h + 1) * hd, :] = o * gate[h:h + 1, :]
        psum = (p_l[0] + p_l[1]) + (p_l[2] + p_l[3])
        ph, plo = _split2(psum)
        imp = _dot(overlap_t, ph) + _dot(overlap_t, plo)
        score = jnp.where(causal_blk, imp + bonus, NEG_INF)
        scores.append(jnp.where(blk < n_slc, score, -jnp.inf))

    def pick_one(_, work):
        m = jnp.max(work, axis=0, keepdims=True)
        idx = jnp.min(jnp.where(work == m, blk_f, 1e9), axis=0, keepdims=True)
        return jnp.where(blk_f == idx, -jnp.inf, work)

    work = lax.fori_loop(0, n_pick, pick_one, jnp.concatenate(scores, axis=1))
    sel = jnp.where((work == -jnp.inf) & (blk < n_slc), 1.0, 0.0)
    for k in range(NSA_KV_HEADS):
        selt_ref[0, k * nbp:(k + 1) * nbp, :] = sel[:, k * tq:(k + 1) * tq]


def cmp_attn_select_t(qt, gate_t, kc, vct, b, tq, pos0, n_slc):
    nt, dq, _ = qt.shape
    nq = nt // b
    ncp = kc.shape[1]
    nbp = _nbp(n_slc)
    tile = lambda rows: pl.BlockSpec((1, rows, tq), lambda i, j: (i * nq + j, 0, 0))
    return pl.pallas_call(
        functools.partial(_cmp_attn_t_body, tq=tq, pos0=pos0, n_slc=n_slc, nbp=nbp),
        grid=(b, nq),
        in_specs=[tile(dq), tile(LANES), pl.BlockSpec((1, ncp, NSA_KV_COLS), lambda i, j: (i, 0, 0)),
                  pl.BlockSpec((1, NSA_KV_COLS, ncp), lambda i, j: (i, 0, 0))],
        out_specs=[tile(dq), tile(NSA_KV_HEADS * nbp)],
        out_shape=[jax.ShapeDtypeStruct((nt, dq, tq), F32), jax.ShapeDtypeStruct((nt, NSA_KV_HEADS * nbp, tq), F32)],
        compiler_params=_cparams(("parallel", "parallel")), name="cmp_attn_select_t",
    )(qt, gate_t, kc, vct)


def _nsa_attn_t_body(qt_ref, gt_ref, kv_ref, vt_ref, *rest, mode, tq, tk, gate_off, nbp):
    if mode == "slc":
        selt_ref, o_ref, qs_ref, m_ref, l_ref, acc_ref = rest
    else:
        o_ref, qs_ref, m_ref, l_ref, acc_ref = rest
    qi = pl.program_id(1)
    hd = NSA_HEAD_DIM
    nk = kv_ref.shape[1] // tk
    qmin = qi * tq
    t_row = qmin + _iota2((1, tq), 1)
    for r in range(NSA_HEADS):
        qs_ref[r, 0:hd, :] = (qt_ref[0, r * hd:(r + 1) * hd, :] * (hd ** -0.5 * LOG2E)).astype(BF16)
        if mode == "slc":
            k = r // NSA_GROUP
            qs_ref[r, hd:hd + nbp, :] = ((1.0 - selt_ref[0, k * nbp:(k + 1) * nbp, :]) * MASKED).astype(BF16)
    m_ref[...] = jnp.full(m_ref.shape, NEG_INF, F32)
    l_ref[...] = jnp.zeros(l_ref.shape, F32)
    acc_ref[...] = jnp.zeros(acc_ref.shape, F32)
    hi = jnp.minimum(nk - 1, (qmin + tq - 1) // tk)

    heads_per_stage = NSA_HEADS

    def tile(ki, positional):
        k0 = pl.multiple_of(ki * tk, tk)
        krows = kv_ref[0, pl.ds(k0, tk), 0:NSA_KV_COLS].astype(BF16)
        vt = vt_ref[ki].astype(BF16)
        kpos = k0 + _iota2((tk, 1), 0)
        if mode == "slc":
            expand_t = jnp.where(_iota2((tk, nbp), 1) == lax.shift_right_logical(k0 + _iota2((tk, nbp), 0), 6),
                                 1.0, 0.0).astype(BF16)
            cbias = jnp.where(kpos <= t_row, 0.0, MASKED) if positional else None
        else:
            dist = t_row - kpos
            cbias = jnp.where((dist >= 0) & (dist < WINDOW), 0.0, MASKED)
        k_l, vt_l = [], []
        for k in range(NSA_KV_HEADS):
            k_k = krows[:, k * hd:(k + 1) * hd]
            k_l.append(jnp.concatenate([k_k, expand_t], axis=1) if mode == "slc" else k_k)
            vt_l.append(vt[k * hd:(k + 1) * hd, :])
        for r0 in range(0, NSA_HEADS, heads_per_stage):
            rs = list(range(r0, r0 + heads_per_stage))
            s_l = [_dot(k_l[r // NSA_GROUP], qs_ref[r]) for r in rs]
            if cbias is not None:
                s_l = [s + cbias for s in s_l]
            m_old = [m_ref[r] for r in rs]
            m_new = [jnp.maximum(mo, jnp.max(s, axis=0, keepdims=True)) for mo, s in zip(m_old, s_l)]
            p_l = [jnp.exp2(s - mn) for s, mn in zip(s_l, m_new)]
            pv_l = [_dot(vt_l[r // NSA_GROUP], p.astype(BF16)) for r, p in zip(rs, p_l)]
            for r, mo, mn, p, pv in zip(rs, m_old, m_new, p_l, pv_l):
                alpha = jnp.exp2(mo - mn)
                l_ref[r] = alpha * l_ref[r] + jnp.sum(p, axis=0, keepdims=True)
                acc_ref[r] = alpha * acc_ref[r] + pv
                m_ref[r] = mn

    def step(ki, carry):
        tile(ki, mode != "slc")
        return carry

    if mode == "slc":
        lax.fori_loop(0, hi, step, 0)
        tile(hi, True)
    else:
        lax.fori_loop(jnp.maximum(qmin - (WINDOW - 1), 0) // tk, hi + 1, step, 0)
    gate = jax.nn.sigmoid(gt_ref[0, gate_off:gate_off + NSA_HEADS, :])
    for r in range(NSA_HEADS):
        l = l_ref[r]
        o_ref[0, r * hd:(r + 1) * hd, :] = acc_ref[r] * (gate[r:r + 1, :] / jnp.where(l > 0.0, l, 1.0))


def nsa_attn_t(mode, qt, gate_t, kv, vt, selt, b, tq, tk):
    nt, dq, _ = qt.shape
    nq = nt // b
    l = kv.shape[1]
    nk = l // tk
    assert l % tk == 0 and vt.shape == (b * nk, NSA_KV_COLS, tk) and (mode != "slc" or tq == tk)
    tile = lambda rows: pl.BlockSpec((1, rows, tq), lambda i, j: (i * nq + j, 0, 0))
    in_specs = [tile(dq), tile(LANES), pl.BlockSpec((1, l, kv.shape[2]), lambda i, j: (i, 0, 0)),
                pl.BlockSpec((nk, NSA_KV_COLS, tk), lambda i, j: (i, 0, 0))]
    args = [qt, gate_t, kv, vt]
    nbp = 0
    if mode == "slc":
        nbp = selt.shape[1] // NSA_KV_HEADS
        in_specs.append(tile(selt.shape[1]))
        args.append(selt)
    return pl.pallas_call(
        functools.partial(_nsa_attn_t_body, mode=mode, tq=tq, tk=tk,
                          gate_off=NSA_HEADS * (1 if mode == "slc" else 2), nbp=nbp),
        grid=(b, nq), in_specs=in_specs, out_specs=tile(dq),
        out_shape=jax.ShapeDtypeStruct((nt, dq, tq), F32),
        scratch_shapes=[pltpu.VMEM((NSA_HEADS, NSA_HEAD_DIM + nbp, tq), BF16),
                        pltpu.VMEM((NSA_HEADS, 1, tq), F32), pltpu.VMEM((NSA_HEADS, 1, tq), F32),
                        pltpu.VMEM((NSA_HEADS, NSA_HEAD_DIM, tq), F32)],
        compiler_params=_cparams(("parallel", "parallel")), name="nsa_attn_t_" + mode,
    )(*args)


def _prep_nsa_weights(w_in_nsa):
    w = w_in_nsa.astype(BF16)
    dq = NSA_HEADS * NSA_HEAD_DIM
    kvw = 2 * NSA_KV_COLS
    gate = jnp.pad(w[:, dq + 3 * kvw:], ((0, 0), (0, LANES - 3 * NSA_HEADS)))
    return [w[:, :dq], w[:, dq:dq + kvw], w[:, dq + kvw:dq + 2 * kvw], w[:, dq + 2 * kvw:dq + 3 * kvw], gate]


def _kv5(a):
    return a.reshape(a.shape[0], a.shape[1], 2, NSA_KV_HEADS, NSA_HEAD_DIM)


def nsa_prompt(x, p, tm):
    b, l, d = x.shape
    assert l % tm == 0
    wq, wc, ws, ww, wg = p["w_nsa"]
    v_rows = slice(NSA_KV_COLS, 2 * NSA_KV_COLS)
    wts = [wq.T, ws.T[v_rows], ww.T[v_rows], wg.T]
    kvc, kvs, kvw, qt, vst, vwt, gt = norm_proj(x.reshape(b * l, d), p["norm_mix"][1], [wc, ws, ww], tm, wts)
    r3 = lambda a: a.reshape(b, l, a.shape[-1])
    kvc, kvs, kvw = r3(kvc), r3(kvs), r3(kvw)
    lt = nsa_lt_dense(kvc, p["lt_wk"], p["lt_wv"], 256)
    kc, _, vct = cmp_kv(lt, p)
    n_slc = -(-l // SLC_BLOCK)
    o_c, selt = cmp_attn_select_t(qt, gt, kc, vct, b, tm, 0, n_slc)
    o_s = nsa_attn_t("slc", qt, gt, kvs, vst, selt, b, tm, tm)
    o_w = nsa_attn_t("win", qt, gt, kvw, vwt, None, b, tm, tm)
    keep = min(WINDOW, l)
    return [o_c, o_s, o_w], _kv5(kvc), _kv5(kvs), _kv5(kvw[:, l - keep:])


def nsa_sample(x, cache_cmp, cache_slc, past_win, page_table, p, n_pg):
    b, lq, d = x.shape
    tq = 8
    assert lq <= tq
    npool, page = cache_cmp.shape[:2]
    w = 2 * NSA_KV_COLS
    p0 = page_table.shape[1] * page
    xp = _pad_time(x, tq).reshape(b * tq, d)
    q, kvc, kvs, kvw, gate = norm_proj(xp, p["norm_mix"][1], p["w_nsa"], b * tq)
    r3 = lambda a: a.reshape(b, tq, a.shape[-1])
    q, kvc, kvs, kvw, gate = r3(q), r3(kvc), r3(kvs), r3(kvw), r3(gate)
    pages_t = lambda c: c.reshape(npool, page, w).transpose(0, 2, 1)
    lt = nsa_lt_paged(pages_t(cache_cmp), page_table, p["lt_wk"], p["lt_wv"], n_pg)
    kc, vc, _ = cmp_kv(lt, p)
    n_slc = -(-(p0 + lq) // SLC_BLOCK)
    o_c, sel = cmp_attn_select(q, gate, kc, vc, tq, p0, n_slc)
    nbp = sel.shape[2] // NSA_KV_HEADS
    selrows = sel.reshape(b, tq, NSA_KV_HEADS, 1, nbp).transpose(0, 2, 3, 1, 4)
    selrows = jnp.broadcast_to(selrows, (b, NSA_KV_HEADS, NSA_GROUP, tq, nbp)).reshape(b, NSA_HEADS * tq, nbp)
    o_s = slc_paged(q, gate, selrows, kvs, pages_t(cache_slc), page_table, n_pg, lq, p0)
    n_past = past_win.shape[1]
    win_all = jnp.concatenate([past_win.reshape(b, n_past, w), kvw[:, :lq]], axis=1)
    nkeys = -(-(n_past + lq) // 8) * 8
    o_w = nsa_attn("win", q, gate, _pad_time(win_all, nkeys), None, tq, nkeys, p0, p0 - n_past)
    keep = min(WINDOW, n_past + lq)
    flat = lambda a: a[:, :lq].reshape(b * lq, a.shape[-1])
    return ([flat(o_c), flat(o_s), flat(o_w)], _kv5(kvc[:, :lq]), _kv5(kvs[:, :lq]),
            _kv5(win_all[:, n_past + lq - keep:]))


def _trunk(x, ssd_state0, ssd_conv0, gdn_state0, gdn_conv0, nsa_fn, p, tm):
    b, l, d = x.shape
    t = b * l
    tm = min(tm, t)
    y_ssd, y_gdn, ssd_state, ssd_conv, gdn_state, gdn_conv = layer0_mixer(
        x, ssd_state0, ssd_conv0, gdn_state0, gdn_conv0, p, tm)
    tm_mlp = 2 * tm if t % (2 * tm) == 0 else tm
    x1 = mix_mlp(x.reshape(t, d), [[y_ssd], [y_gdn]], p["w_out_ab"], p["norm_mlp"][0], p["w_up"][0],
                 p["w_down"][0], None, tm_mlp)
    branches, cmp_rows, slc_rows, win_rows = nsa_fn(x1.reshape(b, l, d))
    y = mix_mlp(x1, [branches], [p["w_out_nsa"]], p["norm_mlp"][1], p["w_up"][1], p["w_down"][1],
                p["norm_final"], tm, transposed=branches[0].ndim == 3)
    return y.reshape(b, l, d), ssd_state, ssd_conv, gdn_state, gdn_conv, cmp_rows, slc_rows, win_rows


def kernel(x_prompt, x_sample, state_ssd, state_ssd_conv, state_gdn, state_gdn_conv, cache_cmp_kv, cache_slc_kv,
           state_win_kv, page_table, norm_mix, norm_mlp, norm_final, w_in_ab, ssd_conv_w, ssd_conv_b, ssd_dt_bias,
           ssd_a_log, ssd_d, ssd_norm_w, gdn_conv_w, gdn_dt_bias, gdn_a_log, gdn_norm_w, w_out_ab, w_in_nsa,
           cmp_pe_k, cmp_w1_k, cmp_b1_k, cmp_w2_k, cmp_pe_v, cmp_w1_v, cmp_b1_v, cmp_w2_v, w_out_nsa, w_up, w_down):
    wo = w_out_ab.astype(BF16)
    p = dict(
        norm_mix=norm_mix, norm_mlp=norm_mlp, norm_final=norm_final, w_ab=_prep_ab_weights(w_in_ab),
        ssd_conv_w=ssd_conv_w, ssd_conv_b=ssd_conv_b, ssd_dt_bias=ssd_dt_bias, ssd_a_log=ssd_a_log, ssd_d=ssd_d,
        ssd_norm_w=ssd_norm_w, gdn_conv_w=gdn_conv_w, gdn_dt_bias=gdn_dt_bias, gdn_a_log=gdn_a_log,
        gdn_norm_w=gdn_norm_w, w_out_ab=[wo[:SSD_INNER], wo[SSD_INNER:]], w_nsa=_prep_nsa_weights(w_in_nsa),
        lt_wk=_lt_weights(cmp_w1_k), lt_wv=_lt_weights(cmp_w1_v),
        cmp_pe_k=cmp_pe_k, cmp_w1_k=cmp_w1_k, cmp_b1_k=cmp_b1_k, cmp_w2_k=cmp_w2_k,
        cmp_pe_v=cmp_pe_v, cmp_w1_v=cmp_w1_v, cmp_b1_v=cmp_b1_v, cmp_w2_v=cmp_w2_v,
        w_out_nsa=w_out_nsa.astype(BF16), w_up=w_up.astype(BF16), w_down=w_down.astype(BF16))
    bp = x_prompt.shape[0]
    zeros = lambda *s: jnp.zeros(s, F32)
    tm = 256
    out_p = _trunk(x_prompt, zeros(bp, SSD_HEADS, SSD_HEADDIM, SSD_STATE), zeros(bp, CONV_W - 1, SSD_CONV_DIM),
                   zeros(bp, GDN_V_HEADS, GDN_HEAD_DIM, GDN_HEAD_DIM), zeros(bp, CONV_W - 1, GDN_CONV_DIM),
                   lambda x1: nsa_prompt(x1, p, tm), p, tm)
    out_s = _trunk(x_sample, state_ssd, state_ssd_conv, state_gdn, state_gdn_conv,
                   lambda x1: nsa_sample(x1, cache_cmp_kv, cache_slc_kv, state_win_kv, page_table, p, 16), p, tm)
    return tuple(v for pair in zip(out_p, out_s) for v in pair)
```

```python
import functools

import jax
import jax.numpy as jnp
from jax import lax
from jax.experimental import pallas as pl
from jax.experimental.pallas import tpu as pltpu

F32 = jnp.float32
BF16 = jnp.bfloat16

RMS_EPS = 1e-6
NEG_INF = -1e30
MASKED = 2.0 * NEG_INF
FORCE_BONUS = 1e4
LOG2E = 1.4426950408889634
CONV_W = 4
CHUNK = 64
SSD_HEADS, SSD_HEADDIM, SSD_STATE, SSD_GROUPS = 16, 64, 128, 2
SSD_INNER = SSD_HEADS * SSD_HEADDIM
SSD_CONV_DIM = SSD_INNER + 2 * SSD_GROUPS * SSD_STATE
GDN_QK_HEADS, GDN_V_HEADS, GDN_HEAD_DIM = 4, 8, 128
GDN_QK_DIM = GDN_QK_HEADS * GDN_HEAD_DIM
GDN_V_DIM = GDN_V_HEADS * GDN_HEAD_DIM
GDN_CONV_DIM = 2 * GDN_QK_DIM + GDN_V_DIM
NSA_HEADS, NSA_KV_HEADS, NSA_HEAD_DIM = 16, 4, 64
NSA_GROUP = NSA_HEADS // NSA_KV_HEADS
NSA_KV_COLS = NSA_KV_HEADS * NSA_HEAD_DIM
CMP_STRIDE, CMP_LEN, CMP_HIDDEN = 16, 32, 64
SLC_BLOCK, SLC_TOPN, WINDOW = 64, 16, 512
PAGE_SIZE = 128
LANES = 128
VMEM_LIMIT = 56 * 1024 * 1024


def _cparams(sem):
    return pltpu.CompilerParams(dimension_semantics=sem, vmem_limit_bytes=VMEM_LIMIT)


def _dot(a, b):
    return jnp.dot(a, b, preferred_element_type=F32)


def _dot_nt(a, b):
    return lax.dot_general(a, b, (((1,), (1,)), ((), ())), preferred_element_type=F32)


def _dot_tn(a, b):
    return lax.dot_general(a, b, (((0,), (0,)), ((), ())), preferred_element_type=F32)


def _split2(x):
    hi = x.astype(BF16)
    lo = (x - hi.astype(F32)).astype(BF16)
    return hi, lo


def _split3(x):
    hi = x.astype(BF16)
    r = x - hi.astype(F32)
    mid = r.astype(BF16)
    lo = (r - mid.astype(F32)).astype(BF16)
    return hi, mid, lo


def _mm3(a, b):
    ah, al = _split2(a)
    bh, bl = _split2(b)
    return _dot(ah, bh) + (_dot(ah, bl) + _dot(al, bh))


def _rms_unit(x):
    return x * lax.rsqrt(jnp.mean(x * x, axis=-1, keepdims=True) + RMS_EPS)


def _silu(x):
    return x * jax.nn.sigmoid(x)


def _softplus(x):
    return jnp.maximum(x, 0.0) + jnp.log1p(jnp.exp(-jnp.abs(x)))


def _iota2(shape, axis):
    return lax.broadcasted_iota(jnp.int32, shape, axis)


def _norm_proj_body(x_ref, g_ref, *refs, n_out, n_t):
    n_w = n_out + n_t
    w_refs, o_refs = refs[:n_w], refs[n_w:]
    hn = (_rms_unit(x_ref[...]) * g_ref[...]).astype(BF16)
    for w_ref, o_ref in zip(w_refs[:n_out], o_refs[:n_out]):
        n = w_ref.shape[1]
        for c0 in range(0, n, 512):
            c1 = min(n, c0 + 512)
            o_ref[:, c0:c1] = _dot(hn, w_ref[:, c0:c1])
    for w_ref, o_ref in zip(w_refs[n_out:], o_refs[n_out:]):
        n = w_ref.shape[0]
        for c0 in range(0, n, 512):
            c1 = min(n, c0 + 512)
            o_ref[0, c0:c1, :] = _dot_nt(w_ref[c0:c1, :], hn)


def norm_proj(x, g, ws, tm, wts=()):
    t, d = x.shape
    assert t % tm == 0
    in_specs = [pl.BlockSpec((tm, d), lambda i: (i, 0)), pl.BlockSpec((1, d), lambda i: (0, 0))]
    in_specs += [pl.BlockSpec(w.shape, lambda i: (0, 0), pipeline_mode=pl.Buffered(1)) for w in (*ws, *wts)]
    out_specs = [pl.BlockSpec((tm, w.shape[1]), lambda i: (i, 0)) for w in ws]
    out_specs += [pl.BlockSpec((1, w.shape[0], tm), lambda i: (i, 0, 0)) for w in wts]
    out_shape = [jax.ShapeDtypeStruct((t, w.shape[1]), F32) for w in ws]
    out_shape += [jax.ShapeDtypeStruct((t // tm, w.shape[0], tm), F32) for w in wts]
    return pl.pallas_call(
        functools.partial(_norm_proj_body, n_out=len(ws), n_t=len(wts)),
        grid=(t // tm,), in_specs=in_specs, out_specs=out_specs, out_shape=out_shape,
        compiler_params=_cparams(("parallel",)), name="norm_proj",
    )(x, g.reshape(1, d), *ws, *wts)


def _mix_mlp_body(x_ref, *refs, group_sizes, transposed, final, hc):
    n_y, n_g = sum(group_sizes), len(group_sizes)
    y_refs = refs[:n_y]
    wo_refs = refs[n_y:n_y + n_g]
    g_ref, wu_ref, wd_ref = refs[n_y + n_g:n_y + n_g + 3]
    rest = refs[n_y + n_g + 3:]
    gf_ref = rest[0] if final else None
    o_ref = rest[-1]
    x1 = x_ref[...]
    i = 0
    for gs, wo_ref in zip(group_sizes, wo_refs):
        rd = (lambda r: r[0]) if transposed else (lambda r: r[...])
        y = rd(y_refs[i])
        for r in y_refs[i + 1:i + gs]:
            y = y + rd(r)
        i += gs
        x1 = x1 + (_dot_tn if transposed else _dot)(y.astype(BF16), wo_ref[...])
    hn = (_rms_unit(x1) * g_ref[...]).astype(BF16)
    acc = x1
    hidden = wu_ref.shape[1]
    for c0 in range(0, hidden, hc):
        h = jnp.maximum(_dot(hn, wu_ref[:, c0:c0 + hc]), 0.0)
        acc = acc + _dot((h * h).astype(BF16), wd_ref[c0:c0 + hc, :])
    if final:
        acc = _rms_unit(acc) * gf_ref[...]
    o_ref[...] = acc


def mix_mlp(x, y_groups, w_outs, g, w_up, w_down, g_final, tm, hc=None, transposed=False):
    t, d = x.shape
    hc = hc or w_up.shape[1]
    assert t % tm == 0
    final = g_final is not None
    const = lambda i: (0, 0)
    ys = [y for grp in y_groups for y in grp]
    in_specs = [pl.BlockSpec((tm, d), lambda i: (i, 0))]
    if transposed:
        assert all(y.shape[0] == t // tm and y.shape[2] == tm for y in ys)
        in_specs += [pl.BlockSpec((1, y.shape[1], tm), lambda i: (i, 0, 0)) for y in ys]
    else:
        in_specs += [pl.BlockSpec((tm, y.shape[1]), lambda i: (i, 0)) for y in ys]
    resident = lambda w: pl.BlockSpec(w.shape, const, pipeline_mode=pl.Buffered(1))
    in_specs += [resident(w) for w in w_outs]
    in_specs += [pl.BlockSpec((1, d), const), resident(w_up), resident(w_down)]
    args = [x, *ys, *w_outs, g.reshape(1, d), w_up, w_down]
    if final:
        in_specs.append(pl.BlockSpec((1, d), const))
        args.append(g_final.reshape(1, d))
    return pl.pallas_call(
        functools.partial(_mix_mlp_body, group_sizes=tuple(len(grp) for grp in y_groups), transposed=transposed,
                          final=final, hc=hc),
        grid=(t // tm,), in_specs=in_specs, out_specs=pl.BlockSpec((tm, d), lambda i: (i, 0)),
        out_shape=jax.ShapeDtypeStruct((t, d), F32),
        compiler_params=_cparams(("parallel",)), name="mix_mlp",
    )(*args)


def _causal_conv(x, xp_ref, w_ref, first, conv0_ref):
    c = x.shape[0]

    @pl.when(first)
    def _():
        xp_ref[5:8, :] = conv0_ref[0]

    xp_ref[8:8 + c, :] = x
    y = xp_ref[5:5 + c, :] * w_ref[0:1, :]
    for k in range(1, CONV_W):
        y = y + xp_ref[5 + k:5 + k + c, :] * w_ref[k:k + 1, :]
    xp_ref[5:8, :] = x[c - 3:c, :]
    return y


def _cumsum_pair(col, row):
    c = col.shape[0]
    ri, ci = _iota2((c, c), 0), _iota2((c, c), 1)
    tril = jnp.where(ri >= ci, 1.0, 0.0).astype(BF16)
    triu = jnp.where(ri <= ci, 1.0, 0.0).astype(BF16)
    ch, cm, cl = _split3(col)
    rh, rm, rl = _split3(row)
    cs = _dot(tril, ch) + (_dot(tril, cm) + _dot(tril, cl))
    cst = _dot(rh, triu) + (_dot(rm, triu) + _dot(rl, triu))
    return cs, cst


def _ssd_body(xbc_ref, z_ref, sm_ref, smt_ref, st0_ref, cv0_ref, cw_ref, cb_ref, dtb_r_ref, dtb_c_ref,
              al_r_ref, al_c_ref, dfull_ref, nw_ref, y_ref, st_ref, xp_ref, ybuf_ref, *, l_valid, nsub):
    cidx = pl.program_id(1)
    c = CHUNK
    tb = nsub * c
    n, p = SSD_STATE, SSD_HEADDIM
    heads_per_group = SSD_HEADS // SSD_GROUPS
    heads = range(SSD_HEADS)
    grp = lambda h: h // heads_per_group

    @pl.when(cidx == 0)
    def _():
        st_ref[0] = st0_ref[0]

    conv = _causal_conv(xbc_ref[0], xp_ref, cw_ref, cidx == 0, cv0_ref) + cb_ref[...]
    act = _silu(conv)
    xs = act[:, :SSD_INNER]
    valid_c = (cidx * tb + _iota2((tb, 1), 0)) < l_valid
    dt_all = jnp.where(valid_c, _softplus(sm_ref[0][:, 0:SSD_HEADS] + dtb_r_ref[...]), 0.0)
    a_r, a_c = -jnp.exp(al_r_ref[...]), -jnp.exp(al_c_ref[...])
    causal = _iota2((c, c), 0) >= _iota2((c, c), 1)

    subs = []
    for s in range(nsub):
        rows = slice(s * c, (s + 1) * c)
        valid_r = (cidx * tb + s * c + _iota2((1, c), 1)) < l_valid
        dt = dt_all[rows]
        dtt = jnp.where(valid_r, _softplus(smt_ref[0, s][0:SSD_HEADS, :] + dtb_c_ref[...]), 0.0)
        cs, cst = _cumsum_pair(dt * a_r, dtt * a_c)
        cs_last = cs[c - 1:c, :]
        bm_l = [act[rows, SSD_INNER + g * n:SSD_INNER + (g + 1) * n].astype(BF16) for g in range(SSD_GROUPS)]
        c0 = SSD_INNER + SSD_GROUPS * n
        cm_l = [act[rows, c0 + g * n:c0 + (g + 1) * n].astype(BF16) for g in range(SSD_GROUPS)]
        subs.append(dict(rows=rows, dtt=dtt, cs=cs, cst=cst, ecs=jnp.exp(cs), to_end=jnp.exp(cs_last - cs) * dt,
                         edec=jnp.exp(cs_last), bm=bm_l, cm=cm_l, x=[xs[rows, h * p:(h + 1) * p] for h in heads]))
    for d in subs:
        d["cb"] = [_dot_nt(c_g, b_g) for c_g, b_g in zip(d["cm"], d["bm"])]
    for d in subs:
        d["sc"] = []
        for h in heads:
            seg = d["cs"][:, h:h + 1] - d["cst"][h:h + 1, :]
            decay = jnp.exp(jnp.where(causal, seg, NEG_INF))
            d["sc"].append((d["cb"][grp(h)] * decay * d["dtt"][h:h + 1, :]).astype(BF16))
    for d in subs:
        d["yd"] = [_dot(d["sc"][h], d["x"][h].astype(BF16)) for h in heads]

    s_l = [st_ref[0, h] for h in heads]
    for d in subs:
        yo_l = [_dot_nt(d["cm"][grp(h)], s_l[h].astype(BF16)) * d["ecs"][:, h:h + 1] for h in heads]
        for h in heads:
            ybuf_ref[d["rows"], h * p:(h + 1) * p] = d["yd"][h] + yo_l[h]
        s_l = [s_l[h] * d["edec"][:, h:h + 1]
               + _dot_tn((d["x"][h] * d["to_end"][:, h:h + 1]).astype(BF16), d["bm"][grp(h)]) for h in heads]
    for h in heads:
        st_ref[0, h] = s_l[h]

    y = ybuf_ref[...] + dfull_ref[...] * xs
    yz = y * _silu(z_ref[0])
    gw = SSD_INNER // SSD_GROUPS
    for g in range(SSD_GROUPS):
        seg = yz[:, g * gw:(g + 1) * gw]
        y_ref[0, :, g * gw:(g + 1) * gw] = _rms_unit(seg) * nw_ref[:, g * gw:(g + 1) * gw]


def ssd_mixer(xbc, z, small, small_t, state0, conv0, conv_w, conv_b, dt_bias, a_log, d_skip, norm_w, l_valid):
    b, lp, _ = xbc.shape
    nc = lp // CHUNK
    nsub = 1
    h = SSD_HEADS
    d_full = jnp.repeat(d_skip, SSD_HEADDIM).reshape(1, SSD_INNER)
    const2 = lambda i, j: (0, 0)
    tok = lambda w: pl.BlockSpec((1, nsub * CHUNK, w), lambda i, j: (i, j, 0))
    in_specs = [
        tok(SSD_CONV_DIM), tok(SSD_INNER), tok(LANES),
        pl.BlockSpec((1, nsub, 32, CHUNK), lambda i, j: (i, j, 0, 0)),
        pl.BlockSpec((1, h, SSD_HEADDIM, SSD_STATE), lambda i, j: (i, 0, 0, 0)),
        pl.BlockSpec((1, CONV_W - 1, SSD_CONV_DIM), lambda i, j: (i, 0, 0)),
        pl.BlockSpec((CONV_W, SSD_CONV_DIM), const2), pl.BlockSpec((1, SSD_CONV_DIM), const2),
        pl.BlockSpec((1, h), const2), pl.BlockSpec((h, 1), const2),
        pl.BlockSpec((1, h), const2), pl.BlockSpec((h, 1), const2),
        pl.BlockSpec((1, SSD_INNER), const2), pl.BlockSpec((1, SSD_INNER), const2),
    ]
    out_specs = [tok(SSD_INNER), pl.BlockSpec((1, h, SSD_HEADDIM, SSD_STATE), lambda i, j: (i, 0, 0, 0))]
    out_shape = [jax.ShapeDtypeStruct((b, lp, SSD_INNER), F32),
                 jax.ShapeDtypeStruct((b, h, SSD_HEADDIM, SSD_STATE), F32)]
    return pl.pallas_call(
        functools.partial(_ssd_body, l_valid=l_valid, nsub=nsub),
        grid=(b, nc // nsub), in_specs=in_specs, out_specs=out_specs, out_shape=out_shape,
        scratch_shapes=[pltpu.VMEM((nsub * CHUNK + 8, SSD_CONV_DIM), F32),
                        pltpu.VMEM((nsub * CHUNK, SSD_INNER), F32)],
        compiler_params=_cparams(("parallel", "arbitrary")), name="ssd_mixer",
    )(xbc, z, small, small_t, state0, conv0, conv_w, conv_b.reshape(1, -1),
      dt_bias.reshape(1, h), dt_bias.reshape(h, 1), a_log.reshape(1, h), a_log.reshape(h, 1),
      d_full, norm_w.reshape(1, SSD_INNER))


def _unit_lower_inverses(mats):
    c = mats[0].shape[0]
    ri, ci = _iota2((c, c), 0), _iota2((c, c), 1)
    eye = jnp.where(ri == ci, 1.0, 0.0)
    same_blk = lax.shift_right_logical(ri, 4) == lax.shift_right_logical(ci, 4)
    d = [jnp.where(same_blk, a, 0.0) for a in mats]
    nn = [a - x for a, x in zip(mats, d)]
    d2 = [_mm3(x, x) for x in d]
    d4 = [_mm3(x, x) for x in d2]
    pinv = [eye - x for x in d]
    pinv = [x + _mm3(x, y) for x, y in zip(pinv, d2)]
    d8 = [_mm3(x, x) for x in d4]
    pinv = [x + _mm3(x, y) for x, y in zip(pinv, d4)]
    pinv = [x + _mm3(x, y) for x, y in zip(pinv, d8)]
    e = [_mm3(x, y) for x, y in zip(pinv, nn)]
    e2 = [_mm3(x, x) for x in e]
    q = [eye - x for x in e]
    q = [x + _mm3(x, y) for x, y in zip(q, e2)]
    return [_mm3(x, y) for x, y in zip(q, pinv)]


def _l2norm(x):
    return x * lax.rsqrt(jnp.sum(x * x, axis=-1, keepdims=True) + 1e-6)


def _gdn_body(qkv_ref, z_ref, sm_ref, smt_ref, st0_ref, cv0_ref, cw_ref, dtb_r_ref, dtb_c_ref,
              al_r_ref, al_c_ref, nw_ref, o_ref, st_ref, xp_ref, *, l_valid, nsub):
    cidx = pl.program_id(1)
    c = CHUNK
    tb = nsub * c
    hd = GDN_HEAD_DIM
    nh = GDN_V_HEADS
    heads = range(nh)
    rep = GDN_V_HEADS // GDN_QK_HEADS

    @pl.when(cidx == 0)
    def _():
        st_ref[0] = st0_ref[0]

    act = _silu(_causal_conv(qkv_ref[0], xp_ref, cw_ref, cidx == 0, cv0_ref))
    valid_c = (cidx * tb + _iota2((tb, 1), 0)) < l_valid
    sm = sm_ref[0]
    beta_all = jnp.where(valid_c, jax.nn.sigmoid(sm[:, 16:16 + nh]), 0.0)
    g_all = jnp.where(valid_c, -jnp.exp(al_r_ref[...]) * _softplus(sm[:, 24:24 + nh] + dtb_r_ref[...]), 0.0)
    q_all = [_l2norm(act[:, hq * hd:(hq + 1) * hd]) * (hd ** -0.5) for hq in range(GDN_QK_HEADS)]
    k_all = [_l2norm(act[:, GDN_QK_DIM + hq * hd:GDN_QK_DIM + (hq + 1) * hd]) for hq in range(GDN_QK_HEADS)]
    ri, ci = _iota2((c, c), 0), _iota2((c, c), 1)
    causal = ri >= ci
    strict = ri > ci
    eye = jnp.where(ri == ci, 1.0, 0.0)

    subs = []
    for s in range(nsub):
        rows = slice(s * c, (s + 1) * c)
        valid_r = (cidx * tb + s * c + _iota2((1, c), 1)) < l_valid
        g_r = jnp.where(valid_r, -jnp.exp(al_c_ref[...]) * _softplus(smt_ref[0, s][24:24 + nh, :] + dtb_c_ref[...]),
                        0.0)
        cs, cst = _cumsum_pair(g_all[rows], g_r)
        cs_last = cs[c - 1:c, :]
        d = dict(rows=rows, beta=beta_all[rows], ecs=jnp.exp(cs), e_end=jnp.exp(cs_last - cs),
                 etot=jnp.exp(cs_last), q=[x[rows] for x in q_all], k=[x[rows] for x in k_all], qk=[], a=[])
        for hq in range(GDN_QK_HEADS):
            k16 = d["k"][hq].astype(BF16)
            kk = _dot_nt(k16, k16)
            qk0 = _dot_nt(d["q"][hq].astype(BF16), k16)
            for r in range(rep):
                h = hq * rep + r
                seg = cs[:, h:h + 1] - cst[h:h + 1, :]
                decay = jnp.exp(jnp.where(causal, seg, NEG_INF))
                d["qk"].append((qk0 * decay).astype(BF16))
                d["a"].append(jnp.where(strict, kk * d["beta"][:, h:h + 1] * decay, 0.0))
        subs.append(d)
    tinv_all = _unit_lower_inverses([a for d in subs for a in d["a"]])
    for s, d in enumerate(subs):
        rhs_l = []
        for h in heads:
            b_h = d["beta"][:, h:h + 1]
            v_h = act[d["rows"], 2 * GDN_QK_DIM + h * hd:2 * GDN_QK_DIM + (h + 1) * hd]
            rhs_l.append(jnp.concatenate([v_h * b_h, d["k"][h // rep] * (b_h * d["ecs"][:, h:h + 1])], axis=1))
        d["sol"] = [rhs_l[h] + _dot((tinv_all[s * nh + h] - eye).astype(BF16), rhs_l[h].astype(BF16))
                    for h in heads]

    s_l = [st_ref[0, h] for h in heads]
    for d in subs:
        ecs, rows = d["ecs"], d["rows"]
        s16_l = [x.astype(BF16) for x in s_l]
        o1_l = [_dot((d["q"][h // rep] * ecs[:, h:h + 1]).astype(BF16), s16_l[h]) for h in heads]
        v16_l = [(d["sol"][h][:, :hd] - _dot(d["sol"][h][:, hd:].astype(BF16), s16_l[h])).astype(BF16)
                 for h in heads]
        o_l = [o1_l[h] + _dot(d["qk"][h], v16_l[h]) for h in heads]
        s_l = [s_l[h] * d["etot"][:, h:h + 1]
               + _dot_tn((d["k"][h // rep] * d["e_end"][:, h:h + 1]).astype(BF16), v16_l[h]) for h in heads]
        for h in heads:
            o_ref[0, rows, h * hd:(h + 1) * hd] = ((_rms_unit(o_l[h]) * nw_ref[...])
                                                   * _silu(z_ref[0, rows, h * hd:(h + 1) * hd]))
    for h in heads:
        st_ref[0, h] = s_l[h]


def _chunks_per_step(nc):
    return 2 if nc % 2 == 0 else 1


def gdn_mixer(qkv, z, small, small_t, state0, conv0, conv_w, dt_bias, a_log, norm_w, l_valid):
    b, lp, _ = qkv.shape
    nc = lp // CHUNK
    nsub = _chunks_per_step(nc)
    h = GDN_V_HEADS
    const2 = lambda i, j: (0, 0)
    tok = lambda w: pl.BlockSpec((1, nsub * CHUNK, w), lambda i, j: (i, j, 0))
    in_specs = [
        tok(GDN_CONV_DIM), tok(GDN_V_DIM), tok(LANES),
        pl.BlockSpec((1, nsub, 32, CHUNK), lambda i, j: (i, j, 0, 0)),
        pl.BlockSpec((1, h, GDN_HEAD_DIM, GDN_HEAD_DIM), lambda i, j: (i, 0, 0, 0)),
        pl.BlockSpec((1, CONV_W - 1, GDN_CONV_DIM), lambda i, j: (i, 0, 0)),
        pl.BlockSpec((CONV_W, GDN_CONV_DIM), const2),
        pl.BlockSpec((1, h), const2), pl.BlockSpec((h, 1), const2),
        pl.BlockSpec((1, h), const2), pl.BlockSpec((h, 1), const2),
        pl.BlockSpec((1, GDN_HEAD_DIM), const2),
    ]
    out_specs = [tok(GDN_V_DIM), pl.BlockSpec((1, h, GDN_HEAD_DIM, GDN_HEAD_DIM), lambda i, j: (i, 0, 0, 0))]
    out_shape = [jax.ShapeDtypeStruct((b, lp, GDN_V_DIM), F32),
                 jax.ShapeDtypeStruct((b, h, GDN_HEAD_DIM, GDN_HEAD_DIM), F32)]
    return pl.pallas_call(
        functools.partial(_gdn_body, l_valid=l_valid, nsub=nsub),
        grid=(b, nc // nsub), in_specs=in_specs, out_specs=out_specs, out_shape=out_shape,
        scratch_shapes=[pltpu.VMEM((nsub * CHUNK + 8, GDN_CONV_DIM), F32)],
        compiler_params=_cparams(("parallel", "arbitrary")), name="gdn_mixer",
    )(qkv, z, small, small_t, state0, conv0, conv_w,
      dt_bias.reshape(1, h), dt_bias.reshape(h, 1), a_log.reshape(1, h), a_log.reshape(h, 1),
      norm_w.reshape(1, GDN_HEAD_DIM))


def _prep_ab_weights(w_in_ab):
    o = [0, SSD_INNER]
    o.append(o[-1] + SSD_CONV_DIM)
    o.append(o[-1] + SSD_HEADS)
    o.append(o[-1] + GDN_CONV_DIM)
    o.append(o[-1] + GDN_V_DIM)
    o.append(o[-1] + 2 * GDN_V_HEADS)
    w = w_in_ab.astype(BF16)
    small = jnp.concatenate([w[:, o[2]:o[3]], w[:, o[5]:o[6]]], axis=1)
    small = jnp.pad(small, ((0, 0), (0, LANES - small.shape[1])))
    return [w[:, o[0]:o[1]], w[:, o[1]:o[2]], w[:, o[3]:o[4]], w[:, o[4]:o[5]], small]


def _pad_time(a, lp):
    return jnp.pad(a, [(0, 0), (0, lp - a.shape[1])] + [(0, 0)] * (a.ndim - 2))


def layer0_mixer(x, ssd_state0, ssd_conv0, gdn_state0, gdn_conv0, p, tm):
    b, l, d = x.shape
    lp = -(-l // CHUNK) * CHUNK
    xp = _pad_time(x, lp).reshape(b * lp, d)
    tm_proj = 2 * tm if (b * lp) % (2 * tm) == 0 else min(tm, b * lp)
    z_ssd, xbc, qkv, z_gdn, small = norm_proj(xp, p["norm_mix"][0], p["w_ab"], tm_proj)
    nc = lp // CHUNK
    small_t = small[:, :32].reshape(b, nc, CHUNK, 32).transpose(0, 1, 3, 2)
    r3 = lambda a: a.reshape(b, lp, a.shape[-1])
    xbc, qkv, small = r3(xbc), r3(qkv), r3(small)
    y_ssd, ssd_state = ssd_mixer(xbc, r3(z_ssd), small, small_t, ssd_state0, ssd_conv0, p["ssd_conv_w"],
                                 p["ssd_conv_b"], p["ssd_dt_bias"], p["ssd_a_log"], p["ssd_d"], p["ssd_norm_w"], l)
    y_gdn, gdn_state = gdn_mixer(qkv, r3(z_gdn), small, small_t, gdn_state0, gdn_conv0, p["gdn_conv_w"],
                                 p["gdn_dt_bias"], p["gdn_a_log"], p["gdn_norm_w"], l)
    ssd_conv = jnp.concatenate([ssd_conv0, xbc[:, :l]], axis=1)[:, l:]
    gdn_conv = jnp.concatenate([gdn_conv0, qkv[:, :l]], axis=1)[:, l:]
    y_ssd = y_ssd[:, :l].reshape(b * l, SSD_INNER)
    y_gdn = y_gdn[:, :l].reshape(b * l, GDN_V_DIM)
    return y_ssd, y_gdn, ssd_state, ssd_conv, gdn_state, gdn_conv


def _nsa_lt_body(*refs, n_in):
    x_refs = refs[:n_in]
    wk_ref, wv_ref, o_ref = refs[n_in:]
    w = 2 * NSA_KV_COLS
    acc_k = acc_v = None
    for j in range(CMP_STRIDE):
        xk = jnp.concatenate([r[0, :, j * w:j * w + NSA_KV_COLS] for r in x_refs], axis=0).astype(BF16)
        xv = jnp.concatenate([r[0, :, j * w + NSA_KV_COLS:(j + 1) * w] for r in x_refs], axis=0).astype(BF16)
        dk, dv = _dot(xk, wk_ref[j]), _dot(xv, wv_ref[j])
        acc_k = dk if acc_k is None else acc_k + dk
        acc_v = dv if acc_v is None else acc_v + dv
    o_ref[0, :, 0:2 * NSA_KV_COLS] = acc_k
    o_ref[0, :, 2 * NSA_KV_COLS:] = acc_v


def _lt_weights(w1):
    w1r = w1.reshape(2, CMP_STRIDE, NSA_HEAD_DIM, CMP_HIDDEN)
    eye = jnp.eye(NSA_KV_HEADS, dtype=w1.dtype)
    lead = jnp.einsum("ab,jdh->jadbh", eye, w1r[0]).reshape(CMP_STRIDE, NSA_KV_COLS, NSA_KV_COLS)
    tail = jnp.einsum("ab,jdh->jadbh", eye, w1r[1]).reshape(CMP_STRIDE, NSA_KV_COLS, NSA_KV_COLS)
    return jnp.concatenate([lead, tail], axis=-1).astype(BF16)


def nsa_lt_dense(kv, wk, wv, tc):
    b, l, w = kv.shape
    nch = l // CMP_STRIDE
    x = kv[:, :nch * CMP_STRIDE].reshape(b, nch, CMP_STRIDE * w)
    tc = min(tc, nch)
    assert nch % tc == 0
    const3 = lambda i, j: (0, 0, 0)
    return pl.pallas_call(
        functools.partial(_nsa_lt_body, n_in=1), grid=(b, nch // tc),
        in_specs=[pl.BlockSpec((1, tc, CMP_STRIDE * w), lambda i, j: (i, j, 0)),
                  pl.BlockSpec(wk.shape, const3), pl.BlockSpec(wv.shape, const3)],
        out_specs=pl.BlockSpec((1, tc, 4 * NSA_KV_COLS), lambda i, j: (i, j, 0)),
        out_shape=jax.ShapeDtypeStruct((b, nch, 4 * NSA_KV_COLS), F32),
        compiler_params=_cparams(("parallel", "parallel")), name="nsa_lt_dense",
    )(x, wk, wv)


def _nsa_lt_paged_body(pt_ref, *refs, n_pg):
    page_refs = refs[:n_pg]
    wk_ref, wv_ref, o_ref, xa_ref, xb_ref = refs[n_pg:]
    gi = pl.program_id(1)
    page = page_refs[0].shape[2]
    n_col = page_refs[0].shape[1] // LANES
    rows = n_pg * page // CMP_STRIDE
    half = n_col // 2

    @pl.when(gi == 0)
    def _():
        xb_ref[...] = jnp.zeros(xb_ref.shape, F32)

    chunks = page // CMP_STRIDE
    ti, ci = _iota2((page, page), 0), _iota2((page, page), 1)
    perm = jnp.where(ci == jnp.bitwise_and(ti, CMP_STRIDE - 1) * chunks + lax.shift_right_logical(ti, 4),
                     1.0, 0.0).astype(BF16)

    def work(fill_ref, read_ref):
        for i, r in enumerate(page_refs):
            xp = _dot(r[0].astype(BF16), perm)
            for c in range(n_col):
                fill_ref[c, i * page:(i + 1) * page, :] = xp[c * LANES:(c + 1) * LANES, :].T
        acc_k = acc_v = None
        for j in range(CMP_STRIDE):
            xj = [jnp.concatenate([read_ref[c, i * page + j * chunks:i * page + (j + 1) * chunks, :]
                                   for i in range(n_pg)], axis=0) for c in range(n_col)]
            dk = _dot(jnp.concatenate(xj[:half], axis=1).astype(BF16), wk_ref[j])
            dv = _dot(jnp.concatenate(xj[half:], axis=1).astype(BF16), wv_ref[j])
            acc_k = dk if acc_k is None else acc_k + dk
            acc_v = dv if acc_v is None else acc_v + dv
        o_ref[0, :, 0:2 * NSA_KV_COLS] = acc_k
        o_ref[0, :, 2 * NSA_KV_COLS:] = acc_v

    @pl.when(jnp.bitwise_and(gi, 1) == 0)
    def _():
        work(xa_ref, xb_ref)

    @pl.when(jnp.bitwise_and(gi, 1) == 1)
    def _():
        work(xb_ref, xa_ref)


def nsa_lt_paged(cache_t, page_table, wk, wv, n_pg):
    npool, w, page = cache_t.shape
    b, n_pages = page_table.shape
    rows = page // CMP_STRIDE
    n_grp = n_pages // n_pg
    assert n_pages % n_pg == 0
    page_spec = lambda i: pl.BlockSpec(
        (1, w, page), lambda bi, gi, pt: (pt[bi, jnp.minimum(gi, n_grp - 1) * n_pg + i], 0, 0))
    const3 = lambda bi, gi, pt: (0, 0, 0)
    buf = pltpu.VMEM((w // LANES, n_pg * page, LANES), F32)
    grid_spec = pltpu.PrefetchScalarGridSpec(
        num_scalar_prefetch=1, grid=(b, n_grp + 1),
        in_specs=[page_spec(i) for i in range(n_pg)] + [pl.BlockSpec(wk.shape, const3), pl.BlockSpec(wv.shape, const3)],
        out_specs=pl.BlockSpec((1, n_pg * rows, 4 * NSA_KV_COLS), lambda bi, gi, pt: (bi, jnp.maximum(gi - 1, 0), 0)),
        scratch_shapes=[buf, buf])
    return pl.pallas_call(
        functools.partial(_nsa_lt_paged_body, n_pg=n_pg), grid_spec=grid_spec,
        out_shape=jax.ShapeDtypeStruct((b, n_pages * rows, 4 * NSA_KV_COLS), F32),
        compiler_params=_cparams(("parallel", "arbitrary")), name="nsa_lt_paged",
    )(page_table, *([cache_t] * n_pg), wk, wv)


def _cmp_kv_body(lt_ref, pek_ref, w1k_ref, b1k_ref, w2k_ref, pev_ref, w1v_ref, b1v_ref, w2v_ref, w2vt_ref,
                 kc_ref, vc_ref, vct_ref, sh_ref):
    nch = lt_ref.shape[1]
    kvc = NSA_KV_COLS

    def branch(off, pe_ref, w1_ref, b1_ref, w2_ref, out_ref):
        pe = jnp.broadcast_to(pe_ref[...], (8, pe_ref.shape[1]))
        pe_term = _mm3(pe, w1_ref[...])[0:1, :] + b1_ref[...]
        pe4 = jnp.concatenate([pe_term] * NSA_KV_HEADS, axis=1)
        sh_ref[0:nch, :] = lt_ref[0, :, off + kvc:off + 2 * kvc]
        sh_ref[nch:nch + 8, :] = jnp.zeros((8, kvc), F32)
        hid = _silu(lt_ref[0, :, off:off + kvc] + sh_ref[1:nch + 1, :] + pe4).astype(BF16)
        out_ref[0] = _dot(hid, w2_ref[...])
        return hid

    branch(0, pek_ref, w1k_ref, b1k_ref, w2k_ref, kc_ref)
    hid_v = branch(2 * kvc, pev_ref, w1v_ref, b1v_ref, w2v_ref, vc_ref)
    vct_ref[0] = _dot_nt(w2vt_ref[...], hid_v)


def cmp_kv(lt, p):
    b, nch, _ = lt.shape
    kvc = NSA_KV_COLS
    const2 = lambda i: (0, 0)
    args, specs = [], []
    for s in ("k", "v"):
        pe = p["cmp_pe_" + s].reshape(1, -1)
        w1 = p["cmp_w1_" + s]
        b1 = p["cmp_b1_" + s].reshape(1, -1)
        w2 = jnp.kron(jnp.eye(NSA_KV_HEADS, dtype=F32), p["cmp_w2_" + s]).astype(BF16)
        for a in (pe, w1, b1, w2):
            args.append(a)
            specs.append(pl.BlockSpec(a.shape, const2))
    args.append(jnp.kron(jnp.eye(NSA_KV_HEADS, dtype=F32), p["cmp_w2_v"].T).astype(BF16))
    specs.append(pl.BlockSpec(args[-1].shape, const2))
    out_spec = pl.BlockSpec((1, nch, kvc), lambda i: (i, 0, 0))
    return pl.pallas_call(
        _cmp_kv_body, grid=(b,),
        in_specs=[pl.BlockSpec((1, nch, 4 * kvc), lambda i: (i, 0, 0))] + specs,
        out_specs=[out_spec, out_spec, pl.BlockSpec((1, kvc, nch), lambda i: (i, 0, 0))],
        out_shape=[jax.ShapeDtypeStruct((b, nch, kvc), F32)] * 2 + [jax.ShapeDtypeStruct((b, kvc, nch), F32)],
        scratch_shapes=[pltpu.VMEM((nch + 8, kvc), F32)],
        compiler_params=_cparams(("parallel",)), name="cmp_kv",
    )(lt, *args)


def _cmp_attn_body(q_ref, gate_ref, kc_ref, vc_ref, oc_ref, sel_ref, *, tq, pos0, n_slc, nbp):
    qi = pl.program_id(1)
    ncp = kc_ref.shape[1]
    hd = NSA_HEAD_DIM
    t_col = pos0 + qi * tq + _iota2((tq, 1), 0)
    cmask = (_iota2((1, ncp), 1) * CMP_STRIDE + (CMP_LEN - 1)) <= t_col
    kc = kc_ref[0].astype(BF16)
    vc = vc_ref[0].astype(BF16)
    gate = jax.nn.sigmoid(gate_ref[0])
    nn = _iota2((ncp, nbp), 0) * CMP_STRIDE
    ss = _iota2((ncp, nbp), 1) * SLC_BLOCK
    overlap = jnp.where((nn < ss + SLC_BLOCK) & (nn + CMP_LEN > ss), 1.0, 0.0).astype(BF16)
    blk = _iota2((1, nbp), 1)
    blk_f = blk.astype(F32)
    cur = lax.shift_right_logical(t_col, 6)
    forced = (blk == 0) | (blk == cur) | (blk == cur - 1)
    bonus = jnp.where(forced, FORCE_BONUS, 0.0)
    causal_blk = blk * SLC_BLOCK <= t_col
    n_pick = min(SLC_TOPN, n_slc)

    scores = []
    for k in range(NSA_KV_HEADS):
        kc_k = kc[:, k * hd:(k + 1) * hd]
        vc_k = vc[:, k * hd:(k + 1) * hd]
        hs = [k * NSA_GROUP + g for g in range(NSA_GROUP)]
        q_l = [(q_ref[0, :, h * hd:(h + 1) * hd] * (hd ** -0.5)).astype(BF16) for h in hs]
        s_l = [jnp.where(cmask, _dot_nt(q_h, kc_k), NEG_INF) for q_h in q_l]
        e_l = [jnp.where(cmask, jnp.exp(s - jnp.max(s, axis=-1, keepdims=True)), 0.0) for s in s_l]
        den_l = [jnp.sum(e, axis=-1, keepdims=True) for e in e_l]
        p_l = [e * (1.0 / jnp.where(den > 0.0, den, 1.0)) for e, den in zip(e_l, den_l)]
        o_l = [_dot(p.astype(BF16), vc_k) for p in p_l]
        for h, o in zip(hs, o_l):
            oc_ref[0, :, h * hd:(h + 1) * hd] = o * gate[:, h:h + 1]
        ph, plo = _split2((p_l[0] + p_l[1]) + (p_l[2] + p_l[3]))
        imp = _dot(ph, overlap) + _dot(plo, overlap)
        score = jnp.where(causal_blk, imp + bonus, NEG_INF)
        scores.append(jnp.where(blk < n_slc, score, -jnp.inf))

    def pick_one(_, carry):
        work, sel = carry
        m = jnp.max(work, axis=-1, keepdims=True)
        idx = jnp.min(jnp.where(work == m, blk_f, 1e9), axis=-1, keepdims=True)
        hit = blk_f == idx
        return jnp.where(hit, -jnp.inf, work), jnp.where(hit, 1.0, sel)

    work0 = jnp.concatenate(scores, axis=0)
    _, sel = lax.fori_loop(0, n_pick, pick_one, (work0, jnp.zeros(work0.shape, F32)))
    for k in range(NSA_KV_HEADS):
        sel_ref[0, :, k * nbp:(k + 1) * nbp] = sel[k * tq:(k + 1) * tq]


def _nbp(n_slc):
    return 64 if n_slc <= 64 else -(-n_slc // LANES) * LANES


def cmp_attn_select(q, gate, kc, vc, tq, pos0, n_slc):
    b, l, dq = q.shape
    ncp = kc.shape[1]
    nbp = _nbp(n_slc)
    assert l % tq == 0
    tok = lambda w: pl.BlockSpec((1, tq, w), lambda i, j: (i, j, 0))
    whole = pl.BlockSpec((1, ncp, NSA_KV_COLS), lambda i, j: (i, 0, 0))
    return pl.pallas_call(
        functools.partial(_cmp_attn_body, tq=tq, pos0=pos0, n_slc=n_slc, nbp=nbp),
        grid=(b, l // tq), in_specs=[tok(dq), tok(LANES), whole, whole],
        out_specs=[tok(dq), tok(NSA_KV_HEADS * nbp)],
        out_shape=[jax.ShapeDtypeStruct((b, l, dq), F32), jax.ShapeDtypeStruct((b, l, NSA_KV_HEADS * nbp), F32)],
        compiler_params=_cparams(("parallel", "parallel")), name="cmp_attn_select",
    )(q, gate, kc, vc)


def _nsa_attn_body(q_ref, gate_ref, kv_ref, *rest, mode, tq, tk, qpos0, kpos0, gate_off, nbp):
    if mode == "slc":
        sel_ref, o_ref, qs_ref, m_ref, l_ref, acc_ref = rest
    else:
        o_ref, qs_ref, m_ref, l_ref, acc_ref = rest
    qi = pl.program_id(1)
    hd = NSA_HEAD_DIM
    nk = kv_ref.shape[1] // tk
    qmin = qpos0 + qi * tq
    t_col = qmin + _iota2((tq, 1), 0)
    for k in range(NSA_KV_HEADS):
        qs_ref[k] = jnp.concatenate(
            [q_ref[0, :, (k * NSA_GROUP + g) * hd:(k * NSA_GROUP + g + 1) * hd] * (hd ** -0.5)
             for g in range(NSA_GROUP)], axis=0)
    m_ref[...] = jnp.full(m_ref.shape, NEG_INF, F32)
    l_ref[...] = jnp.zeros(l_ref.shape, F32)
    acc_ref[...] = jnp.zeros(acc_ref.shape, F32)
    hi = jnp.minimum(nk - 1, (qmin + tq - 1 - kpos0) // tk)
    if mode == "slc":
        lo = 0
    else:
        lo = jnp.maximum(qmin - (WINDOW - 1) - kpos0, 0) // tk

    def step(ki, carry):
        k0 = pl.multiple_of(ki * tk, tk)
        kv = kv_ref[0, pl.ds(k0, tk), :]
        kpos = kpos0 + k0 + _iota2((1, tk), 1)
        if mode == "slc":
            expand = jnp.where(_iota2((nbp, tk), 0) == lax.shift_right_logical(k0 + _iota2((nbp, tk), 1), 6),
                               1.0, 0.0).astype(BF16)
            base = kpos <= t_col
        else:
            dist = t_col - kpos
            base = (dist >= 0) & (dist < WINDOW)
        for k in range(NSA_KV_HEADS):
            kk = kv[:, k * hd:(k + 1) * hd].astype(BF16)
            vv = kv[:, NSA_KV_COLS + k * hd:NSA_KV_COLS + (k + 1) * hd].astype(BF16)
            s = _dot_nt(qs_ref[k].astype(BF16), kk)
            if mode == "slc":
                mask = base & (_dot(sel_ref[0, :, k * nbp:(k + 1) * nbp].astype(BF16), expand) > 0.5)
            else:
                mask = base
            for g in range(NSA_GROUP):
                r = k * NSA_GROUP + g
                s_g = jnp.where(mask, s[g * tq:(g + 1) * tq], NEG_INF)
                m_old = m_ref[r]
                m_new = jnp.maximum(m_old, jnp.max(s_g, axis=-1, keepdims=True))
                alpha = jnp.exp(m_old - m_new)
                p = jnp.where(mask, jnp.exp(s_g - m_new), 0.0)
                l_ref[r] = alpha * l_ref[r] + jnp.sum(p, axis=-1, keepdims=True)
                acc_ref[r] = alpha * acc_ref[r] + _dot(p.astype(BF16), vv)
                m_ref[r] = m_new
        return carry

    lax.fori_loop(lo, hi + 1, step, 0)
    gate = jax.nn.sigmoid(gate_ref[0])
    for r in range(NSA_HEADS):
        l = l_ref[r]
        o = acc_ref[r] / jnp.where(l > 0.0, l, 1.0)
        o_ref[0, :, r * hd:(r + 1) * hd] = o * gate[:, gate_off + r:gate_off + r + 1]


def nsa_attn(mode, q, gate, kv, sel, tq, tk, qpos0, kpos0):
    b, l, dq = q.shape
    nkeys = kv.shape[1]
    assert l % tq == 0 and nkeys % tk == 0
    tok = lambda w: pl.BlockSpec((1, tq, w), lambda i, j: (i, j, 0))
    in_specs = [tok(dq), tok(LANES), pl.BlockSpec((1, nkeys, kv.shape[2]), lambda i, j: (i, 0, 0))]
    args = [q, gate, kv]
    nbp = 0
    if mode == "slc":
        nbp = sel.shape[2] // NSA_KV_HEADS
        in_specs.append(tok(sel.shape[2]))
        args.append(sel)
    return pl.pallas_call(
        functools.partial(_nsa_attn_body, mode=mode, tq=tq, tk=tk, qpos0=qpos0, kpos0=kpos0,
                          gate_off=NSA_HEADS * (1 if mode == "slc" else 2), nbp=nbp),
        grid=(b, l // tq), in_specs=in_specs, out_specs=tok(dq),
        out_shape=jax.ShapeDtypeStruct((b, l, dq), F32),
        scratch_shapes=[pltpu.VMEM((NSA_KV_HEADS, NSA_GROUP * tq, NSA_HEAD_DIM), F32),
                        pltpu.VMEM((NSA_HEADS, tq, 1), F32), pltpu.VMEM((NSA_HEADS, tq, 1), F32),
                        pltpu.VMEM((NSA_HEADS, tq, NSA_HEAD_DIM), F32)],
        compiler_params=_cparams(("parallel", "parallel")), name="nsa_attn_" + mode,
    )(*args)


def _slc_paged_body(pt_ref, q_ref, gate_ref, selgrp_ref, selnew_ref, kvnew_ref, *rest, n_pg, tq, l_new):
    page_refs = rest[:n_pg]
    o_ref, qbd_ref, m_ref, l_ref, acc_ref = rest[n_pg:]
    gi = pl.program_id(1)
    hd = NSA_HEAD_DIM
    rows = NSA_HEADS * tq

    @pl.when(gi == 0)
    def _():
        qbd_ref[...] = jnp.zeros(qbd_ref.shape, F32)
        for h in range(NSA_HEADS):
            k = h // NSA_GROUP
            qbd_ref[h * tq:(h + 1) * tq, k * hd:(k + 1) * hd] = q_ref[0, :, h * hd:(h + 1) * hd] * (hd ** -0.5)
        m_ref[...] = jnp.full(m_ref.shape, NEG_INF, F32)
        l_ref[...] = jnp.zeros(l_ref.shape, F32)
        acc_ref[...] = jnp.zeros(acc_ref.shape, F32)

    qbd = qbd_ref[...].astype(BF16)

    def update(s, vv, v_transposed):
        m_old = m_ref[...]
        m_new = jnp.maximum(m_old, jnp.max(s, axis=-1, keepdims=True))
        alpha = jnp.exp(m_old - m_new)
        p = jnp.exp(s - m_new)
        l_ref[...] = alpha * l_ref[...] + jnp.sum(p, axis=-1, keepdims=True)
        acc_ref[...] = alpha * acc_ref[...] + (_dot_nt if v_transposed else _dot)(p.astype(BF16), vv)
        m_ref[...] = m_new

    page = page_refs[0].shape[2]
    wide = n_pg * page
    nblk = wide // SLC_BLOCK
    kt = jnp.concatenate([r[0, 0:NSA_KV_COLS, :] for r in page_refs], axis=1).astype(BF16)
    vt = jnp.concatenate([r[0, NSA_KV_COLS:, :] for r in page_refs], axis=1).astype(BF16)
    expand = jnp.where(_iota2((nblk, wide), 0) == lax.shift_right_logical(_iota2((nblk, wide), 1), 6),
                       1.0, 0.0).astype(BF16)
    selb = ((1.0 - selgrp_ref[0, 0]) * MASKED).astype(BF16)
    update(_dot(qbd, kt) + _dot(selb, expand), vt, True)

    @pl.when(gi == pl.num_programs(1) - 1)
    def _():
        kvn = kvnew_ref[0]
        nr = kvn.shape[0]
        j = _iota2((rows, nr), 1)
        iq = jnp.bitwise_and(_iota2((rows, nr), 0), tq - 1)
        mask = (selnew_ref[0] > 0.5) & (j <= iq) & (j < l_new)
        s_new = _dot_nt(qbd, kvn[:, :NSA_KV_COLS].astype(BF16)) + jnp.where(mask, 0.0, MASKED)
        update(s_new, kvn[:, NSA_KV_COLS:].astype(BF16), False)
        gate = jax.nn.sigmoid(gate_ref[0])
        l = l_ref[...]
        o = acc_ref[...] / jnp.where(l > 0.0, l, 1.0)
        for h in range(NSA_HEADS):
            k = h // NSA_GROUP
            o_ref[0, :, h * hd:(h + 1) * hd] = (o[h * tq:(h + 1) * tq, k * hd:(k + 1) * hd]
                                                * gate[:, NSA_HEADS + h:NSA_HEADS + h + 1])


def slc_paged(q, gate, selrows, kv_new, cache_t, page_table, n_pg, l_new, p0):
    b, tq, dq = q.shape
    npool, w, page = cache_t.shape
    n_pages = page_table.shape[1]
    rows = NSA_HEADS * tq
    n_grp = n_pages // n_pg
    nblk = n_pg * page // SLC_BLOCK
    assert n_pages % n_pg == 0 and p0 % SLC_BLOCK == 0 and l_new <= SLC_BLOCK and p0 == n_pages * page
    assert tq & (tq - 1) == 0
    blk_new = p0 // SLC_BLOCK
    sel_grp = selrows[:, :, :blk_new].reshape(b, rows, n_grp, nblk).transpose(0, 2, 1, 3)
    sel_new = selrows[:, :, blk_new:blk_new + 1]
    per_b = lambda shape: pl.BlockSpec((1,) + shape, lambda bi, gi, pt: (bi, 0, 0))
    page_spec = lambda i: pl.BlockSpec((1, w, page), lambda bi, gi, pt: (pt[bi, gi * n_pg + i], 0, 0))
    grid_spec = pltpu.PrefetchScalarGridSpec(
        num_scalar_prefetch=1, grid=(b, n_grp),
        in_specs=[per_b((tq, dq)), per_b((tq, LANES)),
                  pl.BlockSpec((1, 1, rows, nblk), lambda bi, gi, pt: (bi, gi, 0, 0)),
                  per_b((rows, 1)), per_b((kv_new.shape[1], w))]
        + [page_spec(i) for i in range(n_pg)],
        out_specs=per_b((tq, dq)),
        scratch_shapes=[pltpu.VMEM((rows, NSA_KV_COLS), F32), pltpu.VMEM((rows, 1), F32),
                        pltpu.VMEM((rows, 1), F32), pltpu.VMEM((rows, NSA_KV_COLS), F32)])
    return pl.pallas_call(
        functools.partial(_slc_paged_body, n_pg=n_pg, tq=tq, l_new=l_new),
        grid_spec=grid_spec, out_shape=jax.ShapeDtypeStruct((b, tq, dq), F32),
        compiler_params=_cparams(("parallel", "arbitrary")), name="slc_paged",
    )(page_table, q, gate, sel_grp, sel_new, kv_new, *([cache_t] * n_pg))


def _cmp_attn_t_body(qt_ref, gt_ref, kc_ref, vct_ref, oct_ref, selt_ref, *, tq, pos0, n_slc, nbp):
    qi = pl.program_id(1)
    ncp = kc_ref.shape[1]
    hd = NSA_HEAD_DIM
    t_row = pos0 + qi * tq + _iota2((1, tq), 1)
    cbias = jnp.where((_iota2((ncp, 1), 0) * CMP_STRIDE + (CMP_LEN - 1)) <= t_row, 0.0, MASKED)
    kc = kc_ref[0].astype(BF16)
    vct = vct_ref[0].astype(BF16)
    gate = jax.nn.sigmoid(gt_ref[0, 0:NSA_HEADS, :])
    ss = _iota2((nbp, ncp), 0) * SLC_BLOCK
    nn = _iota2((nbp, ncp), 1) * CMP_STRIDE
    overlap_t = jnp.where((nn < ss + SLC_BLOCK) & (nn + CMP_LEN > ss), 1.0, 0.0).astype(BF16)
    blk = _iota2((nbp, 1), 0)
    blk_f = blk.astype(F32)
    cur = lax.shift_right_logical(t_row, 6)
    forced = (blk == 0) | (blk == cur) | (blk == cur - 1)
    bonus = jnp.where(forced, FORCE_BONUS, 0.0)
    causal_blk = blk * SLC_BLOCK <= t_row
    n_pick = min(SLC_TOPN, n_slc)

    scores = []
    for k in range(NSA_KV_HEADS):
        kc_k = kc[:, k * hd:(k + 1) * hd]
        vct_k = vct[k * hd:(k + 1) * hd, :]
        hs = [k * NSA_GROUP + g for g in range(NSA_GROUP)]
        q_l = [(qt_ref[0, h * hd:(h + 1) * hd, :] * (hd ** -0.5 * LOG2E)).astype(BF16) for h in hs]
        s_l = [_dot(kc_k, q_h) + cbias for q_h in q_l]
        e_l = [jnp.exp2(s - jnp.maximum(jnp.max(s, axis=0, keepdims=True), NEG_INF)) for s in s_l]
        den_l = [jnp.sum(e, axis=0, keepdims=True) for e in e_l]
        p_l = [e * (1.0 / jnp.where(den > 0.0, den, 1.0)) for e, den in zip(e_l, den_l)]
        o_l = [_dot(vct_k, p.astype(BF16)) for p in p_l]
        for h, o in zip(hs, o_l):
            oct_ref[0, h * hd:(h + 1) * hd, :] = o * gate[h:h + 1, :]
        psum = (p_l[0] + p_l[1]) + (p_l[2] + p_l[3])
        ph, plo = _split2(psum)
        imp = _dot(overlap_t, ph) + _dot(overlap_t, plo)
        score = jnp.where(causal_blk, imp + bonus, NEG_INF)
        scores.append(jnp.where(blk < n_slc, score, -jnp.inf))

    def pick_one(_, work):
        m = jnp.max(work, axis=0, keepdims=True)
        idx = jnp.min(jnp.where(work == m, blk_f, 1e9), axis=0, keepdims=True)
        return jnp.where(blk_f == idx, -jnp.inf, work)

    work = lax.fori_loop(0, n_pick, pick_one, jnp.concatenate(scores, axis=1))
    sel = jnp.where((work == -jnp.inf) & (blk < n_slc), 1.0, 0.0)
    for k in range(NSA_KV_HEADS):
        selt_ref[0, k * nbp:(k + 1) * nbp, :] = sel[:, k * tq:(k + 1) * tq]


def cmp_attn_select_t(qt, gate_t, kc, vct, b, tq, pos0, n_slc):
    nt, dq, _ = qt.shape
    nq = nt // b
    ncp = kc.shape[1]
    nbp = _nbp(n_slc)
    tile = lambda rows: pl.BlockSpec((1, rows, tq), lambda i, j: (i * nq + j, 0, 0))
    return pl.pallas_call(
        functools.partial(_cmp_attn_t_body, tq=tq, pos0=pos0, n_slc=n_slc, nbp=nbp),
        grid=(b, nq),
        in_specs=[tile(dq), tile(LANES), pl.BlockSpec((1, ncp, NSA_KV_COLS), lambda i, j: (i, 0, 0)),
                  pl.BlockSpec((1, NSA_KV_COLS, ncp), lambda i, j: (i, 0, 0))],
        out_specs=[tile(dq), tile(NSA_KV_HEADS * nbp)],
        out_shape=[jax.ShapeDtypeStruct((nt, dq, tq), F32), jax.ShapeDtypeStruct((nt, NSA_KV_HEADS * nbp, tq), F32)],
        compiler_params=_cparams(("parallel", "parallel")), name="cmp_attn_select_t",
    )(qt, gate_t, kc, vct)


def _nsa_attn_t_body(qt_ref, gt_ref, kv_ref, vt_ref, *rest, mode, tq, tk, gate_off, nbp):
    if mode == "slc":
        selt_ref, o_ref, qs_ref, m_ref, l_ref, acc_ref = rest
    else:
        o_ref, qs_ref, m_ref, l_ref, acc_ref = rest
    qi = pl.program_id(1)
    hd = NSA_HEAD_DIM
    nk = kv_ref.shape[1] // tk
    qmin = qi * tq
    t_row = qmin + _iota2((1, tq), 1)
    for r in range(NSA_HEADS):
        qs_ref[r, 0:hd, :] = (qt_ref[0, r * hd:(r + 1) * hd, :] * (hd ** -0.5 * LOG2E)).astype(BF16)
        if mode == "slc":
            k = r // NSA_GROUP
            qs_ref[r, hd:hd + nbp, :] = ((1.0 - selt_ref[0, k * nbp:(k + 1) * nbp, :]) * MASKED).astype(BF16)
    m_ref[...] = jnp.full(m_ref.shape, NEG_INF, F32)
    l_ref[...] = jnp.zeros(l_ref.shape, F32)
    acc_ref[...] = jnp.zeros(acc_ref.shape, F32)
    hi = jnp.minimum(nk - 1, (qmin + tq - 1) // tk)

    heads_per_stage = NSA_HEADS

    def tile(ki, positional):
        k0 = pl.multiple_of(ki * tk, tk)
        krows = kv_ref[0, pl.ds(k0, tk), 0:NSA_KV_COLS].astype(BF16)
        vt = vt_ref[ki].astype(BF16)
        kpos = k0 + _iota2((tk, 1), 0)
        if mode == "slc":
            expand_t = jnp.where(_iota2((tk, nbp), 1) == lax.shift_right_logical(k0 + _iota2((tk, nbp), 0), 6),
                                 1.0, 0.0).astype(BF16)
            cbias = jnp.where(kpos <= t_row, 0.0, MASKED) if positional else None
        else:
            dist = t_row - kpos
            cbias = jnp.where((dist >= 0) & (dist < WINDOW), 0.0, MASKED)
        k_l, vt_l = [], []
        for k in range(NSA_KV_HEADS):
            k_k = krows[:, k * hd:(k + 1) * hd]
            k_l.append(jnp.concatenate([k_k, expand_t], axis=1) if mode == "slc" else k_k)
            vt_l.append(vt[k * hd:(k + 1) * hd, :])
        for r0 in range(0, NSA_HEADS, heads_per_stage):
            rs = list(range(r0, r0 + heads_per_stage))
            s_l = [_dot(k_l[r // NSA_GROUP], qs_ref[r]) for r in rs]
            if cbias is not None:
                s_l = [s + cbias for s in s_l]
            m_old = [m_ref[r] for r in rs]
            m_new = [jnp.maximum(mo, jnp.max(s, axis=0, keepdims=True)) for mo, s in zip(m_old, s_l)]
            p_l = [jnp.exp2(s - mn) for s, mn in zip(s_l, m_new)]
            pv_l = [_dot(vt_l[r // NSA_GROUP], p.astype(BF16)) for r, p in zip(rs, p_l)]
            for r, mo, mn, p, pv in zip(rs, m_old, m_new, p_l, pv_l):
                alpha = jnp.exp2(mo - mn)
                l_ref[r] = alpha * l_ref[r] + jnp.sum(p, axis=0, keepdims=True)
                acc_ref[r] = alpha * acc_ref[r] + pv
                m_ref[r] = mn

    def step(ki, carry):
        tile(ki, mode != "slc")
        return carry

    if mode == "slc":
        lax.fori_loop(0, hi, step, 0)
        tile(hi, True)
    else:
        lax.fori_loop(jnp.maximum(qmin - (WINDOW - 1), 0) // tk, hi + 1, step, 0)
    gate = jax.nn.sigmoid(gt_ref[0, gate_off:gate_off + NSA_HEADS, :])
    for r in range(NSA_HEADS):
        l = l_ref[r]
        o_ref[0, r * hd:(r + 1) * hd, :] = acc_ref[r] * (gate[r:r + 1, :] / jnp.where(l > 0.0, l, 1.0))


def nsa_attn_t(mode, qt, gate_t, kv, vt, selt, b, tq, tk):
    nt, dq, _ = qt.shape
    nq = nt // b
    l = kv.shape[1]
    nk = l // tk
    assert l % tk == 0 and vt.shape == (b * nk, NSA_KV_COLS, tk) and (mode != "slc" or tq == tk)
    tile = lambda rows: pl.BlockSpec((1, rows, tq), lambda i, j: (i * nq + j, 0, 0))
    in_specs = [tile(dq), tile(LANES), pl.BlockSpec((1, l, kv.shape[2]), lambda i, j: (i, 0, 0)),
                pl.BlockSpec((nk, NSA_KV_COLS, tk), lambda i, j: (i, 0, 0))]
    args = [qt, gate_t, kv, vt]
    nbp = 0
    if mode == "slc":
        nbp = selt.shape[1] // NSA_KV_HEADS
        in_specs.append(tile(selt.shape[1]))
        args.append(selt)
    return pl.pallas_call(
        functools.partial(_nsa_attn_t_body, mode=mode, tq=tq, tk=tk,
                          gate_off=NSA_HEADS * (1 if mode == "slc" else 2), nbp=nbp),
        grid=(b, nq), in_specs=in_specs, out_specs=tile(dq),
        out_shape=jax.ShapeDtypeStruct((nt, dq, tq), F32),
        scratch_shapes=[pltpu.VMEM((NSA_HEADS, NSA_HEAD_DIM + nbp, tq), BF16),
                        pltpu.VMEM((NSA_HEADS, 1, tq), F32), pltpu.VMEM((NSA_HEADS, 1, tq), F32),
                        pltpu.VMEM((NSA_HEADS, NSA_HEAD_DIM, tq), F32)],
        compiler_params=_cparams(("parallel", "parallel")), name="nsa_attn_t_" + mode,
    )(*args)


def _prep_nsa_weights(w_in_nsa):
    w = w_in_nsa.astype(BF16)
    dq = NSA_HEADS * NSA_HEAD_DIM
    kvw = 2 * NSA_KV_COLS
    gate = jnp.pad(w[:, dq + 3 * kvw:], ((0, 0), (0, LANES - 3 * NSA_HEADS)))
    return [w[:, :dq], w[:, dq:dq + kvw], w[:, dq + kvw:dq + 2 * kvw], w[:, dq + 2 * kvw:dq + 3 * kvw], gate]


def _kv5(a):
    return a.reshape(a.shape[0], a.shape[1], 2, NSA_KV_HEADS, NSA_HEAD_DIM)


def nsa_prompt(x, p, tm):
    b, l, d = x.shape
    assert l % tm == 0
    wq, wc, ws, ww, wg = p["w_nsa"]
    v_rows = slice(NSA_KV_COLS, 2 * NSA_KV_COLS)
    wts = [wq.T, ws.T[v_rows], ww.T[v_rows], wg.T]
    kvc, kvs, kvw, qt, vst, vwt, gt = norm_proj(x.reshape(b * l, d), p["norm_mix"][1], [wc, ws, ww], tm, wts)
    r3 = lambda a: a.reshape(b, l, a.shape[-1])
    kvc, kvs, kvw = r3(kvc), r3(kvs), r3(kvw)
    lt = nsa_lt_dense(kvc, p["lt_wk"], p["lt_wv"], 256)
    kc, _, vct = cmp_kv(lt, p)
    n_slc = -(-l // SLC_BLOCK)
    o_c, selt = cmp_attn_select_t(qt, gt, kc, vct, b, tm, 0, n_slc)
    o_s = nsa_attn_t("slc", qt, gt, kvs, vst, selt, b, tm, tm)
    o_w = nsa_attn_t("win", qt, gt, kvw, vwt, None, b, tm, tm)
    keep = min(WINDOW, l)
    return [o_c, o_s, o_w], _kv5(kvc), _kv5(kvs), _kv5(kvw[:, l - keep:])


def nsa_sample(x, cache_cmp, cache_slc, past_win, page_table, p, n_pg):
    b, lq, d = x.shape
    tq = 8
    assert lq <= tq
    npool, page = cache_cmp.shape[:2]
    w = 2 * NSA_KV_COLS
    p0 = page_table.shape[1] * page
    xp = _pad_time(x, tq).reshape(b * tq, d)
    q, kvc, kvs, kvw, gate = norm_proj(xp, p["norm_mix"][1], p["w_nsa"], b * tq)
    r3 = lambda a: a.reshape(b, tq, a.shape[-1])
    q, kvc, kvs, kvw, gate = r3(q), r3(kvc), r3(kvs), r3(kvw), r3(gate)
    pages_t = lambda c: c.reshape(npool, page, w).transpose(0, 2, 1)
    lt = nsa_lt_paged(pages_t(cache_cmp), page_table, p["lt_wk"], p["lt_wv"], n_pg)
    kc, vc, _ = cmp_kv(lt, p)
    n_slc = -(-(p0 + lq) // SLC_BLOCK)
    o_c, sel = cmp_attn_select(q, gate, kc, vc, tq, p0, n_slc)
    nbp = sel.shape[2] // NSA_KV_HEADS
    selrows = sel.reshape(b, tq, NSA_KV_HEADS, 1, nbp).transpose(0, 2, 3, 1, 4)
    selrows = jnp.broadcast_to(selrows, (b, NSA_KV_HEADS, NSA_GROUP, tq, nbp)).reshape(b, NSA_HEADS * tq, nbp)
    n_pg_slc = 2 * n_pg if page_table.shape[1] % (2 * n_pg) == 0 else n_pg
    o_s = slc_paged(q, gate, selrows, kvs, pages_t(cache_slc), page_table, n_pg_slc, lq, p0)
    n_past = past_win.shape[1]
    win_all = jnp.concatenate([past_win.reshape(b, n_past, w), kvw[:, :lq]], axis=1)
    nkeys = -(-(n_past + lq) // 8) * 8
    o_w = nsa_attn("win", q, gate, _pad_time(win_all, nkeys), None, tq, nkeys, p0, p0 - n_past)
    keep = min(WINDOW, n_past + lq)
    flat = lambda a: a[:, :lq].reshape(b * lq, a.shape[-1])
    return ([flat(o_c), flat(o_s), flat(o_w)], _kv5(kvc[:, :lq]), _kv5(kvs[:, :lq]),
            _kv5(win_all[:, n_past + lq - keep:]))


def _trunk(x, ssd_state0, ssd_conv0, gdn_state0, gdn_conv0, nsa_fn, p, tm):
    b, l, d = x.shape
    t = b * l
    tm = min(tm, t)
    y_ssd, y_gdn, ssd_state, ssd_conv, gdn_state, gdn_conv = layer0_mixer(
        x, ssd_state0, ssd_conv0, gdn_state0, gdn_conv0, p, tm)
    tm_mlp = 2 * tm if t % (2 * tm) == 0 else tm
    x1 = mix_mlp(x.reshape(t, d), [[y_ssd], [y_gdn]], p["w_out_ab"], p["norm_mlp"][0], p["w_up"][0],
                 p["w_down"][0], None, tm_mlp)
    branches, cmp_rows, slc_rows, win_rows = nsa_fn(x1.reshape(b, l, d))
    y = mix_mlp(x1, [branches], [p["w_out_nsa"]], p["norm_mlp"][1], p["w_up"][1], p["w_down"][1],
                p["norm_final"], tm, transposed=branches[0].ndim == 3)
    return y.reshape(b, l, d), ssd_state, ssd_conv, gdn_state, gdn_conv, cmp_rows, slc_rows, win_rows


def kernel(x_prompt, x_sample, state_ssd, state_ssd_conv, state_gdn, state_gdn_conv, cache_cmp_kv, cache_slc_kv,
           state_win_kv, page_table, norm_mix, norm_mlp, norm_final, w_in_ab, ssd_conv_w, ssd_conv_b, ssd_dt_bias,
           ssd_a_log, ssd_d, ssd_norm_w, gdn_conv_w, gdn_dt_bias, gdn_a_log, gdn_norm_w, w_out_ab, w_in_nsa,
           cmp_pe_k, cmp_w1_k, cmp_b1_k, cmp_w2_k, cmp_pe_v, cmp_w1_v, cmp_b1_v, cmp_w2_v, w_out_nsa, w_up, w_down):
    wo = w_out_ab.astype(BF16)
    p = dict(
        norm_mix=norm_mix, norm_mlp=norm_mlp, norm_final=norm_final, w_ab=_prep_ab_weights(w_in_ab),
        ssd_conv_w=ssd_conv_w, ssd_conv_b=ssd_conv_b, ssd_dt_bias=ssd_dt_bias, ssd_a_log=ssd_a_log, ssd_d=ssd_d,
        ssd_norm_w=ssd_norm_w, gdn_conv_w=gdn_conv_w, gdn_dt_bias=gdn_dt_bias, gdn_a_log=gdn_a_log,
        gdn_norm_w=gdn_norm_w, w_out_ab=[wo[:SSD_INNER], wo[SSD_INNER:]], w_nsa=_prep_nsa_weights(w_in_nsa),
        lt_wk=_lt_weights(cmp_w1_k), lt_wv=_lt_weights(cmp_w1_v),
        cmp_pe_k=cmp_pe_k, cmp_w1_k=cmp_w1_k, cmp_b1_k=cmp_b1_k, cmp_w2_k=cmp_w2_k,
        cmp_pe_v=cmp_pe_v, cmp_w1_v=cmp_w1_v, cmp_b1_v=cmp_b1_v, cmp_w2_v=cmp_w2_v,
        w_out_nsa=w_out_nsa.astype(BF16), w_up=w_up.astype(BF16), w_down=w_down.astype(BF16))
    bp = x_prompt.shape[0]
    zeros = lambda *s: jnp.zeros(s, F32)
    tm = 256
    out_p = _trunk(x_prompt, zeros(bp, SSD_HEADS, SSD_HEADDIM, SSD_STATE), zeros(bp, CONV_W - 1, SSD_CONV_DIM),
                   zeros(bp, GDN_V_HEADS, GDN_HEAD_DIM, GDN_HEAD_DIM), zeros(bp, CONV_W - 1, GDN_CONV_DIM),
                   lambda x1: nsa_prompt(x1, p, tm), p, tm)
    out_s = _trunk(x_sample, state_ssd, state_ssd_conv, state_gdn, state_gdn_conv,
                   lambda x1: nsa_sample(x1, cache_cmp_kv, cache_slc_kv, state_win_kv, page_table, p, 16), p, tm)
    return tuple(v for pair in zip(out_p, out_s) for v in pair)
```
